```python
import math
import jax, jax.numpy as jnp
from jax import lax
import numpy as np

D_MODEL = 2048
BATCH = 4
SEQ = 2048
DEPTH = 4
DEC_BATCH = 128
DEC_SEQ = 4
PAST_LEN = 16384
PAGE_SIZE = 128

N_BRANCH = 4
BRANCH_W = D_MODEL // N_BRANCH
NORM_EPS = 1e-6

S5_GROUP = 16
S5_GROUPS = BRANCH_W // S5_GROUP
S5_STATE = 64

GDN_HEADS = 4
GDN_DH = BRANCH_W // GDN_HEADS
GDN_CONV = 4
GDN_CHUNK = 64

RWKV_DH = 64
RWKV_HEADS = BRANCH_W // RWKV_DH
RWKV_W_LORA = 96
RWKV_A_LORA = 96
RWKV_G_LORA = 256
RWKV_LN_EPS = 64e-5

HGRN_HEADS = 4
HGRN_DH = BRANCH_W // HGRN_HEADS
HGRN_CHUNK = 64
LB_TINY = 1e-30

D_FF = 5632
FFN_CONV = 3

S5_COLS = BRANCH_W
GDN_COLS = 4 * BRANCH_W + 2 * GDN_HEADS
RWKV_COLS = 3 * BRANCH_W + RWKV_W_LORA + RWKV_A_LORA + RWKV_G_LORA
HGRN_COLS = 4 * BRANCH_W
IN_COLS = S5_COLS + GDN_COLS + RWKV_COLS + HGRN_COLS

kernel_name = 'hybrid_s5_gdn_rwkv7_hgrn2_step'

F32 = jnp.float32


def rms_normalize(x):
    xf = x.astype(F32)
    return xf * lax.rsqrt(jnp.mean(xf * xf, axis=-1, keepdims=True) + NORM_EPS)


def rmsnorm(x, g):
    return (rms_normalize(x) * g.astype(F32)).astype(x.dtype)


def l2_normalize(x):
    xf = x.astype(F32)
    return xf * lax.rsqrt(jnp.sum(xf * xf, axis=-1, keepdims=True) + 1e-6)


def masked_exp(mask, z):
    return jnp.where(mask, jnp.exp(jnp.where(mask, z, 0.0)), 0.0)


def causal_dwconv(x, buf, w):
    xp = jnp.concatenate([buf.astype(x.dtype), x], axis=1)
    y = lax.conv_general_dilated(xp, w[:, None, :].astype(x.dtype), (1,), 'VALID',
                                 dimension_numbers=('NWC', 'WIO', 'NWC'),
                                 feature_group_count=x.shape[-1])
    return y, xp[:, xp.shape[1] - (w.shape[0] - 1):]


def to_chunks(a, c):
    b, t = a.shape[0], a.shape[1]
    nc = -(-t // c)
    a = jnp.pad(a, [(0, 0), (0, nc * c - t)] + [(0, 0)] * (a.ndim - 2))
    return jnp.moveaxis(a.reshape((b, nc, c) + a.shape[2:]), 1, 0)


def from_chunks(o, t):
    nc, b, c = o.shape[:3]
    return jnp.moveaxis(o, 0, 1).reshape((b, nc * c) + o.shape[3:])[:, :t]


def s5_branch(u, h0_re, h0_im, a_re, a_im, log_dt, b_re, b_im, c_re, c_im, d_skip, w_glu):
    bsz, t, _ = u.shape
    uf = u.astype(F32)
    ug = uf.reshape(bsz, t, S5_GROUPS, S5_GROUP)
    lam = lax.complex(a_re.astype(F32), a_im.astype(F32))
    dt = jnp.exp(log_dt.astype(F32))[:, None]
    a_bar = jnp.exp(lam * dt)
    b_bar = ((a_bar - 1.0) / lam)[..., None] * lax.complex(b_re.astype(F32), b_im.astype(F32))
    bu = jnp.einsum('gph,btgh->btgp', b_bar, ug.astype(jnp.complex64))
    h0 = lax.complex(h0_re.astype(F32), h0_im.astype(F32))
    bu = bu.at[:, 0].add(a_bar * h0)

    def combine(e1, e2):
        a1, b1 = e1
        a2, b2 = e2
        return a1 * a2, a2 * b1 + b2

    _, h = lax.associative_scan(combine, (jnp.broadcast_to(a_bar, bu.shape), bu), axis=1)
    c = lax.complex(c_re.astype(F32), c_im.astype(F32))
    y = jnp.real(jnp.einsum('ghp,btgp->btgh', c, h)).reshape(bsz, t, BRANCH_W) + d_skip.astype(F32) * uf
    g = jax.nn.gelu(y)
    out = g * jax.nn.sigmoid(g @ w_glu.astype(F32))
    return out, jnp.real(h[:, -1]), jnp.imag(h[:, -1])


def gdn_chunked(q, k, v, beta, g, s0):
    t = q.shape[1]
    c = min(GDN_CHUNK, t)
    nv = v.shape[-1]
    causal = jnp.tril(jnp.ones((c, c), bool))
    strict = jnp.tril(jnp.ones((c, c), bool), -1)
    eye = jnp.eye(c, dtype=F32)

    def body(s, inp):
        qc, kc, vc, bc, gc = inp
        d = jnp.cumsum(gc, axis=1)
        dh = jnp.moveaxis(d, 1, 2)
        decay = masked_exp(causal, dh[..., :, None] - dh[..., None, :])
        kb = kc * bc[..., None]
        m = jnp.where(strict, jnp.einsum('bthk,bshk->bhts', kb, kc) * decay, 0.0)
        rhs = jnp.moveaxis(jnp.concatenate([vc * bc[..., None], kb * jnp.exp(d)[..., None]], axis=-1), 1, 2)
        sol = lax.linalg.triangular_solve(eye + m, rhs, left_side=True, lower=True, unit_diagonal=True)
        u = sol[..., :nv] - jnp.einsum('bhtk,bhkv->bhtv', sol[..., nv:], s)
        attn = jnp.einsum('bthk,bshk->bhts', qc, kc) * decay
        o = (jnp.einsum('bthk,bhkv->bhtv', qc * jnp.exp(d)[..., None], s)
             + jnp.einsum('bhts,bhsv->bhtv', attn, u))
        dl = d[:, -1]
        s = s * jnp.exp(dl)[..., None, None] + jnp.einsum(
            'bthk,bhtv->bhkv', kc * jnp.exp(dl[:, None] - d)[..., None], u)
        return s, jnp.moveaxis(o, 1, 2)

    xs = tuple(to_chunks(a.astype(F32), c) for a in (q, k, v, beta, g))
    s, o = lax.scan(body, s0.astype(F32), xs)
    return from_chunks(o, t), s


def hgrn_chunked(q, k, v, logf, s0):
    t = q.shape[1]
    c = min(HGRN_CHUNK, t)
    causal = jnp.tril(jnp.ones((c, c), bool))[:, :, None, None]

    def body(s, inp):
        qc, kc, vc, lc = inp
        b = jnp.cumsum(lc, axis=1)
        decay = masked_exp(causal, b[:, :, None] - b[:, None, :])
        attn = jnp.einsum('bthk,bshk,btshk->bhts', qc, kc, decay)
        o = (jnp.einsum('bthk,bhkv->bthv', qc * jnp.exp(b), s)
             + jnp.einsum('bhts,bshv->bthv', attn, vc))
        bl = b[:, -1]
        s = s * jnp.exp(bl)[..., None] + jnp.einsum('bthk,bthv->bhkv', kc * jnp.exp(bl[:, None] - b), vc)
        return s, o

    xs = tuple(to_chunks(a.astype(F32), c) for a in (q, k, v, logf))
    s, o = lax.scan(body, s0.astype(F32), xs)
    return from_chunks(o, t), s


def rwkv7_scan(r, w, k, v, kk, a, s0):
    def step(s, inp):
        rt, wt, kt, vt, kkt, at = inp
        sa = jnp.einsum('bhvk,bhk->bhv', s, -kkt)
        s = (s * wt[:, :, None, :] + sa[..., None] * (kkt * at)[:, :, None, :]
             + vt[..., None] * kt[:, :, None, :])
        return s, jnp.einsum('bhvk,bhk->bhv', s, rt)

    xs = tuple(jnp.moveaxis(z.astype(F32), 1, 0) for z in (r, w, k, v, kk, a))
    s, y = lax.scan(step, s0.astype(F32), xs)
    return jnp.moveaxis(y, 0, 1), s


def hybrid_layer(x, s5_re, s5_im, gdn_s, gdn_buf, rwkv_s, rwkv_prev, hgrn_s, ffn_buf, lb, p):
    bsz, t, _ = x.shape
    heads = lambda z, n: z.reshape(bsz, t, n, -1)
    h = rmsnorm(x, p['g_pre_mix'])
    proj = h @ p['w_in']
    u_s5, p_gdn, p_rwkv, p_hgrn = jnp.split(
        proj, [S5_COLS, S5_COLS + GDN_COLS, S5_COLS + GDN_COLS + RWKV_COLS], axis=-1)

    o_a, s5_re_new, s5_im_new = s5_branch(
        u_s5, s5_re, s5_im, p['s5_a_re'], p['s5_a_im'], p['s5_log_dt'], p['s5_b_re'], p['s5_b_im'],
        p['s5_c_re'], p['s5_c_im'], p['s5_d'], p['s5_w_glu'])

    qkv, z, ba = jnp.split(p_gdn, [3 * BRANCH_W, 4 * BRANCH_W], axis=-1)
    qkv, gdn_buf_new = causal_dwconv(qkv, gdn_buf, p['gdn_conv_w'])
    gq, gk, gv = jnp.split(jax.nn.silu(qkv), 3, axis=-1)
    gq = l2_normalize(heads(gq, GDN_HEADS)) * (GDN_DH ** -0.5)
    gk = l2_normalize(heads(gk, GDN_HEADS))
    ba = ba.astype(F32)
    beta = jax.nn.sigmoid(ba[..., :GDN_HEADS])
    g_log = -jnp.exp(p['gdn_a_log'].astype(F32)) * jax.nn.softplus(
        ba[..., GDN_HEADS:] + p['gdn_dt_bias'].astype(F32))
    o, gdn_s_new = gdn_chunked(gq, gk, heads(gv, GDN_HEADS), beta, g_log, gdn_s)
    o_b = (rms_normalize(o) * p['gdn_norm_g'].astype(F32)
           * jax.nn.silu(heads(z, GDN_HEADS).astype(F32))).reshape(bsz, t, BRANCH_W)

    prev = jnp.concatenate([rwkv_prev[:, None].astype(p_rwkv.dtype), p_rwkv[:, :-1]], axis=1)
    pm = (p_rwkv + (prev - p_rwkv) * p['rwkv_mu']).astype(F32)
    r, k, v, wl, al, gl = jnp.split(pm, [BRANCH_W, 2 * BRANCH_W, 3 * BRANCH_W, 3 * BRANCH_W + RWKV_W_LORA,
                                         3 * BRANCH_W + RWKV_W_LORA + RWKV_A_LORA], axis=-1)
    w_log = -jax.nn.softplus(-(p['rwkv_w0'] + jnp.tanh(wl) @ p['rwkv_w2'])) - 0.5
    decay = jnp.exp(-jnp.exp(w_log))
    a = jax.nn.sigmoid(p['rwkv_a0'] + al @ p['rwkv_a2'])
    gate = jax.nn.sigmoid(gl) @ p['rwkv_g2']
    kk = l2_normalize(heads(k * p['rwkv_k_k'], RWKV_HEADS))
    k = k * (1.0 + (a - 1.0) * p['rwkv_k_a'])
    rh, kh, vh = heads(r, RWKV_HEADS), heads(k, RWKV_HEADS), heads(v, RWKV_HEADS)
    y, rwkv_s_new = rwkv7_scan(rh, heads(decay, RWKV_HEADS), kh, vh, kk, heads(a, RWKV_HEADS), rwkv_s)
    mu = jnp.mean(y, axis=-1, keepdims=True)
    var = jnp.mean(jnp.square(y - mu), axis=-1, keepdims=True)
    yn = ((y - mu) * lax.rsqrt(var + RWKV_LN_EPS)).reshape(bsz, t, BRANCH_W) * p['rwkv_ln_g'] + p['rwkv_ln_b']
    bonus = jnp.sum(rh * kh * p['rwkv_r_k'], axis=-1, keepdims=True) * vh
    o_c = (yn + bonus.reshape(bsz, t, BRANCH_W)) * gate
    rwkv_prev_new = p_rwkv[:, -1]

    hq, hf, hi, hg = jnp.split(p_hgrn.astype(F32), 4, axis=-1)
    logf = jnp.logaddexp(jnp.log(jnp.maximum(lb, LB_TINY)), jnp.log1p(-lb) + jax.nn.log_sigmoid(hf))
    kf = (1.0 - lb) * jax.nn.sigmoid(-hf)
    o, hgrn_s_new = hgrn_chunked(heads(jax.nn.silu(hq), HGRN_HEADS), heads(kf, HGRN_HEADS),
                                 heads(hi, HGRN_HEADS), heads(logf, HGRN_HEADS), hgrn_s)
    o_d = (rms_normalize(o) * p['hgrn_norm_g'].astype(F32)
           * jax.nn.silu(heads(hg, HGRN_HEADS))).reshape(bsz, t, BRANCH_W)

    gates = jax.nn.sigmoid((h @ p['w_gate']).astype(F32))
    mix = sum(gates[..., n * D_MODEL:(n + 1) * D_MODEL]
              * (o_n.astype(x.dtype) @ p['w_br'][n]).astype(F32)
              for n, o_n in enumerate((o_a, o_b, o_c, o_d)))
    x = x + rmsnorm(mix.astype(x.dtype) @ p['w_o'], p['g_post_mix'])

    up, ffn_buf_new = causal_dwconv(rmsnorm(x, p['g_pre_ffn']) @ p['w_up'], ffn_buf, p['ffn_conv_w'])
    ua, ub = jnp.split(up, 2, axis=-1)
    x = x + rmsnorm((jax.nn.gelu(ua, approximate=True) * ub) @ p['w_down'], p['g_post_ffn'])
    return x, (s5_re_new, s5_im_new, gdn_s_new, gdn_buf_new, rwkv_s_new, rwkv_prev_new, hgrn_s_new, ffn_buf_new)


def run_trunk(x, states, layers, lb_all):
    new = [[] for _ in states]
    for l in range(DEPTH):
        x, st = hybrid_layer(x, *[s[l] for s in states], lb_all[l], layers[l])
        for lst, s in zip(new, st):
            lst.append(s)
    return x, [jnp.stack(lst) for lst in new]


def setup_inputs(seed: int = 0) -> dict:
    key = jax.random.key(seed)
    ks = iter(jax.random.split(key, 64))
    L = DEPTH

    def nrm(shape, scale):
        return scale * jax.random.normal(next(ks), shape, F32)

    def unif(shape, lo, hi):
        return jax.random.uniform(next(ks), shape, F32, lo, hi)

    def gain(shape):
        return 1.0 + nrm(shape, 0.02)

    gdn_dt = jnp.exp(unif((L, GDN_HEADS), math.log(1e-3), math.log(1e-1)))
    s5_imag = jnp.broadcast_to(math.pi * jnp.arange(S5_STATE, dtype=F32), (L, S5_GROUPS, S5_STATE))
    return {
        'x_prompt': nrm((BATCH, SEQ, D_MODEL), 1.0),
        'x_sample': nrm((DEC_BATCH, DEC_SEQ, D_MODEL), 1.0),
        'state_s5_re': nrm((L, DEC_BATCH, S5_GROUPS, S5_STATE), 0.5),
        'state_s5_im': nrm((L, DEC_BATCH, S5_GROUPS, S5_STATE), 0.5),
        'state_gdn': nrm((L, DEC_BATCH, GDN_HEADS, GDN_DH, GDN_DH), 0.1),
        'state_gdn_conv': nrm((L, DEC_BATCH, GDN_CONV - 1, 3 * BRANCH_W), 1.0),
        'state_rwkv': nrm((L, DEC_BATCH, RWKV_HEADS, RWKV_DH, RWKV_DH), 0.1),
        'state_rwkv_shift': nrm((L, DEC_BATCH, RWKV_COLS), 1.0),
        'state_hgrn': nrm((L, DEC_BATCH, HGRN_HEADS, HGRN_DH, HGRN_DH), 0.5),
        'state_ffn_conv': nrm((L, DEC_BATCH, FFN_CONV - 1, 2 * D_FF), 1.0),
        'g_pre_mix': gain((L, D_MODEL)),
        'w_in': nrm((L, D_MODEL, IN_COLS), D_MODEL ** -0.5),
        'w_gate': nrm((L, D_MODEL, N_BRANCH * D_MODEL), D_MODEL ** -0.5),
        'w_br': nrm((L, N_BRANCH, BRANCH_W, D_MODEL), BRANCH_W ** -0.5),
        'w_o': nrm((L, D_MODEL, D_MODEL), D_MODEL ** -0.5),
        'g_post_mix': gain((L, D_MODEL)),
        's5_a_re': -0.5 + nrm((L, S5_GROUPS, S5_STATE), 0.01),
        's5_a_im': s5_imag + nrm((L, S5_GROUPS, S5_STATE), 0.01),
        's5_log_dt': unif((L, S5_GROUPS), math.log(1e-3), math.log(1e-1)),
        's5_b_re': nrm((L, S5_GROUPS, S5_STATE, S5_GROUP), (2 * S5_GROUP) ** -0.5),
        's5_b_im': nrm((L, S5_GROUPS, S5_STATE, S5_GROUP), (2 * S5_GROUP) ** -0.5),
        's5_c_re': nrm((L, S5_GROUPS, S5_GROUP, S5_STATE), (2 * S5_STATE) ** -0.5),
        's5_c_im': nrm((L, S5_GROUPS, S5_GROUP, S5_STATE), (2 * S5_STATE) ** -0.5),
        's5_d': nrm((L, BRANCH_W), 0.5),
        's5_w_glu': nrm((L, BRANCH_W, BRANCH_W), BRANCH_W ** -0.5),
        'gdn_conv_w': nrm((L, GDN_CONV, 3 * BRANCH_W), GDN_CONV ** -0.5),
        'gdn_a_log': jnp.log(unif((L, GDN_HEADS), 1.0, 16.0)),
        'gdn_dt_bias': gdn_dt + jnp.log(-jnp.expm1(-gdn_dt)),
        'gdn_norm_g': gain((L, GDN_DH)),
        'rwkv_mu': unif((L, RWKV_COLS), 0.0, 1.0),
        'rwkv_w0': unif((L, BRANCH_W), -5.5, -0.5),
        'rwkv_w2': nrm((L, RWKV_W_LORA, BRANCH_W), 0.1 * RWKV_W_LORA ** -0.5),
        'rwkv_a0': nrm((L, BRANCH_W), 0.1),
        'rwkv_a2': nrm((L, RWKV_A_LORA, BRANCH_W), 0.1 * RWKV_A_LORA ** -0.5),
        'rwkv_g2': nrm((L, RWKV_G_LORA, BRANCH_W), RWKV_G_LORA ** -0.5),
        'rwkv_k_k': 0.85 + nrm((L, BRANCH_W), 0.02),
        'rwkv_k_a': 1.0 + nrm((L, BRANCH_W), 0.02),
        'rwkv_r_k': nrm((L, RWKV_HEADS, RWKV_DH), 0.1),
        'rwkv_ln_g': gain((L, BRANCH_W)),
        'rwkv_ln_b': nrm((L, BRANCH_W), 0.02),
        'hgrn_lb_logits': nrm((L, HGRN_HEADS * HGRN_DH), 0.1),
        'hgrn_norm_g': gain((L, HGRN_DH)),
        'g_pre_ffn': gain((L, D_MODEL)),
        'w_up': nrm((L, D_MODEL, 2 * D_FF), D_MODEL ** -0.5),
        'ffn_conv_w': nrm((L, FFN_CONV, 2 * D_FF), FFN_CONV ** -0.5),
        'w_down': nrm((L, D_FF, D_MODEL), D_FF ** -0.5),
        'g_post_ffn': gain((L, D_MODEL)),
    }


def reference(x_prompt, x_sample, state_s5_re, state_s5_im, state_gdn, state_gdn_conv, state_rwkv,
              state_rwkv_shift, state_hgrn, state_ffn_conv, g_pre_mix, w_in, w_gate, w_br, w_o,
              g_post_mix, s5_a_re, s5_a_im, s5_log_dt, s5_b_re, s5_b_im, s5_c_re, s5_c_im, s5_d,
              s5_w_glu, gdn_conv_w, gdn_a_log, gdn_dt_bias, gdn_norm_g, rwkv_mu, rwkv_w0, rwkv_w2,
              rwkv_a0, rwkv_a2, rwkv_g2, rwkv_k_k, rwkv_k_a, rwkv_r_k, rwkv_ln_g, rwkv_ln_b,
              hgrn_lb_logits, hgrn_norm_g, g_pre_ffn, w_up, ffn_conv_w, w_down, g_post_ffn):
    sm = jax.nn.softmax(hgrn_lb_logits.astype(F32), axis=0)
    lb_all = jnp.maximum(jnp.cumsum(sm, axis=0) - sm[0], 0.0)
    layers = [dict(g_pre_mix=g_pre_mix[l], w_in=w_in[l], w_gate=w_gate[l], w_br=w_br[l], w_o=w_o[l],
                   g_post_mix=g_post_mix[l], s5_a_re=s5_a_re[l], s5_a_im=s5_a_im[l],
                   s5_log_dt=s5_log_dt[l], s5_b_re=s5_b_re[l], s5_b_im=s5_b_im[l], s5_c_re=s5_c_re[l],
                   s5_c_im=s5_c_im[l], s5_d=s5_d[l], s5_w_glu=s5_w_glu[l], gdn_conv_w=gdn_conv_w[l],
                   gdn_a_log=gdn_a_log[l], gdn_dt_bias=gdn_dt_bias[l], gdn_norm_g=gdn_norm_g[l],
                   rwkv_mu=rwkv_mu[l], rwkv_w0=rwkv_w0[l], rwkv_w2=rwkv_w2[l], rwkv_a0=rwkv_a0[l],
                   rwkv_a2=rwkv_a2[l], rwkv_g2=rwkv_g2[l], rwkv_k_k=rwkv_k_k[l], rwkv_k_a=rwkv_k_a[l],
                   rwkv_r_k=rwkv_r_k[l], rwkv_ln_g=rwkv_ln_g[l], rwkv_ln_b=rwkv_ln_b[l],
                   hgrn_norm_g=hgrn_norm_g[l], g_pre_ffn=g_pre_ffn[l], w_up=w_up[l],
                   ffn_conv_w=ffn_conv_w[l], w_down=w_down[l], g_post_ffn=g_post_ffn[l])
              for l in range(DEPTH)]
    sample_states = (state_s5_re, state_s5_im, state_gdn, state_gdn_conv, state_rwkv,
                     state_rwkv_shift, state_hgrn, state_ffn_conv)
    nb = x_prompt.shape[0]
    prompt_states = tuple(jnp.zeros((DEPTH, nb) + s.shape[2:], F32) for s in sample_states)
    y_prompt, ps = run_trunk(x_prompt, prompt_states, layers, lb_all)
    y_sample, ss = run_trunk(x_sample, sample_states, layers, lb_all)
    s5_re_p, s5_im_p, gdn_p, gdn_conv_p, rwkv_p, rwkv_shift_p, hgrn_p, ffn_conv_p = ps
    s5_re_s, s5_im_s, gdn_s, gdn_conv_s, rwkv_s, rwkv_shift_s, hgrn_s, ffn_conv_s = ss
    return (y_prompt, y_sample, s5_re_p, s5_re_s, s5_im_p, s5_im_s, gdn_p, gdn_s, gdn_conv_p, gdn_conv_s,
            rwkv_p, rwkv_s, rwkv_shift_p, rwkv_shift_s, hgrn_p, hgrn_s, ffn_conv_p, ffn_conv_s)
```

```python
import functools
import math

import jax
import jax.numpy as jnp
from jax import lax
from jax.experimental import pallas as pl
from jax.experimental.pallas import tpu as pltpu

F32 = jnp.float32
BF16 = jnp.bfloat16

D_MODEL = 2048
DEPTH = 4
N_BRANCH = 4
BRANCH_W = 512
NORM_EPS = 1e-6

S5_GROUP = 16
S5_GROUPS = 32
S5_STATE = 64

GDN_HEADS = 4
GDN_DH = 128
GDN_CONV = 4

RWKV_DH = 64
RWKV_HEADS = 8
RWKV_W_LORA = 96
RWKV_A_LORA = 96
RWKV_G_LORA = 256
RWKV_LN_EPS = 64e-5
RWKV_COLS = 3 * BRANCH_W + RWKV_W_LORA + RWKV_A_LORA + RWKV_G_LORA

HGRN_HEADS = 4
HGRN_DH = 128
HGRN_SUB = 16
LB_TINY = 1e-30
EXP_CLAMP = 80.0

D_FF = 5632
FFN_CONV = 3

LANES = 128
SUBLANES = 8
VMEM_LIMIT_BYTES = 56 * 1024 * 1024

PROJ_COLS = 7168
COL_S5U, COL_GQ, COL_GK, COL_GV, COL_GZ = 0, 1, 2, 3, 4
COL_RR, COL_RK, COL_RV = 5, 6, 7
COL_HQ, COL_HF, COL_HI, COL_HG = 8, 9, 10, 11
COL_RGL = 6144 // 256
COL_GBA, COL_RWL, COL_RAL = 6400 // 128, 6528 // 128, 6656 // 128
RWKV_PAD_COLS = 2048

ROW_TILE = 512
PROMPT_CHUNK = 64
SAMPLE_CHUNK = 8
S5_CHUNK = 32


def _cparams(n_axes):
    return pltpu.CompilerParams(dimension_semantics=("arbitrary",) * n_axes,
                                vmem_limit_bytes=VMEM_LIMIT_BYTES)


def _mm(a, b):
    return jnp.dot(a.astype(BF16), b.astype(BF16), preferred_element_type=F32)


def _mm_nt(a, b):
    return lax.dot_general(a.astype(BF16), b.astype(BF16), (((1,), (1,)), ((), ())),
                           preferred_element_type=F32)


def _mm_tn(a, b):
    return lax.dot_general(a.astype(BF16), b.astype(BF16), (((0,), (0,)), ((), ())),
                           preferred_element_type=F32)


def _split(a):
    hi = a.astype(BF16)
    lo = (a - hi.astype(F32)).astype(BF16)
    return hi, lo


def _mm3(a, b):
    ah, al = _split(a)
    bh, bl = _split(b)
    d = functools.partial(jnp.dot, preferred_element_type=F32)
    return d(ah, bh) + (d(ah, bl) + d(al, bh))


def _rms(x, eps):
    return x * lax.rsqrt(jnp.mean(x * x, axis=-1, keepdims=True) + eps)


def _sigmoid(x):
    return 1.0 / (1.0 + jnp.exp(-x))


def _silu(x):
    return x * _sigmoid(x)


def _softplus(x):
    return jnp.maximum(x, 0.0) + jnp.log1p(jnp.exp(-jnp.abs(x)))


def _gelu_tanh(x):
    return 0.5 * x * (1.0 + jnp.tanh(math.sqrt(2.0 / math.pi) * (x + 0.044715 * (x * x * x))))


def _lower_tri(c, strict=False):
    r = lax.broadcasted_iota(jnp.int32, (c, c), 0)
    s = lax.broadcasted_iota(jnp.int32, (c, c), 1)
    return (r > s) if strict else (r >= s)


def _unit_lower_inverse(a, c):
    eye = (lax.broadcasted_iota(jnp.int32, (c, c), 0)
           == lax.broadcasted_iota(jnp.int32, (c, c), 1)).astype(F32)
    p = -a
    t = eye + p
    k = 2
    while k < c:
        p = _mm3(p, p)
        t = t + _mm3(t, p)
        k *= 2
    return t


def _norm_matmul_kernel(x_ref, g_ref, w_ref, o_ref, h_ref):
    @pl.when(pl.program_id(1) == 0)
    def _():
        h_ref[...] = (_rms(x_ref[...], NORM_EPS) * g_ref[...]).astype(BF16)

    o_ref[...] = jnp.dot(h_ref[...], w_ref[...], preferred_element_type=F32)


def _norm_matmul(x, g, w, *, tn):
    rows, d = x.shape
    n = w.shape[1]
    tm = min(ROW_TILE, rows)
    return pl.pallas_call(
        _norm_matmul_kernel,
        grid=(rows // tm, n // tn),
        in_specs=[pl.BlockSpec((tm, d), lambda i, j: (i, 0)),
                  pl.BlockSpec((1, d), lambda i, j: (0, 0)),
                  pl.BlockSpec((d, tn), lambda i, j: (0, j))],
        out_specs=pl.BlockSpec((tm, tn), lambda i, j: (i, j)),
        out_shape=jax.ShapeDtypeStruct((rows, n), F32),
        scratch_shapes=[pltpu.VMEM((tm, d), BF16)],
        compiler_params=_cparams(2),
        name="norm_matmul",
    )(x, g, w)


def _mix_kernel(x_ref, g_ref, oa_ref, ob_ref, oc_ref, od_ref, wg_ref, wbr_ref, o_ref, h_ref):
    @pl.when(pl.program_id(1) == 0)
    def _():
        h_ref[...] = (_rms(x_ref[...], NORM_EPS) * g_ref[...]).astype(BF16)

    h = h_ref[...]
    acc = None
    for n, br_ref in enumerate((oa_ref, ob_ref, oc_ref, od_ref)):
        gate = _sigmoid(jnp.dot(h, wg_ref[n], preferred_element_type=F32))
        term = gate * jnp.dot(br_ref[...], wbr_ref[n], preferred_element_type=F32)
        acc = term if acc is None else acc + term
    o_ref[...] = acc.astype(BF16)


def _mix(x, g, branches, wg, wbr, *, tn=256):
    rows, d = x.shape
    tm = min(ROW_TILE, rows)
    bw = branches[0].shape[1]
    br_spec = pl.BlockSpec((tm, bw), lambda i, j: (i, 0))
    return pl.pallas_call(
        _mix_kernel,
        grid=(rows // tm, d // tn),
        in_specs=[pl.BlockSpec((tm, d), lambda i, j: (i, 0)),
                  pl.BlockSpec((1, d), lambda i, j: (0, 0)),
                  br_spec, br_spec, br_spec, br_spec,
                  pl.BlockSpec((N_BRANCH, d, tn), lambda i, j: (0, 0, j)),
                  pl.BlockSpec((N_BRANCH, bw, tn), lambda i, j: (0, 0, j))],
        out_specs=pl.BlockSpec((tm, tn), lambda i, j: (i, j)),
        out_shape=jax.ShapeDtypeStruct((rows, d), BF16),
        scratch_shapes=[pltpu.VMEM((tm, d), BF16)],
        compiler_params=_cparams(2),
        name="branch_mix",
    )(x, g, *branches, wg, wbr)


def _matmul_resnorm_kernel(a_ref, w_ref, x_ref, g_ref, o_ref, acc_ref):
    k = pl.program_id(1)

    @pl.when(k == 0)
    def _():
        acc_ref[...] = jnp.zeros_like(acc_ref)

    acc_ref[...] += jnp.dot(a_ref[...], w_ref[...], preferred_element_type=F32)

    @pl.when(k == pl.num_programs(1) - 1)
    def _():
        o_ref[...] = x_ref[...] + _rms(acc_ref[...], NORM_EPS) * g_ref[...]


def _matmul_resnorm(a, w, x, g, *, tk=512):
    rows, kdim = a.shape
    d = w.shape[1]
    tm = min(ROW_TILE, rows)
    return pl.pallas_call(
        _matmul_resnorm_kernel,
        grid=(rows // tm, kdim // tk),
        in_specs=[pl.BlockSpec((tm, tk), lambda i, k: (i, k)),
                  pl.BlockSpec((tk, d), lambda i, k: (k, 0)),
                  pl.BlockSpec((tm, d), lambda i, k: (i, 0)),
                  pl.BlockSpec((1, d), lambda i, k: (0, 0))],
        out_specs=pl.BlockSpec((tm, d), lambda i, k: (i, 0)),
        out_shape=jax.ShapeDtypeStruct((rows, d), F32),
        scratch_shapes=[pltpu.VMEM((tm, d), F32)],
        compiler_params=_cparams(2),
        name="matmul_resnorm",
    )(a, w, x, g)


def _ffn_kernel(x_ref, gpre_ref, wa_ref, wb_ref, cwa_ref, cwb_ref, bufa_ref, bufb_ref, wd_ref,
                gpost_ref, o_ref, nbufa_ref, nbufb_ref, h_ref, acc_ref, sa_ref, sb_ref, carry_ref,
                *, tm, stride, tiles_per_seq):
    i = pl.program_id(0)
    f = pl.program_id(1)
    keep = (FFN_CONV - 1) * stride
    pad = max(SUBLANES, keep)

    @pl.when(f == 0)
    def _():
        h_ref[...] = (_rms(x_ref[...], NORM_EPS) * gpre_ref[...]).astype(BF16)
        acc_ref[...] = jnp.zeros_like(acc_ref)

    h = h_ref[...]
    halves = []
    for idx, (w_ref, cw_ref, buf_ref, nbuf_ref, s_ref) in enumerate(
            ((wa_ref, cwa_ref, bufa_ref, nbufa_ref, sa_ref),
             (wb_ref, cwb_ref, bufb_ref, nbufb_ref, sb_ref))):
        if tiles_per_seq == 1:
            s_ref[pad - keep:pad, :] = buf_ref[0]
        else:
            first = (i % tiles_per_seq) == 0

            @pl.when(first)
            def _():
                s_ref[pad - keep:pad, :] = buf_ref[0]

            @pl.when(jnp.logical_not(first))
            def _():
                s_ref[pad - keep:pad, :] = carry_ref[f, idx]

        s_ref[pad:pad + tm, :] = jnp.dot(h, w_ref[...], preferred_element_type=F32)
        cw = cw_ref[...]
        y = None
        for j in range(FFN_CONV):
            off = pad - (FFN_CONV - 1 - j) * stride
            term = cw[j:j + 1, :] * s_ref[off:off + tm, :]
            y = term if y is None else y + term
        last = s_ref[pad + tm - keep:pad + tm, :]
        nbuf_ref[0] = last
        if tiles_per_seq > 1:
            carry_ref[f, idx] = last
        halves.append(y)

    act = (_gelu_tanh(halves[0]) * halves[1]).astype(BF16)
    acc_ref[...] += jnp.dot(act, wd_ref[...], preferred_element_type=F32)

    @pl.when(f == pl.num_programs(1) - 1)
    def _():
        o_ref[...] = x_ref[...] + _rms(acc_ref[...], NORM_EPS) * gpost_ref[...]


def _ffn(x, gpre, w_up, conv_w, buf, w_down, gpost, *, stride, rows_per_seq, tf=512):
    rows, d = x.shape
    tm = min(ROW_TILE, rows_per_seq)
    tiles_per_seq = rows_per_seq // tm
    nf = D_FF // tf
    keep = (FFN_CONV - 1) * stride
    pad = max(SUBLANES, keep)
    carry_rows = keep if tiles_per_seq > 1 else SUBLANES
    kern = functools.partial(_ffn_kernel, tm=tm, stride=stride, tiles_per_seq=tiles_per_seq)
    buf_a = pl.BlockSpec((1, keep, tf), lambda i, f: (i // tiles_per_seq, 0, f))
    buf_b = pl.BlockSpec((1, keep, tf), lambda i, f: (i // tiles_per_seq, 0, nf + f))
    y, nbuf_a, nbuf_b = pl.pallas_call(
        kern,
        grid=(rows // tm, nf),
        in_specs=[pl.BlockSpec((tm, d), lambda i, f: (i, 0)),
                  pl.BlockSpec((1, d), lambda i, f: (0, 0)),
                  pl.BlockSpec((d, tf), lambda i, f: (0, f)),
                  pl.BlockSpec((d, tf), lambda i, f: (0, nf + f)),
                  pl.BlockSpec((FFN_CONV, tf), lambda i, f: (0, f)),
                  pl.BlockSpec((FFN_CONV, tf), lambda i, f: (0, nf + f)),
                  buf_a, buf_b,
                  pl.BlockSpec((tf, d), lambda i, f: (f, 0)),
                  pl.BlockSpec((1, d), lambda i, f: (0, 0))],
        out_specs=[pl.BlockSpec((tm, d), lambda i, f: (i, 0)),
                   pl.BlockSpec((1, keep, tf), lambda i, f: (i // tiles_per_seq, 0, f)),
                   pl.BlockSpec((1, keep, tf), lambda i, f: (i // tiles_per_seq, 0, f))],
        out_shape=[jax.ShapeDtypeStruct((rows, d), F32),
                   jax.ShapeDtypeStruct((buf.shape[0], keep, D_FF), F32),
                   jax.ShapeDtypeStruct((buf.shape[0], keep, D_FF), F32)],
        scratch_shapes=[pltpu.VMEM((tm, d), BF16),
                        pltpu.VMEM((tm, d), F32),
                        pltpu.VMEM((pad + tm, tf), F32),
                        pltpu.VMEM((pad + tm, tf), F32),
                        pltpu.VMEM((nf, 2, carry_rows, tf), F32)],
        compiler_params=_cparams(2),
        name="conv_ffn",
    )(x, gpre, w_up, w_up, conv_w, conv_w, buf, buf, w_down, gpost)
    return y, jnp.concatenate([nbuf_a, nbuf_b], axis=-1)


def _s5_kernel(u_ref, h0_ref, bst_ref, kst_ref, cst_ref, apow_ref, y_ref, hfin_ref, e_ref,
               *, n_seq, n_chunks):
    rows = n_seq * n_chunks
    u = u_ref[0]
    h0 = h0_ref[0]
    half = S5_STATE

    def cmul(k, x):
        return x * apow_ref[0, 2 * k:2 * k + 1, :] + pltpu.roll(x, half, 1) * apow_ref[0, 2 * k + 1:2 * k + 2, :]

    e = _mm3(u, bst_ref[0]) + cmul(0, h0)
    if n_chunks > 1:
        j = lax.broadcasted_iota(jnp.int32, (rows, 2 * half), 0) % n_chunks
        k, sh = 0, 1
        while sh < n_chunks:
            e = e + jnp.where(j >= sh, cmul(k, pltpu.roll(e, sh, 0)), 0.0)
            k, sh = k + 1, sh * 2
        e_ref[...] = e
        hfin_ref[0] = e_ref[pl.ds(n_chunks - 1, n_seq, stride=n_chunks), :]
        h_start = jnp.where(j >= 1, pltpu.roll(e, 1, 0), 0.0) + h0
    else:
        hfin_ref[0] = e
        h_start = h0
    y_ref[0] = _mm3(u, kst_ref[0]) + _mm3(h_start, cst_ref[0])


def _s5_scan(u_g, h0_g, bst, kst, cst, apow, *, n_seq, n_chunks):
    groups, rows, cw = u_g.shape
    kern = functools.partial(_s5_kernel, n_seq=n_seq, n_chunks=n_chunks)
    spec3 = lambda a: pl.BlockSpec((1,) + a.shape[1:], lambda g: (g, 0, 0))
    return pl.pallas_call(
        kern,
        grid=(groups,),
        in_specs=[spec3(u_g), spec3(h0_g), spec3(bst), spec3(kst), spec3(cst), spec3(apow)],
        out_specs=[pl.BlockSpec((1, rows, cw), lambda g: (g, 0, 0)),
                   pl.BlockSpec((1, n_seq, 2 * S5_STATE), lambda g: (g, 0, 0))],
        out_shape=[jax.ShapeDtypeStruct((groups, rows, cw), F32),
                   jax.ShapeDtypeStruct((groups, n_seq, 2 * S5_STATE), F32)],
        scratch_shapes=[pltpu.VMEM((rows, 2 * S5_STATE), F32)],
        compiler_params=_cparams(1),
        name="s5_scan",
    )(u_g, h0_g, bst, kst, cst, apow)


def _s5_glu_kernel(y_ref, u_ref, d_ref, w_ref, o_ref):
    g = jax.nn.gelu(y_ref[...] + d_ref[...] * u_ref[...], approximate=True)
    o_ref[...] = (g * _sigmoid(_mm(g, w_ref[...]))).astype(BF16)


def _s5_glu(y, u, d, w_glu):
    rows, bw = y.shape
    tm = min(ROW_TILE, rows)
    row_spec = pl.BlockSpec((tm, bw), lambda i: (i, 0))
    return pl.pallas_call(
        _s5_glu_kernel,
        grid=(rows // tm,),
        in_specs=[row_spec, row_spec,
                  pl.BlockSpec((1, bw), lambda i: (0, 0)),
                  pl.BlockSpec((bw, bw), lambda i: (0, 0))],
        out_specs=row_spec,
        out_shape=jax.ShapeDtypeStruct((rows, bw), BF16),
        compiler_params=_cparams(1),
        name="s5_glu",
    )(y, u, d, w_glu)


def _s5_weights(a_re, a_im, log_dt, b_re, b_im, c_re, c_im, c):
    lam = lax.complex(a_re.astype(F32), a_im.astype(F32))
    ldt = lam * jnp.exp(log_dt.astype(F32))[:, None]
    a_bar = jnp.exp(ldt)
    b_bar = ((a_bar - 1.0) / lam)[..., None] * lax.complex(b_re.astype(F32), b_im.astype(F32))
    cm = lax.complex(c_re.astype(F32), c_im.astype(F32))
    tau = jnp.arange(c + 1, dtype=F32)
    apw = jnp.exp(ldt[None] * tau[:, None, None])
    bst = apw[:c][::-1][:, :, :, None] * b_bar[None]
    bst = jnp.transpose(bst, (1, 0, 3, 2)).reshape(S5_GROUPS, c * S5_GROUP, S5_STATE)
    bst = jnp.concatenate([jnp.real(bst), jnp.imag(bst)], axis=-1)
    kt = jnp.real(jnp.einsum('ghp,tgp,gpk->tgkh', cm, apw[:c], b_bar))
    s_idx = jnp.arange(c)[:, None]
    t_idx = jnp.arange(c)[None, :]
    kst = jnp.where((t_idx >= s_idx)[:, :, None, None, None],
                    kt[jnp.clip(t_idx - s_idx, 0, c - 1)], 0.0)
    kst = jnp.transpose(kst, (2, 0, 3, 1, 4)).reshape(S5_GROUPS, c * S5_GROUP, c * S5_GROUP)
    ca = cm[None] * apw[1:c + 1][:, :, None, :]
    ca = jnp.transpose(ca, (1, 3, 0, 2)).reshape(S5_GROUPS, S5_STATE, c * S5_GROUP)
    cst = jnp.concatenate([jnp.real(ca), -jnp.imag(ca)], axis=1)
    rows = []
    k = 0
    while True:
        p = jnp.exp(ldt * float(c * 2 ** k))
        rows.append(jnp.concatenate([jnp.real(p), jnp.real(p)], axis=-1))
        rows.append(jnp.concatenate([-jnp.imag(p), jnp.imag(p)], axis=-1))
        k += 1
        if c * 2 ** k > 4096:
            break
    apow = jnp.stack(rows, axis=1)
    return bst, kst, cst, apow


def _s5_branch(u3, h0_re, h0_im, sw, d_skip, w_glu, *, c):
    n_seq, t, _ = u3.shape
    n_chunks = t // c
    bst, kst, cst, apow = sw
    u_g = u3.reshape(n_seq, n_chunks, c, S5_GROUPS, S5_GROUP)
    u_g = jnp.transpose(u_g, (3, 0, 1, 2, 4)).reshape(S5_GROUPS, n_seq * n_chunks, c * S5_GROUP)
    h0 = jnp.concatenate([h0_re, h0_im], axis=-1)
    h0 = jnp.transpose(h0, (1, 0, 2))
    h0_g = jnp.zeros((S5_GROUPS, n_seq, n_chunks, 2 * S5_STATE), F32).at[:, :, 0].set(h0)
    h0_g = h0_g.reshape(S5_GROUPS, n_seq * n_chunks, 2 * S5_STATE)
    y_g, hfin = _s5_scan(u_g, h0_g, bst, kst, cst, apow, n_seq=n_seq, n_chunks=n_chunks)
    y = y_g.reshape(S5_GROUPS, n_seq, n_chunks, c, S5_GROUP)
    y = jnp.transpose(y, (1, 2, 3, 0, 4)).reshape(n_seq * t, BRANCH_W)
    o = _s5_glu(y, u3.reshape(n_seq * t, BRANCH_W), d_skip, w_glu)
    hfin = jnp.transpose(hfin, (1, 0, 2))
    return o.reshape(n_seq, t, BRANCH_W), hfin[..., :S5_STATE], hfin[..., S5_STATE:]


def _gdn_kernel(pq_ref, pk_ref, pv_ref, pz_ref, pba_ref, cw_ref, cbuf_ref, alog_ref, dtb_ref,
                ng_ref, s0_ref, o_ref, sout_ref, cout_ref, s_ref, conv_ref, *, c, tv):
    i = pl.program_id(1)
    keep = GDN_CONV - 1

    @pl.when(i == 0)
    def _():
        s_ref[...] = s0_ref[0]
        conv_ref[SUBLANES - keep:SUBLANES, :] = cbuf_ref[0]

    conv_ref[SUBLANES:SUBLANES + c, :] = jnp.concatenate([pq_ref[0], pk_ref[0], pv_ref[0]], axis=1)
    cw = cw_ref[...]
    y = None
    for j in range(GDN_CONV):
        off = SUBLANES - keep + j
        term = cw[j:j + 1, :] * conv_ref[off:off + c, :]
        y = term if y is None else y + term
    last = conv_ref[SUBLANES + tv - keep:SUBLANES + tv, :]
    conv_ref[SUBLANES - keep:SUBLANES, :] = last
    cout_ref[0] = last
    act = _silu(y)

    ba = pba_ref[0]
    beta_all = _sigmoid(ba)
    g_all = -jnp.exp(alog_ref[...]) * _softplus(ba + dtb_ref[...])
    if tv < c:
        valid = lax.broadcasted_iota(jnp.int32, (c, LANES), 0) < tv
        beta_all = jnp.where(valid, beta_all, 0.0)
        g_all = jnp.where(valid, g_all, 0.0)
    causal = _lower_tri(c)
    strict = _lower_tri(c, strict=True)
    d_all = _mm3(causal.astype(F32), g_all)
    dt_all = d_all.T
    ng = ng_ref[...]
    z = pz_ref[0]

    outs = []
    for h in range(GDN_HEADS):
        sl = slice(h * GDN_DH, (h + 1) * GDN_DH)
        q = act[:, sl]
        k = act[:, BRANCH_W + h * GDN_DH:BRANCH_W + (h + 1) * GDN_DH]
        v = act[:, 2 * BRANCH_W + h * GDN_DH:2 * BRANCH_W + (h + 1) * GDN_DH]
        q = q * lax.rsqrt(jnp.sum(q * q, axis=-1, keepdims=True) + 1e-6) * (GDN_DH ** -0.5)
        k = k * lax.rsqrt(jnp.sum(k * k, axis=-1, keepdims=True) + 1e-6)
        beta = beta_all[:, h:h + 1]
        d = d_all[:, GDN_HEADS + h:GDN_HEADS + h + 1]
        d_row = dt_all[GDN_HEADS + h:GDN_HEADS + h + 1, :]
        decay = jnp.where(causal, jnp.exp(jnp.where(causal, d - d_row, 0.0)), 0.0)
        kb = k * beta
        m = jnp.where(strict, _mm_nt(kb, k) * decay, 0.0)
        attn = _mm_nt(q, k) * decay
        t_inv = _unit_lower_inverse(m, c)
        ed = jnp.exp(d)
        sol = _mm3(t_inv, jnp.concatenate([v * beta, kb * ed], axis=1))
        s = s_ref[h]
        u = sol[:, :GDN_DH] - _mm(sol[:, GDN_DH:], s)
        o = _mm(q * ed, s) + _mm(attn, u)
        dl = d[c - 1:c, :]
        s_ref[h] = s * jnp.exp(dl) + _mm_tn(k * jnp.exp(dl - d), u)
        outs.append(_rms(o, NORM_EPS) * ng * _silu(z[:, sl]))
    o_ref[0] = jnp.concatenate(outs, axis=1).astype(BF16)

    @pl.when(i == pl.num_programs(1) - 1)
    def _():
        sout_ref[0] = s_ref[...]


def _gdn(proj3, conv_w, cbuf, alog, dtb, ng, s0, *, c, tv):
    n_seq, t, _ = proj3.shape
    n_chunks = t // c
    kern = functools.partial(_gdn_kernel, c=c, tv=tv)
    col = lambda blk, w: pl.BlockSpec((1, c, w), lambda b, i: (b, i, blk))
    const2 = lambda a: pl.BlockSpec(a.shape, lambda b, i: (0, 0))
    state_spec = pl.BlockSpec((1, GDN_HEADS, GDN_DH, GDN_DH), lambda b, i: (b, 0, 0, 0))
    cbuf_spec = pl.BlockSpec((1, GDN_CONV - 1, 3 * BRANCH_W), lambda b, i: (b, 0, 0))
    return pl.pallas_call(
        kern,
        grid=(n_seq, n_chunks),
        in_specs=[col(COL_GQ, BRANCH_W), col(COL_GK, BRANCH_W), col(COL_GV, BRANCH_W),
                  col(COL_GZ, BRANCH_W), col(COL_GBA, LANES),
                  const2(conv_w), cbuf_spec, const2(alog), const2(dtb), const2(ng), state_spec],
        out_specs=[pl.BlockSpec((1, c, BRANCH_W), lambda b, i: (b, i, 0)), state_spec, cbuf_spec],
        out_shape=[jax.ShapeDtypeStruct((n_seq, t, BRANCH_W), BF16),
                   jax.ShapeDtypeStruct(s0.shape, F32),
                   jax.ShapeDtypeStruct(cbuf.shape, F32)],
        scratch_shapes=[pltpu.VMEM((GDN_HEADS, GDN_DH, GDN_DH), F32),
                        pltpu.VMEM((SUBLANES + c, 3 * BRANCH_W), F32)],
        compiler_params=_cparams(2),
        name="gdn",
    )(proj3, proj3, proj3, proj3, proj3, conv_w, cbuf, alog, dtb, ng, s0)


def _rwkv_kernel(pr_ref, pk_ref, pv_ref, pwl_ref, pal_ref, pgl_ref, prev_ref, mu_ref, w0_ref, w2_ref,
                 a0_ref, a2_ref, g2_ref, kk_ref, ka_ref, rk_ref, lng_ref, lnb_ref, s0_ref,
                 o_ref, sout_ref, pout_ref, s_ref, p_ref, *, c, tv):
    i = pl.program_id(1)

    @pl.when(i == 0)
    def _():
        s_ref[...] = s0_ref[0]
        p_ref[SUBLANES - 1:SUBLANES, :] = prev_ref[0]

    p = jnp.concatenate([pr_ref[0], pk_ref[0], pv_ref[0], pwl_ref[0], pal_ref[0], pgl_ref[0]], axis=1)
    p_ref[SUBLANES:SUBLANES + c, :] = p
    prev = p_ref[SUBLANES - 1:SUBLANES - 1 + c, :]
    last = p_ref[SUBLANES + tv - 1:SUBLANES + tv, :]
    p_ref[SUBLANES - 1:SUBLANES, :] = last
    pout_ref[0] = last
    pm = p + (prev - p) * mu_ref[...]
    r = pm[:, :BRANCH_W]
    k = pm[:, BRANCH_W:2 * BRANCH_W]
    v = pm[:, 2 * BRANCH_W:3 * BRANCH_W]
    wl = pm[:, 3 * BRANCH_W:3 * BRANCH_W + LANES]
    al = pm[:, 3 * BRANCH_W + LANES:3 * BRANCH_W + 2 * LANES]
    gl = pm[:, 3 * BRANCH_W + 2 * LANES:]
    w_log = -_softplus(-(w0_ref[...] + _mm(jnp.tanh(wl), w2_ref[...]))) - 0.5
    lw = -jnp.exp(w_log)
    a = _sigmoid(a0_ref[...] + _mm(al, a2_ref[...]))
    gate = _mm(_sigmoid(gl), g2_ref[...])
    kk_un = k * kk_ref[...]
    k2 = k * (1.0 + (a - 1.0) * ka_ref[...])
    if tv < c:
        valid = lax.broadcasted_iota(jnp.int32, (c, BRANCH_W), 0) < tv
        lw = jnp.where(valid, lw, 0.0)
        kk_un = jnp.where(valid, kk_un, 0.0)
        k2 = jnp.where(valid, k2, 0.0)
    causal = _lower_tri(c)
    strict = _lower_tri(c, strict=True)
    g_cum = _mm3(causal.astype(F32), lw)
    e_pos = jnp.exp(g_cum)
    e_neg = jnp.exp(-g_cum)
    e_prev = jnp.exp(g_cum - lw)
    e_last = e_pos[c - 1:c, :]
    rk_w = rk_ref[...]
    lng = lng_ref[...]
    lnb = lnb_ref[...]

    outs = []
    for h in range(RWKV_HEADS):
        sl = slice(h * RWKV_DH, (h + 1) * RWKV_DH)
        kkh = kk_un[:, sl]
        kkh = kkh * lax.rsqrt(jnp.sum(kkh * kkh, axis=-1, keepdims=True) + 1e-6)
        alpha_hat = kkh * a[:, sl] * e_neg[:, sl]
        k_hat = k2[:, sl] * e_neg[:, sl]
        kap = kkh * e_prev[:, sl]
        r_t = r[:, sl] * e_pos[:, sl]
        vh = v[:, sl]
        s = s_ref[h]
        a_m = jnp.where(strict, _mm_nt(kap, alpha_hat), 0.0)
        b_m = jnp.where(strict, _mm_nt(kap, k_hat), 0.0)
        t_inv = _unit_lower_inverse(a_m, c)
        u = _mm3(t_inv, _mm_nt(kap, s) + _mm(b_m, vh))
        y = (_mm_nt(r_t, s) + _mm(jnp.where(causal, _mm_nt(r_t, k_hat), 0.0), vh)
             - _mm(jnp.where(causal, _mm_nt(r_t, alpha_hat), 0.0), u))
        el = e_last[:, sl]
        s_ref[h] = s * el + _mm_tn(vh, k_hat * el) - _mm_tn(u, alpha_hat * el)
        mu = jnp.mean(y, axis=-1, keepdims=True)
        yc = y - mu
        var = jnp.mean(yc * yc, axis=-1, keepdims=True)
        yn = yc * lax.rsqrt(var + RWKV_LN_EPS) * lng[:, sl] + lnb[:, sl]
        bonus = jnp.sum(r[:, sl] * k2[:, sl] * rk_w[:, sl], axis=-1, keepdims=True) * vh
        outs.append(yn + bonus)
    o_ref[0] = (jnp.concatenate(outs, axis=1) * gate).astype(BF16)

    @pl.when(i == pl.num_programs(1) - 1)
    def _():
        sout_ref[0] = s_ref[...]


def _rwkv(proj3, prev, mu, w0, w2, a0, a2, g2, k_k, k_a, r_k, ln_g, ln_b, s0, *, c, tv):
    n_seq, t, _ = proj3.shape
    n_chunks = t // c
    kern = functools.partial(_rwkv_kernel, c=c, tv=tv)
    col = lambda blk, w: pl.BlockSpec((1, c, w), lambda b, i: (b, i, blk))
    const2 = lambda a: pl.BlockSpec(a.shape, lambda b, i: (0, 0))
    state_spec = pl.BlockSpec((1, RWKV_HEADS, RWKV_DH, RWKV_DH), lambda b, i: (b, 0, 0, 0))
    prev_spec = pl.BlockSpec((1, 1, RWKV_PAD_COLS), lambda b, i: (b, 0, 0))
    return pl.pallas_call(
        kern,
        grid=(n_seq, n_chunks),
        in_specs=[col(COL_RR, BRANCH_W), col(COL_RK, BRANCH_W), col(COL_RV, BRANCH_W),
                  col(COL_RWL, LANES), col(COL_RAL, LANES), col(COL_RGL, 2 * LANES),
                  prev_spec, const2(mu), const2(w0), const2(w2), const2(a0), const2(a2), const2(g2),
                  const2(k_k), const2(k_a), const2(r_k), const2(ln_g), const2(ln_b), state_spec],
        out_specs=[pl.BlockSpec((1, c, BRANCH_W), lambda b, i: (b, i, 0)), state_spec, prev_spec],
        out_shape=[jax.ShapeDtypeStruct((n_seq, t, BRANCH_W), BF16),
                   jax.ShapeDtypeStruct(s0.shape, F32),
                   jax.ShapeDtypeStruct(prev.shape, F32)],
        scratch_shapes=[pltpu.VMEM((RWKV_HEADS, RWKV_DH, RWKV_DH), F32),
                        pltpu.VMEM((SUBLANES + c, RWKV_PAD_COLS), F32)],
        compiler_params=_cparams(2),
        name="rwkv7",
    )(proj3, proj3, proj3, proj3, proj3, proj3, prev, mu, w0, w2, a0, a2, g2, k_k, k_a, r_k,
      ln_g, ln_b, s0)


def _hgrn_kernel(pq_ref, pf_ref, pi_ref, pg_ref, lb_ref, ng_ref, s0_ref, o_ref, sout_ref, st_ref,
                 *, c, tv):
    i = pl.program_id(1)

    @pl.when(i == 0)
    def _():
        for h in range(HGRN_HEADS):
            st_ref[h] = s0_ref[0, h].T

    lb = lb_ref[...]
    hf = pf_ref[0]
    log_sig = jnp.minimum(hf, 0.0) - jnp.log1p(jnp.exp(-jnp.abs(hf)))
    x1 = jnp.log(jnp.maximum(lb, LB_TINY))
    x2 = jnp.log1p(-lb) + log_sig
    hi_ = jnp.maximum(x1, x2)
    logf = hi_ + jnp.log1p(jnp.exp(-jnp.abs(x1 - x2)))
    kf = (1.0 - lb) * _sigmoid(-hf)
    q = _silu(pq_ref[0])
    v = pi_ref[0]
    if tv < c:
        valid = lax.broadcasted_iota(jnp.int32, (c, BRANCH_W), 0) < tv
        logf = jnp.where(valid, logf, 0.0)
        kf = jnp.where(valid, kf, 0.0)
    sub = min(HGRN_SUB, c)
    n_sub = c // sub
    r_i = lax.broadcasted_iota(jnp.int32, (c, c), 0)
    s_i = lax.broadcasted_iota(jnp.int32, (c, c), 1)
    blk_tri = ((r_i >= s_i) & (r_i // sub == s_i // sub)).astype(F32)
    b_all = _mm3(blk_tri, logf)
    causal = _lower_tri(sub)
    mid = sub // 2 - 1
    ng = ng_ref[...]
    gate = _silu(pg_ref[0])

    outs = [[] for _ in range(HGRN_HEADS)]
    for j in range(n_sub):
        rows = slice(j * sub, (j + 1) * sub)
        for h in range(HGRN_HEADS):
            sl = slice(h * HGRN_DH, (h + 1) * HGRN_DH)
            b = b_all[rows, sl]
            qh, kh, vh = q[rows, sl], kf[rows, sl], v[rows, sl]
            b_mid = b[mid:mid + 1, :]
            b_last = b[sub - 1:sub, :]
            q_in = qh * jnp.exp(jnp.minimum(b - b_mid, EXP_CLAMP))
            k_in = kh * jnp.exp(jnp.minimum(b_mid - b, EXP_CLAMP))
            attn = jnp.where(causal, _mm_nt(q_in, k_in), 0.0)
            st = st_ref[h]
            o = _mm_nt(qh * jnp.exp(b), st) + _mm(attn, vh)
            st_ref[h] = st * jnp.exp(b_last) + _mm_tn(vh, kh * jnp.exp(b_last - b))
            outs[h].append(o)
    cols = []
    for h in range(HGRN_HEADS):
        sl = slice(h * HGRN_DH, (h + 1) * HGRN_DH)
        o = outs[h][0] if n_sub == 1 else jnp.concatenate(outs[h], axis=0)
        cols.append(_rms(o, NORM_EPS) * ng * gate[:, sl])
    o_ref[0] = jnp.concatenate(cols, axis=1).astype(BF16)

    @pl.when(i == pl.num_programs(1) - 1)
    def _():
        for h in range(HGRN_HEADS):
            sout_ref[0, h] = st_ref[h].T


def _hgrn(proj3, lb, ng, s0, *, c, tv):
    n_seq, t, _ = proj3.shape
    n_chunks = t // c
    kern = functools.partial(_hgrn_kernel, c=c, tv=tv)
    col = lambda blk: pl.BlockSpec((1, c, BRANCH_W), lambda b, i: (b, i, blk))
    const2 = lambda a: pl.BlockSpec(a.shape, lambda b, i: (0, 0))
    state_spec = pl.BlockSpec((1, HGRN_HEADS, HGRN_DH, HGRN_DH), lambda b, i: (b, 0, 0, 0))
    return pl.pallas_call(
        kern,
        grid=(n_seq, n_chunks),
        in_specs=[col(COL_HQ), col(COL_HF), col(COL_HI), col(COL_HG), const2(lb), const2(ng), state_spec],
        out_specs=[pl.BlockSpec((1, c, BRANCH_W), lambda b, i: (b, i, 0)), state_spec],
        out_shape=[jax.ShapeDtypeStruct((n_seq, t, BRANCH_W), BF16),
                   jax.ShapeDtypeStruct(s0.shape, F32)],
        scratch_shapes=[pltpu.VMEM((HGRN_HEADS, HGRN_DH, HGRN_DH), F32)],
        compiler_params=_cparams(2),
        name="hgrn2",
    )(proj3, proj3, proj3, proj3, lb, ng, s0)


def _pad_cols(a, width):
    return jnp.pad(a, [(0, 0)] * (a.ndim - 1) + [(0, width - a.shape[-1])])


def _rwkv_cols_to_padded(a):
    o = 3 * BRANCH_W
    return jnp.concatenate([a[..., :o],
                            _pad_cols(a[..., o:o + RWKV_W_LORA], LANES),
                            _pad_cols(a[..., o + RWKV_W_LORA:o + RWKV_W_LORA + RWKV_A_LORA], LANES),
                            a[..., o + RWKV_W_LORA + RWKV_A_LORA:]], axis=-1)


def _rwkv_cols_from_padded(a):
    o = 3 * BRANCH_W
    return jnp.concatenate([a[..., :o], a[..., o:o + RWKV_W_LORA],
                            a[..., o + LANES:o + LANES + RWKV_A_LORA], a[..., o + 2 * LANES:]], axis=-1)


def _layout_w_in(w):
    s5 = w[:, :512]
    gdn = w[:, 512:2560]
    ba = w[:, 2560:2568]
    rwkv = w[:, 2568:4552]
    hgrn = w[:, 4552:6600]
    rp = _rwkv_cols_to_padded(rwkv)
    out = jnp.concatenate([s5, gdn, rp[:, :1536], hgrn, rp[:, 1792:2048], _pad_cols(ba, LANES),
                           rp[:, 1536:1664], rp[:, 1664:1792]], axis=-1)
    return _pad_cols(out, PROJ_COLS)


def _lane_vec(a, offset):
    return jnp.zeros((1, LANES), F32).at[0, offset:offset + a.shape[0]].set(a.astype(F32))


def _prep_layer(l, w):
    row = lambda a: a.astype(F32).reshape(1, -1)
    p = dict(
        g_pre_mix=row(w['g_pre_mix'][l]),
        w_in=_layout_w_in(w['w_in'][l]).astype(BF16),
        w_gate=jnp.transpose(w['w_gate'][l].reshape(D_MODEL, N_BRANCH, D_MODEL), (1, 0, 2)).astype(BF16),
        w_br=w['w_br'][l].astype(BF16),
        w_o=w['w_o'][l].astype(BF16),
        g_post_mix=row(w['g_post_mix'][l]),
        s5_d=row(w['s5_d'][l]),
        s5_w_glu=w['s5_w_glu'][l].astype(BF16),
        gdn_conv_w=w['gdn_conv_w'][l].astype(F32),
        gdn_alog=_lane_vec(w['gdn_a_log'][l], GDN_HEADS),
        gdn_dtb=_lane_vec(w['gdn_dt_bias'][l], GDN_HEADS),
        gdn_norm_g=row(w['gdn_norm_g'][l]),
        rwkv_mu=_rwkv_cols_to_padded(row(w['rwkv_mu'][l])),
        rwkv_w0=row(w['rwkv_w0'][l]),
        rwkv_w2=jnp.pad(w['rwkv_w2'][l], ((0, LANES - RWKV_W_LORA), (0, 0))).astype(BF16),
        rwkv_a0=row(w['rwkv_a0'][l]),
        rwkv_a2=jnp.pad(w['rwkv_a2'][l], ((0, LANES - RWKV_A_LORA), (0, 0))).astype(BF16),
        rwkv_g2=w['rwkv_g2'][l].astype(BF16),
        rwkv_k_k=row(w['rwkv_k_k'][l]),
        rwkv_k_a=row(w['rwkv_k_a'][l]),
        rwkv_r_k=row(w['rwkv_r_k'][l]),
        rwkv_ln_g=row(w['rwkv_ln_g'][l]),
        rwkv_ln_b=row(w['rwkv_ln_b'][l]),
        hgrn_norm_g=row(w['hgrn_norm_g'][l]),
        g_pre_ffn=row(w['g_pre_ffn'][l]),
        w_up=w['w_up'][l].astype(BF16),
        ffn_conv_w=w['ffn_conv_w'][l].astype(F32),
        w_down=w['w_down'][l].astype(BF16),
        g_post_ffn=row(w['g_post_ffn'][l]),
    )
    s5_args = (w['s5_a_re'][l], w['s5_a_im'][l], w['s5_log_dt'][l], w['s5_b_re'][l], w['s5_b_im'][l],
               w['s5_c_re'][l], w['s5_c_im'][l])
    p['s5_prompt'] = _s5_weights(*s5_args, S5_CHUNK)
    p['s5_sample'] = _s5_weights(*s5_args, 4)
    return p


def _layer(x, st, p, lb, *, n_seq, t, time_major):
    s5_re, s5_im, gdn_s, gdn_buf, rwkv_s, rwkv_prev, hgrn_s, ffn_buf = st
    if time_major:
        c, tp = SAMPLE_CHUNK, SAMPLE_CHUNK
        to_seq = lambda a: jnp.transpose(
            jnp.pad(a.reshape(t, n_seq, a.shape[-1]), ((0, tp - t), (0, 0), (0, 0))), (1, 0, 2))
        from_seq = lambda a: jnp.transpose(a[:, :t], (1, 0, 2)).reshape(t * n_seq, a.shape[-1])
    else:
        c, tp = PROMPT_CHUNK, t
        to_seq = lambda a: a.reshape(n_seq, t, a.shape[-1])
        from_seq = lambda a: a.reshape(n_seq * t, a.shape[-1])
    tv = min(c, t)

    proj = _norm_matmul(x, p['g_pre_mix'], p['w_in'], tn=1024)
    proj3 = to_seq(proj)

    s5c = min(S5_CHUNK, t)
    o_a, s5_re_n, s5_im_n = _s5_branch(proj3[:, :t, :BRANCH_W], s5_re, s5_im,
                                       p['s5_prompt'] if s5c == S5_CHUNK else p['s5_sample'],
                                       p['s5_d'], p['s5_w_glu'], c=s5c)
    o_b, gdn_s_n, gdn_buf_n = _gdn(proj3, p['gdn_conv_w'], gdn_buf, p['gdn_alog'], p['gdn_dtb'],
                                   p['gdn_norm_g'], gdn_s, c=c, tv=tv)
    prev = _rwkv_cols_to_padded(rwkv_prev)[:, None, :]
    o_c, rwkv_s_n, prev_n = _rwkv(proj3, prev, p['rwkv_mu'], p['rwkv_w0'], p['rwkv_w2'], p['rwkv_a0'],
                                  p['rwkv_a2'], p['rwkv_g2'], p['rwkv_k_k'], p['rwkv_k_a'],
                                  p['rwkv_r_k'], p['rwkv_ln_g'], p['rwkv_ln_b'], rwkv_s, c=c, tv=tv)
    rwkv_prev_n = _rwkv_cols_from_padded(prev_n[:, 0, :])
    o_d, hgrn_s_n = _hgrn(proj3, lb, p['hgrn_norm_g'], hgrn_s, c=c, tv=tv)

    branches = [from_seq(o) for o in (o_a, o_b, o_c, o_d)]
    mix = _mix(x, p['g_pre_mix'], branches, p['w_gate'], p['w_br'])
    x = _matmul_resnorm(mix, p['w_o'], x, p['g_post_mix'])

    if time_major:
        buf = jnp.transpose(ffn_buf, (1, 0, 2)).reshape(1, (FFN_CONV - 1) * n_seq, 2 * D_FF)
        x, nbuf = _ffn(x, p['g_pre_ffn'], p['w_up'], p['ffn_conv_w'], buf, p['w_down'], p['g_post_ffn'],
                       stride=n_seq, rows_per_seq=t * n_seq)
        ffn_buf_n = jnp.transpose(nbuf.reshape(FFN_CONV - 1, n_seq, 2 * D_FF), (1, 0, 2))
    else:
        x, ffn_buf_n = _ffn(x, p['g_pre_ffn'], p['w_up'], p['ffn_conv_w'], ffn_buf, p['w_down'],
                            p['g_post_ffn'], stride=1, rows_per_seq=t)
    return x, (s5_re_n, s5_im_n, gdn_s_n, gdn_buf_n, rwkv_s_n, rwkv_prev_n, hgrn_s_n, ffn_buf_n)


def _run_group(x3, states, layers, lb_all, *, time_major):
    n_seq, t, d = x3.shape
    if time_major:
        x = jnp.transpose(x3, (1, 0, 2)).reshape(t * n_seq, d)
    else:
        x = x3.reshape(n_seq * t, d)
    new = [[] for _ in states]
    for l in range(DEPTH):
        x, st = _layer(x, [s[l] for s in states], layers[l], lb_all[l:l + 1],
                       n_seq=n_seq, t=t, time_major=time_major)
        for lst, s in zip(new, st):
            lst.append(s)
    if time_major:
        y = jnp.transpose(x.reshape(t, n_seq, d), (1, 0, 2))
    else:
        y = x.reshape(n_seq, t, d)
    return y, [jnp.stack(lst) for lst in new]


def kernel(x_prompt, x_sample, state_s5_re, state_s5_im, state_gdn, state_gdn_conv, state_rwkv, state_rwkv_shift, state_hgrn, state_ffn_conv, g_pre_mix, w_in, w_gate, w_br, w_o, g_post_mix, s5_a_re, s5_a_im, s5_log_dt, s5_b_re, s5_b_im, s5_c_re, s5_c_im, s5_d, s5_w_glu, gdn_conv_w, gdn_a_log, gdn_dt_bias, gdn_norm_g, rwkv_mu, rwkv_w0, rwkv_w2, rwkv_a0, rwkv_a2, rwkv_g2, rwkv_k_k, rwkv_k_a, rwkv_r_k, rwkv_ln_g, rwkv_ln_b, hgrn_lb_logits, hgrn_norm_g, g_pre_ffn, w_up, ffn_conv_w, w_down, g_post_ffn):
    w = dict(g_pre_mix=g_pre_mix, w_in=w_in, w_gate=w_gate, w_br=w_br, w_o=w_o, g_post_mix=g_post_mix,
             s5_a_re=s5_a_re, s5_a_im=s5_a_im, s5_log_dt=s5_log_dt, s5_b_re=s5_b_re, s5_b_im=s5_b_im,
             s5_c_re=s5_c_re, s5_c_im=s5_c_im, s5_d=s5_d, s5_w_glu=s5_w_glu, gdn_conv_w=gdn_conv_w,
             gdn_a_log=gdn_a_log, gdn_dt_bias=gdn_dt_bias, gdn_norm_g=gdn_norm_g, rwkv_mu=rwkv_mu,
             rwkv_w0=rwkv_w0, rwkv_w2=rwkv_w2, rwkv_a0=rwkv_a0, rwkv_a2=rwkv_a2, rwkv_g2=rwkv_g2,
             rwkv_k_k=rwkv_k_k, rwkv_k_a=rwkv_k_a, rwkv_r_k=rwkv_r_k, rwkv_ln_g=rwkv_ln_g,
             rwkv_ln_b=rwkv_ln_b, hgrn_norm_g=hgrn_norm_g, g_pre_ffn=g_pre_ffn, w_up=w_up,
             ffn_conv_w=ffn_conv_w, w_down=w_down, g_post_ffn=g_post_ffn)
    sm = jax.nn.softmax(hgrn_lb_logits.astype(F32), axis=0)
    lb_all = jnp.maximum(jnp.cumsum(sm, axis=0) - sm[0], 0.0)
    layers = [_prep_layer(l, w) for l in range(DEPTH)]
    sample_states = (state_s5_re, state_s5_im, state_gdn, state_gdn_conv, state_rwkv,
                     state_rwkv_shift, state_hgrn, state_ffn_conv)
    nb = x_prompt.shape[0]
    prompt_states = tuple(jnp.zeros((DEPTH, nb) + s.shape[2:], F32) for s in sample_states)
    y_prompt, ps = _run_group(x_prompt, prompt_states, layers, lb_all, time_major=False)
    y_sample, ss = _run_group(x_sample, sample_states, layers, lb_all, time_major=True)
    out = [y_prompt, y_sample]
    for a, b in zip(ps, ss):
        out.extend((a, b))
    return tuple(out)
```

```python
import functools
import math

import jax
import jax.numpy as jnp
from jax import lax
from jax.experimental import pallas as pl
from jax.experimental.pallas import tpu as pltpu

F32 = jnp.float32
BF16 = jnp.bfloat16

D_MODEL = 2048
DEPTH = 4
N_BRANCH = 4
BRANCH_W = 512
NORM_EPS = 1e-6

S5_GROUP = 16
S5_GROUPS = 32
S5_STATE = 64

GDN_HEADS = 4
GDN_DH = 128
GDN_CONV = 4

RWKV_DH = 64
RWKV_HEADS = 8
RWKV_W_LORA = 96
RWKV_A_LORA = 96
RWKV_G_LORA = 256
RWKV_LN_EPS = 64e-5
RWKV_COLS = 3 * BRANCH_W + RWKV_W_LORA + RWKV_A_LORA + RWKV_G_LORA

HGRN_HEADS = 4
HGRN_DH = 128
HGRN_SUB = 16
LB_TINY = 1e-30
EXP_CLAMP = 80.0

D_FF = 5632
FFN_CONV = 3

LANES = 128
SUBLANES = 8
VMEM_LIMIT_BYTES = 56 * 1024 * 1024

PROJ_COLS = 7168
COL_S5U, COL_GQ, COL_GK, COL_GV, COL_GZ = 0, 1, 2, 3, 4
COL_RR, COL_RK, COL_RV = 5, 6, 7
COL_HQ, COL_HF, COL_HI, COL_HG = 8, 9, 10, 11
COL_RGL = 6144 // 256
COL_GBA, COL_RWL, COL_RAL = 6400 // 128, 6528 // 128, 6656 // 128
RWKV_PAD_COLS = 2048

ROW_TILE = 512
PROMPT_CHUNK = 64
SEQ_ROWS = SUBLANES
PRE_ROWS = GDN_CONV - 1
SEQ_BLOCK = 8
S5_CHUNK = 16


def _cparams(n_axes):
    return pltpu.CompilerParams(dimension_semantics=("arbitrary",) * n_axes,
                                vmem_limit_bytes=VMEM_LIMIT_BYTES)


def _mm(a, b):
    return jnp.dot(a.astype(BF16), b.astype(BF16), preferred_element_type=F32)


def _mm_nt(a, b):
    return lax.dot_general(a.astype(BF16), b.astype(BF16), (((1,), (1,)), ((), ())),
                           preferred_element_type=F32)


def _mm_tn(a, b):
    return lax.dot_general(a.astype(BF16), b.astype(BF16), (((0,), (0,)), ((), ())),
                           preferred_element_type=F32)


def _split(a):
    hi = a.astype(BF16)
    lo = (a - hi.astype(F32)).astype(BF16)
    return hi, lo


def _mm3(a, b):
    ah, al = _split(a)
    bh, bl = _split(b)
    d = functools.partial(jnp.dot, preferred_element_type=F32)
    return d(ah, bh) + (d(ah, bl) + d(al, bh))


def _rms(x, eps):
    return x * lax.rsqrt(jnp.mean(x * x, axis=-1, keepdims=True) + eps)


def _sigmoid(x):
    return 1.0 / (1.0 + jnp.exp(-x))


def _silu(x):
    return x * _sigmoid(x)


def _softplus(x):
    return jnp.maximum(x, 0.0) + jnp.log1p(jnp.exp(-jnp.abs(x)))


def _gelu_tanh(x):
    return 0.5 * x * (1.0 + jnp.tanh(math.sqrt(2.0 / math.pi) * (x + 0.044715 * (x * x * x))))


def _seg_masks(c, seg, reps=1):
    r = lax.broadcasted_iota(jnp.int32, (c, reps * c), 0)
    s = lax.broadcasted_iota(jnp.int32, (c, reps * c), 1)
    if reps > 1:
        s = s % c
    causal, strict = r >= s, r > s
    if seg < c:
        same = (r // seg) == (s // seg)
        causal, strict = jnp.logical_and(causal, same), jnp.logical_and(strict, same)
    return causal, strict


def _unit_lower_inverses(mats, seg):
    c = mats[0].shape[0]
    eye = (lax.broadcasted_iota(jnp.int32, (c, c), 0)
           == lax.broadcasted_iota(jnp.int32, (c, c), 1)).astype(F32)
    ps = [-a for a in mats]
    ts = [eye + p for p in ps]
    k = 2
    while k < seg:
        ps = [_mm(p, p) for p in ps]
        ts = [t + _mm(t, p) for t, p in zip(ts, ps)]
        k *= 2
    return ts


def _unit_lower_inverse(a, seg):
    return _unit_lower_inverses([a], seg)[0]


def _rows(ref, c):
    return ref[0] if ref.shape[0] == 1 else ref[...].reshape(c, ref.shape[-1])


def _token_rows(c, width):
    return lax.broadcasted_iota(jnp.int32, (c, width), 0) % SEQ_ROWS


def _norm_matmul_kernel(x_ref, g_ref, w_ref, o_ref, h_ref):
    @pl.when(pl.program_id(1) == 0)
    def _():
        h_ref[...] = (_rms(x_ref[...], NORM_EPS) * g_ref[...]).astype(BF16)

    o_ref[...] = jnp.dot(h_ref[...], w_ref[...], preferred_element_type=F32)


def _norm_matmul(x, g, w, *, tn):
    rows, d = x.shape
    n = w.shape[1]
    tm = min(ROW_TILE, rows)
    return pl.pallas_call(
        _norm_matmul_kernel,
        grid=(rows // tm, n // tn),
        in_specs=[pl.BlockSpec((tm, d), lambda i, j: (i, 0)),
                  pl.BlockSpec((1, d), lambda i, j: (0, 0)),
                  pl.BlockSpec((d, tn), lambda i, j: (0, j))],
        out_specs=pl.BlockSpec((tm, tn), lambda i, j: (i, j)),
        out_shape=jax.ShapeDtypeStruct((rows, n), F32),
        scratch_shapes=[pltpu.VMEM((tm, d), BF16)],
        compiler_params=_cparams(2),
        name="norm_matmul",
    )(x, g, w)


def _mix_kernel(x_ref, g_ref, oa_ref, ob_ref, oc_ref, od_ref, wga_ref, wgb_ref, wgc_ref, wgd_ref,
                wbr_ref, o_ref, h_ref):
    @pl.when(pl.program_id(1) == 0)
    def _():
        h_ref[...] = (_rms(x_ref[...], NORM_EPS) * g_ref[...]).astype(BF16)

    h = h_ref[...]
    acc = None
    for n, (br_ref, wg_ref) in enumerate(((oa_ref, wga_ref), (ob_ref, wgb_ref), (oc_ref, wgc_ref),
                                          (od_ref, wgd_ref))):
        gate = _sigmoid(jnp.dot(h, wg_ref[...], preferred_element_type=F32))
        term = gate * jnp.dot(br_ref[...], wbr_ref[n], preferred_element_type=F32)
        acc = term if acc is None else acc + term
    o_ref[...] = acc.astype(BF16)


def _mix(x, g, branches, wg, wbr, *, tn=256):
    rows, d = x.shape
    tm = min(ROW_TILE, rows)
    bw = branches[0].shape[1]
    br_spec = pl.BlockSpec((tm, bw), lambda i, j: (i, 0))
    nt = d // tn
    wg_specs = [pl.BlockSpec((d, tn), functools.partial(lambda i, j, n: (0, n * nt + j), n=n))
                for n in range(N_BRANCH)]
    return pl.pallas_call(
        _mix_kernel,
        grid=(rows // tm, nt),
        in_specs=[pl.BlockSpec((tm, d), lambda i, j: (i, 0)),
                  pl.BlockSpec((1, d), lambda i, j: (0, 0)),
                  br_spec, br_spec, br_spec, br_spec, *wg_specs,
                  pl.BlockSpec((N_BRANCH, bw, tn), lambda i, j: (0, 0, j))],
        out_specs=pl.BlockSpec((tm, tn), lambda i, j: (i, j)),
        out_shape=jax.ShapeDtypeStruct((rows, d), BF16),
        scratch_shapes=[pltpu.VMEM((tm, d), BF16)],
        compiler_params=_cparams(2),
        name="branch_mix",
    )(x, g, *branches, wg, wg, wg, wg, wbr)


def _matmul_resnorm_kernel(a_ref, w_ref, x_ref, g_ref, o_ref, acc_ref):
    k = pl.program_id(1)

    @pl.when(k == 0)
    def _():
        acc_ref[...] = jnp.zeros_like(acc_ref)

    acc_ref[...] += jnp.dot(a_ref[...], w_ref[...], preferred_element_type=F32)

    @pl.when(k == pl.num_programs(1) - 1)
    def _():
        o_ref[...] = x_ref[...] + _rms(acc_ref[...], NORM_EPS) * g_ref[...]


def _matmul_resnorm(a, w, x, g, *, tk=512):
    rows, kdim = a.shape
    d = w.shape[1]
    tm = min(ROW_TILE, rows)
    return pl.pallas_call(
        _matmul_resnorm_kernel,
        grid=(rows // tm, kdim // tk),
        in_specs=[pl.BlockSpec((tm, tk), lambda i, k: (i, k)),
                  pl.BlockSpec((tk, d), lambda i, k: (k, 0)),
                  pl.BlockSpec((tm, d), lambda i, k: (i, 0)),
                  pl.BlockSpec((1, d), lambda i, k: (0, 0))],
        out_specs=pl.BlockSpec((tm, d), lambda i, k: (i, 0)),
        out_shape=jax.ShapeDtypeStruct((rows, d), F32),
        scratch_shapes=[pltpu.VMEM((tm, d), F32)],
        compiler_params=_cparams(2),
        name="matmul_resnorm",
    )(a, w, x, g)


def _ffn_kernel(x_ref, gpre_ref, wa_ref, wb_ref, cwa_ref, cwb_ref, bufa_ref, bufb_ref, wd_ref,
                gpost_ref, o_ref, nbufa_ref, nbufb_ref, h_ref, acc_ref, sa_ref, sb_ref, carry_ref,
                *, tm, stride, tiles_per_seq):
    i = pl.program_id(0)
    f = pl.program_id(1)
    keep = (FFN_CONV - 1) * stride
    pad = max(SUBLANES, keep)

    @pl.when(f == 0)
    def _():
        h_ref[...] = (_rms(x_ref[...], NORM_EPS) * gpre_ref[...]).astype(BF16)
        acc_ref[...] = jnp.zeros_like(acc_ref)

    h = h_ref[...]
    halves = []
    for idx, (w_ref, cw_ref, buf_ref, nbuf_ref, s_ref) in enumerate(
            ((wa_ref, cwa_ref, bufa_ref, nbufa_ref, sa_ref),
             (wb_ref, cwb_ref, bufb_ref, nbufb_ref, sb_ref))):
        if tiles_per_seq == 1:
            s_ref[pad - keep:pad, :] = buf_ref[0]
        else:
            first = (i % tiles_per_seq) == 0

            @pl.when(first)
            def _():
                s_ref[pad - keep:pad, :] = buf_ref[0]

            @pl.when(jnp.logical_not(first))
            def _():
                s_ref[pad - keep:pad, :] = carry_ref[f, idx]

        s_ref[pad:pad + tm, :] = jnp.dot(h, w_ref[...], preferred_element_type=F32)
        cw = cw_ref[...]
        y = None
        for j in range(FFN_CONV):
            off = pad - (FFN_CONV - 1 - j) * stride
            term = cw[j:j + 1, :] * s_ref[off:off + tm, :]
            y = term if y is None else y + term
        last = s_ref[pad + tm - keep:pad + tm, :]
        nbuf_ref[0] = last
        if tiles_per_seq > 1:
            carry_ref[f, idx] = last
        halves.append(y)

    act = (_gelu_tanh(halves[0]) * halves[1]).astype(BF16)
    acc_ref[...] += jnp.dot(act, wd_ref[...], preferred_element_type=F32)

    @pl.when(f == pl.num_programs(1) - 1)
    def _():
        o_ref[...] = x_ref[...] + _rms(acc_ref[...], NORM_EPS) * gpost_ref[...]


def _ffn(x, gpre, w_up, conv_w, buf, w_down, gpost, *, stride, rows_per_seq, tf=512):
    rows, d = x.shape
    tm = min(ROW_TILE, rows_per_seq)
    tiles_per_seq = rows_per_seq // tm
    nf = D_FF // tf
    keep = (FFN_CONV - 1) * stride
    pad = max(SUBLANES, keep)
    carry_rows = keep if tiles_per_seq > 1 else SUBLANES
    kern = functools.partial(_ffn_kernel, tm=tm, stride=stride, tiles_per_seq=tiles_per_seq)
    buf_a = pl.BlockSpec((1, keep, tf), lambda i, f: (i // tiles_per_seq, 0, f))
    buf_b = pl.BlockSpec((1, keep, tf), lambda i, f: (i // tiles_per_seq, 0, nf + f))
    y, nbuf_a, nbuf_b = pl.pallas_call(
        kern,
        grid=(rows // tm, nf),
        in_specs=[pl.BlockSpec((tm, d), lambda i, f: (i, 0)),
                  pl.BlockSpec((1, d), lambda i, f: (0, 0)),
                  pl.BlockSpec((d, tf), lambda i, f: (0, f)),
                  pl.BlockSpec((d, tf), lambda i, f: (0, nf + f)),
                  pl.BlockSpec((FFN_CONV, tf), lambda i, f: (0, f)),
                  pl.BlockSpec((FFN_CONV, tf), lambda i, f: (0, nf + f)),
                  buf_a, buf_b,
                  pl.BlockSpec((tf, d), lambda i, f: (f, 0)),
                  pl.BlockSpec((1, d), lambda i, f: (0, 0))],
        out_specs=[pl.BlockSpec((tm, d), lambda i, f: (i, 0)),
                   pl.BlockSpec((1, keep, tf), lambda i, f: (i, 0, f)),
                   pl.BlockSpec((1, keep, tf), lambda i, f: (i, 0, f))],
        out_shape=[jax.ShapeDtypeStruct((rows, d), F32),
                   jax.ShapeDtypeStruct((rows // tm, keep, D_FF), F32),
                   jax.ShapeDtypeStruct((rows // tm, keep, D_FF), F32)],
        scratch_shapes=[pltpu.VMEM((tm, d), BF16),
                        pltpu.VMEM((tm, d), F32),
                        pltpu.VMEM((pad + tm, tf), F32),
                        pltpu.VMEM((pad + tm, tf), F32),
                        pltpu.VMEM((nf, 2, carry_rows, tf), F32)],
        compiler_params=_cparams(2),
        name="conv_ffn",
    )(x, gpre, w_up, w_up, conv_w, conv_w, buf, buf, w_down, gpost)
    nbuf = jnp.concatenate([nbuf_a, nbuf_b], axis=-1)
    return y, nbuf[tiles_per_seq - 1::tiles_per_seq]


def _s5_kernel(u_ref, h0_ref, bst_ref, kst_ref, cst_ref, apow_ref, y_ref, hfin_ref, e_ref,
               *, n_seq, n_chunks):
    rows = n_seq * n_chunks
    u = u_ref[0]
    h0 = h0_ref[0]
    half = S5_STATE

    def cmul(k, x):
        return x * apow_ref[0, 2 * k:2 * k + 1, :] + pltpu.roll(x, half, 1) * apow_ref[0, 2 * k + 1:2 * k + 2, :]

    e = _mm3(u, bst_ref[0]) + cmul(0, h0)
    if n_chunks > 1:
        j = lax.broadcasted_iota(jnp.int32, (rows, 2 * half), 0) % n_chunks
        k, sh = 0, 1
        while sh < n_chunks:
            e = e + jnp.where(j >= sh, cmul(k, pltpu.roll(e, sh, 0)), 0.0)
            k, sh = k + 1, sh * 2
        e_ref[...] = e
        hfin_ref[0] = e_ref[pl.ds(n_chunks - 1, n_seq, stride=n_chunks), :]
        h_start = jnp.where(j >= 1, pltpu.roll(e, 1, 0), 0.0) + h0
    else:
        hfin_ref[0] = e
        h_start = h0
    y_ref[0] = _mm3(u, kst_ref[0]) + _mm3(h_start, cst_ref[0])


def _s5_scan(u_g, h0_g, bst, kst, cst, apow, *, n_seq, n_chunks):
    groups, rows, cw = u_g.shape
    kern = functools.partial(_s5_kernel, n_seq=n_seq, n_chunks=n_chunks)
    spec3 = lambda a: pl.BlockSpec((1,) + a.shape[1:], lambda g: (g, 0, 0))
    return pl.pallas_call(
        kern,
        grid=(groups,),
        in_specs=[spec3(u_g), spec3(h0_g), spec3(bst), spec3(kst), spec3(cst), spec3(apow)],
        out_specs=[pl.BlockSpec((1, rows, cw), lambda g: (g, 0, 0)),
                   pl.BlockSpec((1, n_seq, 2 * S5_STATE), lambda g: (g, 0, 0))],
        out_shape=[jax.ShapeDtypeStruct((groups, rows, cw), F32),
                   jax.ShapeDtypeStruct((groups, n_seq, 2 * S5_STATE), F32)],
        scratch_shapes=[pltpu.VMEM((rows, 2 * S5_STATE), F32)],
        compiler_params=_cparams(1),
        name="s5_scan",
    )(u_g, h0_g, bst, kst, cst, apow)


def _s5_glu_kernel(y_ref, u_ref, d_ref, w_ref, o_ref):
    g = _gelu_tanh(y_ref[...] + d_ref[...] * u_ref[...])
    o_ref[...] = (g * _sigmoid(_mm(g, w_ref[...]))).astype(BF16)


def _s5_glu(y, u, d, w_glu):
    rows, bw = y.shape
    tm = min(ROW_TILE, rows)
    row_spec = pl.BlockSpec((tm, bw), lambda i: (i, 0))
    return pl.pallas_call(
        _s5_glu_kernel,
        grid=(rows // tm,),
        in_specs=[row_spec, row_spec,
                  pl.BlockSpec((1, bw), lambda i: (0, 0)),
                  pl.BlockSpec((bw, bw), lambda i: (0, 0))],
        out_specs=row_spec,
        out_shape=jax.ShapeDtypeStruct((rows, bw), BF16),
        compiler_params=_cparams(1),
        name="s5_glu",
    )(y, u, d, w_glu)


def _s5_weights(a_re, a_im, log_dt, b_re, b_im, c_re, c_im, c):
    lam = lax.complex(a_re.astype(F32), a_im.astype(F32))
    ldt = lam * jnp.exp(log_dt.astype(F32))[:, None]
    a_bar = jnp.exp(ldt)
    b_bar = ((a_bar - 1.0) / lam)[..., None] * lax.complex(b_re.astype(F32), b_im.astype(F32))
    cm = lax.complex(c_re.astype(F32), c_im.astype(F32))
    tau = jnp.arange(c + 1, dtype=F32)
    apw = jnp.exp(ldt[None] * tau[:, None, None])
    bst = apw[:c][::-1][:, :, :, None] * b_bar[None]
    bst = jnp.transpose(bst, (1, 0, 3, 2)).reshape(S5_GROUPS, c * S5_GROUP, S5_STATE)
    bst = jnp.concatenate([jnp.real(bst), jnp.imag(bst)], axis=-1)
    kt = jnp.real(jnp.einsum('ghp,tgp,gpk->tgkh', cm, apw[:c], b_bar))
    s_idx = jnp.arange(c)[:, None]
    t_idx = jnp.arange(c)[None, :]
    kst = jnp.where((t_idx >= s_idx)[:, :, None, None, None],
                    kt[jnp.clip(t_idx - s_idx, 0, c - 1)], 0.0)
    kst = jnp.transpose(kst, (2, 0, 3, 1, 4)).reshape(S5_GROUPS, c * S5_GROUP, c * S5_GROUP)
    ca = cm[None] * apw[1:c + 1][:, :, None, :]
    ca = jnp.transpose(ca, (1, 3, 0, 2)).reshape(S5_GROUPS, S5_STATE, c * S5_GROUP)
    cst = jnp.concatenate([jnp.real(ca), -jnp.imag(ca)], axis=1)
    rows = []
    k = 0
    while True:
        p = jnp.exp(ldt * float(c * 2 ** k))
        rows.append(jnp.concatenate([jnp.real(p), jnp.real(p)], axis=-1))
        rows.append(jnp.concatenate([-jnp.imag(p), jnp.imag(p)], axis=-1))
        k += 1
        if c * 2 ** k > 4096:
            break
    apow = jnp.stack(rows, axis=1)
    return bst, kst, cst, apow


def _s5_branch(u3, h0_re, h0_im, sw, d_skip, w_glu, *, c):
    n_seq, t, _ = u3.shape
    n_chunks = t // c
    bst, kst, cst, apow = sw
    u_g = u3.reshape(n_seq, n_chunks, c, S5_GROUPS, S5_GROUP)
    u_g = jnp.transpose(u_g, (3, 0, 1, 2, 4)).reshape(S5_GROUPS, n_seq * n_chunks, c * S5_GROUP)
    h0 = jnp.concatenate([h0_re, h0_im], axis=-1)
    h0 = jnp.transpose(h0, (1, 0, 2))
    h0_g = jnp.zeros((S5_GROUPS, n_seq, n_chunks, 2 * S5_STATE), F32).at[:, :, 0].set(h0)
    h0_g = h0_g.reshape(S5_GROUPS, n_seq * n_chunks, 2 * S5_STATE)
    y_g, hfin = _s5_scan(u_g, h0_g, bst, kst, cst, apow, n_seq=n_seq, n_chunks=n_chunks)
    y = y_g.reshape(S5_GROUPS, n_seq, n_chunks, c, S5_GROUP)
    y = jnp.transpose(y, (1, 2, 3, 0, 4)).reshape(n_seq * t, BRANCH_W)
    o = _s5_glu(y, u3.reshape(n_seq * t, BRANCH_W), d_skip, w_glu)
    hfin = jnp.transpose(hfin, (1, 0, 2))
    return o.reshape(n_seq, t, BRANCH_W), hfin[..., :S5_STATE], hfin[..., S5_STATE:]


def _mixer_grid(n_seq, t, stacked):
    if stacked:
        assert t == SEQ_ROWS and n_seq % SEQ_BLOCK == 0
        c = SEQ_BLOCK * SEQ_ROWS
        shape = lambda w: (SEQ_BLOCK, SEQ_ROWS, w)
        imap = lambda blk: (lambda b: (b, 0, blk))
        smap = lambda nd: (lambda b: (b,) + (0,) * (nd - 1))
        return (n_seq // SEQ_BLOCK,), c, SEQ_ROWS, shape, imap, smap, SEQ_BLOCK
    assert t % PROMPT_CHUNK == 0
    c = PROMPT_CHUNK
    shape = lambda w: (1, c, w)
    imap = lambda blk: (lambda b, i: (b, i, blk))
    smap = lambda nd: (lambda b, i: (b,) + (0,) * (nd - 1))
    return (n_seq, t // c), c, c, shape, imap, smap, 1


def _layer_state_spec(s_all, layer, nblk, n_grid):
    tail = (0,) * (s_all.ndim - 2)
    imap = (lambda b: (layer, b) + tail) if n_grid == 1 else (lambda b, i: (layer, b) + tail)
    return pl.BlockSpec((None, nblk) + s_all.shape[2:], imap)


def _const_spec(a, n_grid):
    zeros = (0,) * a.ndim
    return pl.BlockSpec(a.shape, (lambda b: zeros) if n_grid == 1 else (lambda b, i: zeros))


def _gdn_kernel(pq_ref, pk_ref, pv_ref, pz_ref, pba_ref, cw_ref, cbuf_ref, alog_ref, dtb_ref,
                ng_ref, s0_ref, o_ref, sout_ref, cout_ref, *scratch, c, seg, tv):
    stacked = seg < c
    keep = GDN_CONV - 1
    x = jnp.concatenate([_rows(pq_ref, c), _rows(pk_ref, c), _rows(pv_ref, c)], axis=1)
    if stacked:
        (conv_ref,) = scratch
        t_in = _token_rows(c, 3 * BRANCH_W)
        x = jnp.where(t_in < keep, _rows(cbuf_ref, c), x)
        conv_ref[0:SUBLANES, :] = jnp.zeros((SUBLANES, 3 * BRANCH_W), F32)
        cout_ref[...] = x.reshape(cout_ref.shape)
    else:
        s_ref, conv_ref = scratch

        @pl.when(pl.program_id(1) == 0)
        def _():
            s_ref[...] = s0_ref[0]
            conv_ref[SUBLANES - keep:SUBLANES, :] = cbuf_ref[0]

    conv_ref[SUBLANES:SUBLANES + c, :] = x
    cw = cw_ref[...]
    y = None
    for j in range(GDN_CONV):
        off = SUBLANES - keep + j
        term = cw[j:j + 1, :] * conv_ref[off:off + c, :]
        y = term if y is None else y + term
    if not stacked:
        last = conv_ref[SUBLANES + c - keep:SUBLANES + c, :]
        conv_ref[SUBLANES - keep:SUBLANES, :] = last
        cout_ref[0] = last
    act = _silu(y)

    ba = _rows(pba_ref, c)
    beta_all = _sigmoid(ba)
    g_all = -jnp.exp(alog_ref[...]) * _softplus(ba + dtb_ref[...])
    if stacked:
        t_in = _token_rows(c, LANES)
        valid = jnp.logical_and(t_in >= PRE_ROWS, t_in < PRE_ROWS + tv)
        beta_all = jnp.where(valid, beta_all, 0.0)
        g_all = jnp.where(valid, g_all, 0.0)
    causal, strict = _seg_masks(c, seg)
    d_all = _mm3(causal.astype(F32), g_all)
    dt_all = d_all.T
    ng = ng_ref[...]
    z = _rows(pz_ref, c)
    n_seg = c // seg

    heads = range(GDN_HEADS)
    sls = [slice(h * GDN_DH, (h + 1) * GDN_DH) for h in heads]
    segs = [slice(j * seg, (j + 1) * seg) for j in range(n_seg)]
    cat0 = lambda xs: xs[0] if len(xs) == 1 else jnp.concatenate(xs, axis=0)
    state = lambda j, h: s0_ref[j, h] if stacked else s_ref[h]
    q = [act[:, sl] for sl in sls]
    k = [act[:, BRANCH_W + h * GDN_DH:BRANCH_W + (h + 1) * GDN_DH] for h in heads]
    v = [act[:, 2 * BRANCH_W + h * GDN_DH:2 * BRANCH_W + (h + 1) * GDN_DH] for h in heads]
    q = [x * lax.rsqrt(jnp.sum(x * x, axis=-1, keepdims=True) + 1e-6) * (GDN_DH ** -0.5) for x in q]
    k = [x * lax.rsqrt(jnp.sum(x * x, axis=-1, keepdims=True) + 1e-6) for x in k]
    beta = [beta_all[:, h:h + 1] for h in heads]
    d = [d_all[:, GDN_HEADS + h:GDN_HEADS + h + 1] for h in heads]
    d_row = [dt_all[GDN_HEADS + h:GDN_HEADS + h + 1, :] for h in heads]
    decay = [jnp.where(causal, jnp.exp(jnp.where(causal, d[h] - d_row[h], 0.0)), 0.0) for h in heads]
    kb = [k[h] * beta[h] for h in heads]
    prod = [_mm_nt(jnp.concatenate([kb[h], q[h]], axis=0), k[h]) for h in heads]
    m = [jnp.where(strict, prod[h][:c] * decay[h], 0.0) for h in heads]
    attn = [prod[h][c:] * decay[h] for h in heads]
    t_inv = _unit_lower_inverses(m, seg)
    ed = [jnp.exp(x) for x in d]
    sol = [_mm(t_inv[h], jnp.concatenate([v[h] * beta[h], kb[h] * ed[h]], axis=1)) for h in heads]
    qe = [q[h] * ed[h] for h in heads]
    both = [[_mm(jnp.concatenate([sol[h][rows, GDN_DH:], qe[h][rows]], axis=0), state(j, h))
             for j, rows in enumerate(segs)] for h in heads]
    u = [cat0([sol[h][rows, :GDN_DH] - both[h][j][:seg] for j, rows in enumerate(segs)]) for h in heads]
    qs = [cat0([b[seg:] for b in both[h]]) for h in heads]
    o = [qs[h] + _mm(attn[h], u[h]) for h in heads]
    for h in heads:
        for j, rows in enumerate(segs):
            dl = d[h][(j + 1) * seg - 1:(j + 1) * seg, :]
            s_new = state(j, h) * jnp.exp(dl) + _mm_tn(k[h][rows] * jnp.exp(dl - d[h][rows]), u[h][rows])
            if stacked:
                sout_ref[j, h] = s_new
            else:
                s_ref[h] = s_new
    outs = [_rms(o[h], NORM_EPS) * ng * _silu(z[:, sls[h]]) for h in heads]
    o_ref[...] = jnp.concatenate(outs, axis=1).reshape(o_ref.shape).astype(BF16)

    if not stacked:
        @pl.when(pl.program_id(1) == pl.num_programs(1) - 1)
        def _():
            sout_ref[0] = s_ref[...]


def _gdn(proj3, conv_w, cbuf, alog, dtb, ng, s_all, layer, *, stacked, tv):
    n_seq, t, _ = proj3.shape
    grid, c, seg, shape, imap, smap, nblk = _mixer_grid(n_seq, t, stacked)
    ng_ = len(grid)
    kern = functools.partial(_gdn_kernel, c=c, seg=seg, tv=tv)
    col = lambda blk, w: pl.BlockSpec(shape(w), imap(blk))
    state_spec = pl.BlockSpec((nblk, GDN_HEADS, GDN_DH, GDN_DH), smap(4))
    cbuf_spec = pl.BlockSpec((nblk,) + cbuf.shape[1:], smap(3))
    scratch = [pltpu.VMEM((SUBLANES + c, 3 * BRANCH_W), F32)]
    if not stacked:
        scratch = [pltpu.VMEM((GDN_HEADS, GDN_DH, GDN_DH), F32)] + scratch
    return pl.pallas_call(
        kern,
        grid=grid,
        in_specs=[col(COL_GQ, BRANCH_W), col(COL_GK, BRANCH_W), col(COL_GV, BRANCH_W),
                  col(COL_GZ, BRANCH_W), col(COL_GBA, LANES),
                  _const_spec(conv_w, ng_), cbuf_spec, _const_spec(alog, ng_), _const_spec(dtb, ng_),
                  _const_spec(ng, ng_), _layer_state_spec(s_all, layer, nblk, ng_)],
        out_specs=[col(0, BRANCH_W), state_spec, cbuf_spec],
        out_shape=[jax.ShapeDtypeStruct((n_seq, t, BRANCH_W), BF16),
                   jax.ShapeDtypeStruct(s_all.shape[1:], F32),
                   jax.ShapeDtypeStruct(cbuf.shape, F32)],
        scratch_shapes=scratch,
        compiler_params=_cparams(ng_),
        name="gdn",
    )(proj3, proj3, proj3, proj3, proj3, conv_w, cbuf, alog, dtb, ng, s_all)


def _rwkv_kernel(pr_ref, pk_ref, pv_ref, pwl_ref, pal_ref, pgl_ref, prev_ref, mu_ref, w0_ref, w2_ref,
                 a0_ref, a2_ref, g2_ref, kk_ref, ka_ref, rk_ref, lng_ref, lnb_ref, s0_ref,
                 o_ref, sout_ref, pout_ref, *scratch, c, seg, tv):
    stacked = seg < c
    p = jnp.concatenate([_rows(pr_ref, c), _rows(pk_ref, c), _rows(pv_ref, c), _rows(pwl_ref, c),
                         _rows(pal_ref, c), _rows(pgl_ref, c)], axis=1)
    if stacked:
        (p_ref,) = scratch
        t_in = _token_rows(c, RWKV_PAD_COLS)
        p = jnp.where(t_in == PRE_ROWS - 1, _rows(prev_ref, c), p)
        p_ref[0:SUBLANES, :] = jnp.zeros((SUBLANES, RWKV_PAD_COLS), F32)
        pout_ref[...] = p.reshape(pout_ref.shape)
    else:
        s_ref, p_ref = scratch

        @pl.when(pl.program_id(1) == 0)
        def _():
            s_ref[...] = s0_ref[0]
            p_ref[SUBLANES - 1:SUBLANES, :] = prev_ref[0]

    p_ref[SUBLANES:SUBLANES + c, :] = p
    prev = p_ref[SUBLANES - 1:SUBLANES - 1 + c, :]
    if not stacked:
        last = p_ref[SUBLANES + c - 1:SUBLANES + c, :]
        p_ref[SUBLANES - 1:SUBLANES, :] = last
        pout_ref[0] = last
    pm = p + (prev - p) * mu_ref[...]
    r = pm[:, :BRANCH_W]
    k = pm[:, BRANCH_W:2 * BRANCH_W]
    v = pm[:, 2 * BRANCH_W:3 * BRANCH_W]
    wl = pm[:, 3 * BRANCH_W:3 * BRANCH_W + LANES]
    al = pm[:, 3 * BRANCH_W + LANES:3 * BRANCH_W + 2 * LANES]
    gl = pm[:, 3 * BRANCH_W + 2 * LANES:]
    w_log = -_softplus(-(w0_ref[...] + _mm(jnp.tanh(wl), w2_ref[...]))) - 0.5
    lw = -jnp.exp(w_log)
    a = _sigmoid(a0_ref[...] + _mm(al, a2_ref[...]))
    gate = _mm(_sigmoid(gl), g2_ref[...])
    kk_un = k * kk_ref[...]
    k2 = k * (1.0 + (a - 1.0) * ka_ref[...])
    if stacked:
        t_in = _token_rows(c, BRANCH_W)
        valid = jnp.logical_and(t_in >= PRE_ROWS, t_in < PRE_ROWS + tv)
        lw = jnp.where(valid, lw, 0.0)
        kk_un = jnp.where(valid, kk_un, 0.0)
        k2 = jnp.where(valid, k2, 0.0)
    causal, _ = _seg_masks(c, seg)
    causal2, strict2 = _seg_masks(c, seg, reps=2)
    g_cum = _mm3(causal.astype(F32), lw)
    e_pos = jnp.exp(g_cum)
    e_neg = jnp.exp(-g_cum)
    e_prev = jnp.exp(g_cum - lw)
    rk_w = rk_ref[...]
    lng = lng_ref[...]
    lnb = lnb_ref[...]
    n_seg = c // seg

    heads = range(RWKV_HEADS)
    sls = [slice(h * RWKV_DH, (h + 1) * RWKV_DH) for h in heads]
    segs = [slice(j * seg, (j + 1) * seg) for j in range(n_seg)]
    cat0 = lambda xs: xs[0] if len(xs) == 1 else jnp.concatenate(xs, axis=0)
    state = lambda j, h: s0_ref[j, h] if stacked else s_ref[h]
    kkh = [kk_un[:, sl] for sl in sls]
    kkh = [x * lax.rsqrt(jnp.sum(x * x, axis=-1, keepdims=True) + 1e-6) for x in kkh]
    alpha_hat = [kkh[h] * a[:, sls[h]] * e_neg[:, sls[h]] for h in heads]
    k_hat = [k2[:, sl] * e_neg[:, sl] for sl in sls]
    kap = [kkh[h] * e_prev[:, sls[h]] for h in heads]
    r_t = [r[:, sl] * e_pos[:, sl] for sl in sls]
    vh = [v[:, sl] for sl in sls]
    rhs = [jnp.concatenate([alpha_hat[h], k_hat[h]], axis=0) for h in heads]
    ab = [jnp.where(strict2, _mm_nt(kap[h], rhs[h]), 0.0) for h in heads]
    rr = [jnp.where(causal2, _mm_nt(r_t[h], rhs[h]), 0.0) for h in heads]
    both = [[_mm_nt(jnp.concatenate([kap[h][rows], r_t[h][rows]], axis=0), state(j, h))
             for j, rows in enumerate(segs)] for h in heads]
    bv = [_mm(jnp.concatenate([ab[h][:, c:], rr[h][:, c:]], axis=0), vh[h]) for h in heads]
    t_inv = _unit_lower_inverses([ab[h][:, :c] for h in heads], seg)
    ks = [cat0([b[:seg] for b in both[h]]) for h in heads]
    rs = [cat0([b[seg:] for b in both[h]]) for h in heads]
    u = [_mm(t_inv[h], ks[h] + bv[h][:c]) for h in heads]
    y = [rs[h] + bv[h][c:] - _mm(rr[h][:, :c], u[h]) for h in heads]
    for h in heads:
        for j, rows in enumerate(segs):
            el = e_pos[(j + 1) * seg - 1:(j + 1) * seg, sls[h]]
            s_new = state(j, h) * el + _mm_tn(
                jnp.concatenate([vh[h][rows], -u[h][rows]], axis=0),
                jnp.concatenate([k_hat[h][rows] * el, alpha_hat[h][rows] * el], axis=0))
            if stacked:
                sout_ref[j, h] = s_new
            else:
                s_ref[h] = s_new
    outs = []
    for h in heads:
        sl = sls[h]
        mu = jnp.mean(y[h], axis=-1, keepdims=True)
        yc = y[h] - mu
        var = jnp.mean(yc * yc, axis=-1, keepdims=True)
        yn = yc * lax.rsqrt(var + RWKV_LN_EPS) * lng[:, sl] + lnb[:, sl]
        bonus = jnp.sum(r[:, sl] * k2[:, sl] * rk_w[:, sl], axis=-1, keepdims=True) * vh[h]
        outs.append(yn + bonus)
    o_ref[...] = (jnp.concatenate(outs, axis=1) * gate).reshape(o_ref.shape).astype(BF16)

    if not stacked:
        @pl.when(pl.program_id(1) == pl.num_programs(1) - 1)
        def _():
            sout_ref[0] = s_ref[...]


def _rwkv(proj3, prev, mu, w0, w2, a0, a2, g2, k_k, k_a, r_k, ln_g, ln_b, s_all, layer, *, stacked, tv):
    n_seq, t, _ = proj3.shape
    grid, c, seg, shape, imap, smap, nblk = _mixer_grid(n_seq, t, stacked)
    ng_ = len(grid)
    kern = functools.partial(_rwkv_kernel, c=c, seg=seg, tv=tv)
    col = lambda blk, w: pl.BlockSpec(shape(w), imap(blk))
    cs = lambda a: _const_spec(a, ng_)
    state_spec = pl.BlockSpec((nblk, RWKV_HEADS, RWKV_DH, RWKV_DH), smap(4))
    prev_spec = pl.BlockSpec((nblk,) + prev.shape[1:], smap(3))
    scratch = [pltpu.VMEM((SUBLANES + c, RWKV_PAD_COLS), F32)]
    if not stacked:
        scratch = [pltpu.VMEM((RWKV_HEADS, RWKV_DH, RWKV_DH), F32)] + scratch
    return pl.pallas_call(
        kern,
        grid=grid,
        in_specs=[col(COL_RR, BRANCH_W), col(COL_RK, BRANCH_W), col(COL_RV, BRANCH_W),
                  col(COL_RWL, LANES), col(COL_RAL, LANES), col(COL_RGL, 2 * LANES),
                  prev_spec, cs(mu), cs(w0), cs(w2), cs(a0), cs(a2), cs(g2),
                  cs(k_k), cs(k_a), cs(r_k), cs(ln_g), cs(ln_b),
                  _layer_state_spec(s_all, layer, nblk, ng_)],
        out_specs=[col(0, BRANCH_W), state_spec, prev_spec],
        out_shape=[jax.ShapeDtypeStruct((n_seq, t, BRANCH_W), BF16),
                   jax.ShapeDtypeStruct(s_all.shape[1:], F32),
                   jax.ShapeDtypeStruct(prev.shape, F32)],
        scratch_shapes=scratch,
        compiler_params=_cparams(ng_),
        name="rwkv7",
    )(proj3, proj3, proj3, proj3, proj3, proj3, prev, mu, w0, w2, a0, a2, g2, k_k, k_a, r_k,
      ln_g, ln_b, s_all)


def _hgrn_kernel(pq_ref, pf_ref, pi_ref, pg_ref, lb_ref, ng_ref, s0_ref, o_ref, sout_ref, *scratch,
                 c, seg, tv):
    stacked = seg < c
    if not stacked:
        (s_ref,) = scratch

        @pl.when(pl.program_id(1) == 0)
        def _():
            s_ref[...] = s0_ref[0]

    lb = lb_ref[...]
    hf = _rows(pf_ref, c)
    log_sig = jnp.minimum(hf, 0.0) - jnp.log1p(jnp.exp(-jnp.abs(hf)))
    x1 = jnp.log(jnp.maximum(lb, LB_TINY))
    x2 = jnp.log1p(-lb) + log_sig
    logf = jnp.maximum(x1, x2) + jnp.log1p(jnp.exp(-jnp.abs(x1 - x2)))
    kf = (1.0 - lb) * _sigmoid(-hf)
    q = _silu(_rows(pq_ref, c))
    v = _rows(pi_ref, c)
    if stacked:
        t_in = _token_rows(c, BRANCH_W)
        valid = jnp.logical_and(t_in >= PRE_ROWS, t_in < PRE_ROWS + tv)
        logf = jnp.where(valid, logf, 0.0)
        kf = jnp.where(valid, kf, 0.0)
    sub = seg if stacked else HGRN_SUB
    n_sub = c // sub
    blk_causal, _ = _seg_masks(c, sub)
    b_all = _mm3(blk_causal.astype(F32), logf)
    causal, _ = _seg_masks(sub, sub)
    mid = sub // 2 - 1
    b_last_rows = jnp.concatenate([b_all[(j + 1) * sub - 1:(j + 1) * sub, :] for j in range(n_sub)]
                                  + [jnp.zeros((SUBLANES - n_sub, BRANCH_W), F32)] * (n_sub < SUBLANES),
                                  axis=0)
    decay_cols = jnp.exp(b_last_rows).T
    ng = ng_ref[...]
    gate = _silu(_rows(pg_ref, c))

    heads = range(HGRN_HEADS)
    sls = [slice(h * HGRN_DH, (h + 1) * HGRN_DH) for h in heads]
    pairs = [(j, h) for j in range(n_sub) for h in heads]
    blk = lambda x, j, h: x[j * sub:(j + 1) * sub, sls[h]]
    e_mid, e_last = {}, {}
    for j, h in pairs:
        b = blk(b_all, j, h)
        e_mid[j, h] = b - b[mid:mid + 1, :]
        e_last[j, h] = b[sub - 1:sub, :] - b
    attn = {jh: jnp.where(causal, _mm_nt(blk(q, *jh) * jnp.exp(jnp.minimum(e_mid[jh], EXP_CLAMP)),
                                         blk(kf, *jh) * jnp.exp(jnp.minimum(-e_mid[jh], EXP_CLAMP))), 0.0)
            for jh in pairs}
    intra = {jh: _mm(attn[jh], blk(v, *jh)) for jh in pairs}
    q_dec = {jh: blk(q, *jh) * jnp.exp(blk(b_all, *jh)) for jh in pairs}
    kv = {jh: _mm_tn(blk(kf, *jh) * jnp.exp(e_last[jh]), blk(v, *jh)) for jh in pairs}
    outs = [[] for _ in heads]
    for j in range(n_sub):
        s = [s0_ref[j, h] if stacked else s_ref[h] for h in heads]
        o = [_mm(q_dec[j, h], s[h]) + intra[j, h] for h in heads]
        for h in heads:
            s_new = s[h] * decay_cols[sls[h], j:j + 1] + kv[j, h]
            if stacked:
                sout_ref[j, h] = s_new
            else:
                s_ref[h] = s_new
            outs[h].append(o[h])
    cols = []
    for h in range(HGRN_HEADS):
        sl = slice(h * HGRN_DH, (h + 1) * HGRN_DH)
        o = outs[h][0] if n_sub == 1 else jnp.concatenate(outs[h], axis=0)
        cols.append(_rms(o, NORM_EPS) * ng * gate[:, sl])
    o_ref[...] = jnp.concatenate(cols, axis=1).reshape(o_ref.shape).astype(BF16)

    if not stacked:
        @pl.when(pl.program_id(1) == pl.num_programs(1) - 1)
        def _():
            sout_ref[0] = s_ref[...]


def _hgrn(proj3, lb, ng, s_all, layer, *, stacked, tv):
    n_seq, t, _ = proj3.shape
    grid, c, seg, shape, imap, smap, nblk = _mixer_grid(n_seq, t, stacked)
    ng_ = len(grid)
    kern = functools.partial(_hgrn_kernel, c=c, seg=seg, tv=tv)
    col = lambda blk: pl.BlockSpec(shape(BRANCH_W), imap(blk))
    state_spec = pl.BlockSpec((nblk, HGRN_HEADS, HGRN_DH, HGRN_DH), smap(4))
    scratch = [] if stacked else [pltpu.VMEM((HGRN_HEADS, HGRN_DH, HGRN_DH), F32)]
    return pl.pallas_call(
        kern,
        grid=grid,
        in_specs=[col(COL_HQ), col(COL_HF), col(COL_HI), col(COL_HG),
                  _const_spec(lb, ng_), _const_spec(ng, ng_),
                  _layer_state_spec(s_all, layer, nblk, ng_)],
        out_specs=[col(0), state_spec],
        out_shape=[jax.ShapeDtypeStruct((n_seq, t, BRANCH_W), BF16),
                   jax.ShapeDtypeStruct(s_all.shape[1:], F32)],
        scratch_shapes=scratch,
        compiler_params=_cparams(ng_),
        name="hgrn2",
    )(proj3, proj3, proj3, proj3, lb, ng, s_all)


def _pad_cols(a, width):
    return jnp.pad(a, [(0, 0)] * (a.ndim - 1) + [(0, width - a.shape[-1])])


def _rwkv_cols_to_padded(a):
    o = 3 * BRANCH_W
    return jnp.concatenate([a[..., :o],
                            _pad_cols(a[..., o:o + RWKV_W_LORA], LANES),
                            _pad_cols(a[..., o + RWKV_W_LORA:o + RWKV_W_LORA + RWKV_A_LORA], LANES),
                            a[..., o + RWKV_W_LORA + RWKV_A_LORA:]], axis=-1)


def _rwkv_cols_from_padded(a):
    o = 3 * BRANCH_W
    return jnp.concatenate([a[..., :o], a[..., o:o + RWKV_W_LORA],
                            a[..., o + LANES:o + LANES + RWKV_A_LORA], a[..., o + 2 * LANES:]], axis=-1)


def _layout_w_in(w):
    s5 = w[:, :512]
    gdn = w[:, 512:2560]
    ba = w[:, 2560:2568]
    rwkv = w[:, 2568:4552]
    hgrn = w[:, 4552:6600]
    rp = _rwkv_cols_to_padded(rwkv)
    out = jnp.concatenate([s5, gdn, rp[:, :1536], hgrn, rp[:, 1792:2048], _pad_cols(ba, LANES),
                           rp[:, 1536:1664], rp[:, 1664:1792]], axis=-1)
    return _pad_cols(out, PROJ_COLS)


def _lane_vec(a, offset):
    return jnp.zeros((1, LANES), F32).at[0, offset:offset + a.shape[0]].set(a.astype(F32))


def _prep_layer(l, w, s5_chunks):
    row = lambda a: a.astype(F32).reshape(1, -1)
    p = dict(
        g_pre_mix=row(w['g_pre_mix'][l]),
        w_in=_layout_w_in(w['w_in'][l]).astype(BF16),
        w_gate=w['w_gate'][l].astype(BF16),
        w_br=w['w_br'][l].astype(BF16),
        w_o=w['w_o'][l].astype(BF16),
        g_post_mix=row(w['g_post_mix'][l]),
        s5_d=row(w['s5_d'][l]),
        s5_w_glu=w['s5_w_glu'][l].astype(BF16),
        gdn_conv_w=w['gdn_conv_w'][l].astype(F32),
        gdn_alog=_lane_vec(w['gdn_a_log'][l], GDN_HEADS),
        gdn_dtb=_lane_vec(w['gdn_dt_bias'][l], GDN_HEADS),
        gdn_norm_g=row(w['gdn_norm_g'][l]),
        rwkv_mu=_rwkv_cols_to_padded(row(w['rwkv_mu'][l])),
        rwkv_w0=row(w['rwkv_w0'][l]),
        rwkv_w2=jnp.pad(w['rwkv_w2'][l], ((0, LANES - RWKV_W_LORA), (0, 0))).astype(BF16),
        rwkv_a0=row(w['rwkv_a0'][l]),
        rwkv_a2=jnp.pad(w['rwkv_a2'][l], ((0, LANES - RWKV_A_LORA), (0, 0))).astype(BF16),
        rwkv_g2=w['rwkv_g2'][l].astype(BF16),
        rwkv_k_k=row(w['rwkv_k_k'][l]),
        rwkv_k_a=row(w['rwkv_k_a'][l]),
        rwkv_r_k=row(w['rwkv_r_k'][l]),
        rwkv_ln_g=row(w['rwkv_ln_g'][l]),
        rwkv_ln_b=row(w['rwkv_ln_b'][l]),
        hgrn_norm_g=row(w['hgrn_norm_g'][l]),
        g_pre_ffn=row(w['g_pre_ffn'][l]),
        w_up=w['w_up'][l].astype(BF16),
        ffn_conv_w=w['ffn_conv_w'][l].astype(F32),
        w_down=w['w_down'][l].astype(BF16),
        g_post_ffn=row(w['g_post_ffn'][l]),
    )
    s5_args = (w['s5_a_re'][l], w['s5_a_im'][l], w['s5_log_dt'][l], w['s5_b_re'][l], w['s5_b_im'][l],
               w['s5_c_re'][l], w['s5_c_im'][l])
    p['s5'] = {c: _s5_weights(*s5_args, c) for c in s5_chunks}
    return p


def _layer(x, st, p, lb, layer, *, n_seq, t, time_major):
    s5_re, s5_im, gdn_s, gdn_buf, rwkv_s, rwkv_prev, hgrn_s, ffn_buf = st
    rwkv_prev = _rwkv_cols_to_padded(rwkv_prev)
    if time_major:
        lo, hi = PRE_ROWS, PRE_ROWS + t
        pad_rows = lambda a, before: jnp.pad(a, ((0, 0), (before, SEQ_ROWS - before - a.shape[1]), (0, 0)))
        to_seq = lambda a: pad_rows(jnp.transpose(a.reshape(t, n_seq, a.shape[-1]), (1, 0, 2)), lo)
        from_seq = lambda a: jnp.transpose(a, (1, 0, 2)).reshape(t * n_seq, a.shape[-1])
        tokens = lambda a: a[:, lo:hi]
        cbuf = pad_rows(gdn_buf, 0)
        prev = pad_rows(rwkv_prev[:, None, :], lo - 1)
    else:
        to_seq = lambda a: a.reshape(n_seq, t, a.shape[-1])
        from_seq = lambda a: a.reshape(n_seq * t, a.shape[-1])
        tokens = lambda a: a
        cbuf = gdn_buf
        prev = rwkv_prev[:, None, :]

    proj = _norm_matmul(x, p['g_pre_mix'], p['w_in'], tn=1024)
    proj3 = to_seq(proj)

    s5c = min(S5_CHUNK, t)
    o_a, s5_re_n, s5_im_n = _s5_branch(tokens(proj3)[..., :BRANCH_W], s5_re, s5_im, p['s5'][s5c],
                                       p['s5_d'], p['s5_w_glu'], c=s5c)
    o_b, gdn_s_n, cout = _gdn(proj3, p['gdn_conv_w'], cbuf, p['gdn_alog'], p['gdn_dtb'],
                              p['gdn_norm_g'], gdn_s, layer, stacked=time_major, tv=t)
    o_c, rwkv_s_n, pout = _rwkv(proj3, prev, p['rwkv_mu'], p['rwkv_w0'], p['rwkv_w2'], p['rwkv_a0'],
                                p['rwkv_a2'], p['rwkv_g2'], p['rwkv_k_k'], p['rwkv_k_a'],
                                p['rwkv_r_k'], p['rwkv_ln_g'], p['rwkv_ln_b'], rwkv_s, layer,
                                stacked=time_major, tv=t)
    o_d, hgrn_s_n = _hgrn(proj3, lb, p['hgrn_norm_g'], hgrn_s, layer, stacked=time_major, tv=t)
    if time_major:
        gdn_buf_n = cout[:, hi - (GDN_CONV - 1):hi]
        rwkv_prev_n = _rwkv_cols_from_padded(pout[:, hi - 1])
    else:
        gdn_buf_n = cout
        rwkv_prev_n = _rwkv_cols_from_padded(pout[:, 0])

    branches = [from_seq(o_a)] + [from_seq(tokens(o)) for o in (o_b, o_c, o_d)]
    mix = _mix(x, p['g_pre_mix'], branches, p['w_gate'], p['w_br'])
    x = _matmul_resnorm(mix, p['w_o'], x, p['g_post_mix'])

    if time_major:
        buf = jnp.transpose(ffn_buf, (1, 0, 2)).reshape(1, (FFN_CONV - 1) * n_seq, 2 * D_FF)
        x, nbuf = _ffn(x, p['g_pre_ffn'], p['w_up'], p['ffn_conv_w'], buf, p['w_down'], p['g_post_ffn'],
                       stride=n_seq, rows_per_seq=t * n_seq)
        ffn_buf_n = jnp.transpose(nbuf.reshape(FFN_CONV - 1, n_seq, 2 * D_FF), (1, 0, 2))
    else:
        x, ffn_buf_n = _ffn(x, p['g_pre_ffn'], p['w_up'], p['ffn_conv_w'], ffn_buf, p['w_down'],
                            p['g_post_ffn'], stride=1, rows_per_seq=t)
    return x, (s5_re_n, s5_im_n, gdn_s_n, gdn_buf_n, rwkv_s_n, rwkv_prev_n, hgrn_s_n, ffn_buf_n)


def _run_group(x3, states, layers, lb_all, *, time_major):
    n_seq, t, d = x3.shape
    if time_major:
        x = jnp.transpose(x3, (1, 0, 2)).reshape(t * n_seq, d)
    else:
        x = x3.reshape(n_seq * t, d)
    new = [[] for _ in states]
    whole = (2, 4, 6)
    for l in range(DEPTH):
        st = [s if i in whole else s[l] for i, s in enumerate(states)]
        x, st = _layer(x, st, layers[l], lb_all[l:l + 1], l,
                       n_seq=n_seq, t=t, time_major=time_major)
        for lst, s in zip(new, st):
            lst.append(s)
    if time_major:
        y = jnp.transpose(x.reshape(t, n_seq, d), (1, 0, 2))
    else:
        y = x.reshape(n_seq, t, d)
    return y, [jnp.stack(lst) for lst in new]


def kernel(x_prompt, x_sample, state_s5_re, state_s5_im, state_gdn, state_gdn_conv, state_rwkv, state_rwkv_shift, state_hgrn, state_ffn_conv, g_pre_mix, w_in, w_gate, w_br, w_o, g_post_mix, s5_a_re, s5_a_im, s5_log_dt, s5_b_re, s5_b_im, s5_c_re, s5_c_im, s5_d, s5_w_glu, gdn_conv_w, gdn_a_log, gdn_dt_bias, gdn_norm_g, rwkv_mu, rwkv_w0, rwkv_w2, rwkv_a0, rwkv_a2, rwkv_g2, rwkv_k_k, rwkv_k_a, rwkv_r_k, rwkv_ln_g, rwkv_ln_b, hgrn_lb_logits, hgrn_norm_g, g_pre_ffn, w_up, ffn_conv_w, w_down, g_post_ffn):
    w = dict(g_pre_mix=g_pre_mix, w_in=w_in, w_gate=w_gate, w_br=w_br, w_o=w_o, g_post_mix=g_post_mix,
             s5_a_re=s5_a_re, s5_a_im=s5_a_im, s5_log_dt=s5_log_dt, s5_b_re=s5_b_re, s5_b_im=s5_b_im,
             s5_c_re=s5_c_re, s5_c_im=s5_c_im, s5_d=s5_d, s5_w_glu=s5_w_glu, gdn_conv_w=gdn_conv_w,
             gdn_a_log=gdn_a_log, gdn_dt_bias=gdn_dt_bias, gdn_norm_g=gdn_norm_g, rwkv_mu=rwkv_mu,
             rwkv_w0=rwkv_w0, rwkv_w2=rwkv_w2, rwkv_a0=rwkv_a0, rwkv_a2=rwkv_a2, rwkv_g2=rwkv_g2,
             rwkv_k_k=rwkv_k_k, rwkv_k_a=rwkv_k_a, rwkv_r_k=rwkv_r_k, rwkv_ln_g=rwkv_ln_g,
             rwkv_ln_b=rwkv_ln_b, hgrn_norm_g=hgrn_norm_g, g_pre_ffn=g_pre_ffn, w_up=w_up,
             ffn_conv_w=ffn_conv_w, w_down=w_down, g_post_ffn=g_post_ffn)
    sm = jax.nn.softmax(hgrn_lb_logits.astype(F32), axis=0)
    lb_all = jnp.maximum(jnp.cumsum(sm, axis=0) - sm[0], 0.0)
    s5_chunks = {min(S5_CHUNK, x_prompt.shape[1]), min(S5_CHUNK, x_sample.shape[1])}
    layers = [_prep_layer(l, w, s5_chunks) for l in range(DEPTH)]
    sample_states = (state_s5_re, state_s5_im, state_gdn, state_gdn_conv, state_rwkv,
                     state_rwkv_shift, state_hgrn, state_ffn_conv)
    nb = x_prompt.shape[0]
    prompt_states = tuple(jnp.zeros((DEPTH, nb) + s.shape[2:], F32) for s in sample_states)
    y_prompt, ps = _run_group(x_prompt, prompt_states, layers, lb_all, time_major=False)
    y_sample, ss = _run_group(x_sample, sample_states, layers, lb_all, time_major=True)
    out = [y_prompt, y_sample]
    for a, b in zip(ps, ss):
        out.extend((a, b))
    return tuple(out)
```

```python
import functools
import math

import jax
import jax.numpy as jnp
import numpy as np
from jax import lax
from jax.experimental import pallas as pl
from jax.experimental.pallas import tpu as pltpu

F32 = jnp.float32
BF16 = jnp.bfloat16

D_MODEL = 2048
DEPTH = 4
N_BRANCH = 4
BRANCH_W = 512
NORM_EPS = 1e-6

S5_GROUP = 16
S5_GROUPS = 32
S5_STATE = 64

GDN_HEADS = 4
GDN_DH = 128
GDN_CONV = 4

RWKV_DH = 64
RWKV_HEADS = 8
RWKV_W_LORA = 96
RWKV_A_LORA = 96
RWKV_G_LORA = 256
RWKV_LN_EPS = 64e-5
RWKV_COLS = 3 * BRANCH_W + RWKV_W_LORA + RWKV_A_LORA + RWKV_G_LORA

HGRN_HEADS = 4
HGRN_DH = 128
HGRN_SUB = 16
LB_TINY = 1e-30
EXP_CLAMP = 80.0

D_FF = 5632
FFN_CONV = 3

LANES = 128
SUBLANES = 8
VMEM_LIMIT_BYTES = 56 * 1024 * 1024

PROJ_COLS = 7168
COL_S5U, COL_GQ, COL_GK, COL_GV, COL_GZ = 0, 1, 2, 3, 4
COL_RR, COL_RK, COL_RV = 5, 6, 7
COL_HQ, COL_HF, COL_HI, COL_HG = 8, 9, 10, 11
COL_RGL = 6144 // 256
COL_GBA, COL_RWL, COL_RAL = 6400 // 128, 6528 // 128, 6656 // 128
RWKV_PAD_COLS = 2048

ROW_TILE = 512
FFN_SLAB = 128
PROMPT_CHUNK = 64
SEQ_ROWS = SUBLANES
PRE_ROWS = GDN_CONV - 1
SEQ_BLOCK = 8
S5_CHUNK = 16
S5_PERM = np.asarray([g * S5_GROUP + h for h in range(S5_GROUP) for g in range(S5_GROUPS)], np.int32)


def _cparams(n_axes):
    return pltpu.CompilerParams(dimension_semantics=("arbitrary",) * n_axes,
                                vmem_limit_bytes=VMEM_LIMIT_BYTES)


def _mm(a, b):
    return jnp.dot(a.astype(BF16), b.astype(BF16), preferred_element_type=F32)


def _mm_nt(a, b):
    return lax.dot_general(a.astype(BF16), b.astype(BF16), (((1,), (1,)), ((), ())),
                           preferred_element_type=F32)


def _mm_tn(a, b):
    return lax.dot_general(a.astype(BF16), b.astype(BF16), (((0,), (0,)), ((), ())),
                           preferred_element_type=F32)


def _split(a):
    hi = a.astype(BF16)
    lo = (a - hi.astype(F32)).astype(BF16)
    return hi, lo


def _mm3(a, b):
    ah, al = _split(a)
    bh, bl = _split(b)
    d = functools.partial(jnp.dot, preferred_element_type=F32)
    return d(ah, bh) + (d(ah, bl) + d(al, bh))


def _rms(x, eps):
    return x * lax.rsqrt(jnp.mean(x * x, axis=-1, keepdims=True) + eps)


def _sigmoid(x):
    return 1.0 / (1.0 + jnp.exp(-x))


def _silu(x):
    return x * _sigmoid(x)


def _softplus(x):
    return jnp.maximum(x, 0.0) + jnp.log1p(jnp.exp(-jnp.abs(x)))


def _gelu_tanh(x):
    return 0.5 * x * (1.0 + jnp.tanh(math.sqrt(2.0 / math.pi) * (x + 0.044715 * (x * x * x))))


def _seg_masks(c, seg, reps=1):
    r = lax.broadcasted_iota(jnp.int32, (c, reps * c), 0)
    s = lax.broadcasted_iota(jnp.int32, (c, reps * c), 1)
    if reps > 1:
        s = s % c
    causal, strict = r >= s, r > s
    if seg < c:
        same = (r // seg) == (s // seg)
        causal, strict = jnp.logical_and(causal, same), jnp.logical_and(strict, same)
    return causal, strict


def _unit_lower_inverses(mats, seg):
    c = mats[0].shape[0]
    eye = (lax.broadcasted_iota(jnp.int32, (c, c), 0)
           == lax.broadcasted_iota(jnp.int32, (c, c), 1)).astype(F32)
    ps = [-a for a in mats]
    ts = [eye + p for p in ps]
    k = 2
    while k < seg:
        ps = [_mm(p, p) for p in ps]
        ts = [t + _mm(t, p) for t, p in zip(ts, ps)]
        k *= 2
    return ts


def _unit_lower_inverse(a, seg):
    return _unit_lower_inverses([a], seg)[0]


def _rows(ref, c):
    return ref[0] if ref.shape[0] == 1 else ref[...].reshape(c, ref.shape[-1])


def _token_rows(c, width):
    return lax.broadcasted_iota(jnp.int32, (c, width), 0) % SEQ_ROWS


def _norm_matmul_kernel(x_ref, g_ref, w_ref, o_ref, h_ref):
    @pl.when(pl.program_id(1) == 0)
    def _():
        h_ref[...] = (_rms(x_ref[...], NORM_EPS) * g_ref[...]).astype(BF16)

    o_ref[...] = jnp.dot(h_ref[...], w_ref[...], preferred_element_type=F32)


def _norm_matmul(x, g, w_all, layer, *, tn):
    rows, d = x.shape
    n = w_all.shape[2]
    tm = min(ROW_TILE, rows)
    return pl.pallas_call(
        _norm_matmul_kernel,
        grid=(rows // tm, n // tn),
        in_specs=[pl.BlockSpec((tm, d), lambda i, j: (i, 0)),
                  pl.BlockSpec((1, d), lambda i, j: (0, 0)),
                  pl.BlockSpec((None, d, tn), lambda i, j: (layer, 0, j))],
        out_specs=pl.BlockSpec((tm, tn), lambda i, j: (i, j)),
        out_shape=jax.ShapeDtypeStruct((rows, n), F32),
        scratch_shapes=[pltpu.VMEM((tm, d), BF16)],
        compiler_params=_cparams(2),
        name="norm_matmul",
    )(x, g, w_all)


def _mix_kernel(x_ref, g_ref, oa_ref, ob_ref, oc_ref, od_ref, wga_ref, wgb_ref, wgc_ref, wgd_ref,
                wbr_ref, o_ref, h_ref):
    @pl.when(pl.program_id(1) == 0)
    def _():
        h_ref[...] = (_rms(x_ref[...], NORM_EPS) * g_ref[...]).astype(BF16)

    h = h_ref[...]
    acc = None
    for n, (br_ref, wg_ref) in enumerate(((oa_ref, wga_ref), (ob_ref, wgb_ref), (oc_ref, wgc_ref),
                                          (od_ref, wgd_ref))):
        gate = _sigmoid(jnp.dot(h, wg_ref[...], preferred_element_type=F32))
        term = gate * jnp.dot(br_ref[...], wbr_ref[n], preferred_element_type=F32)
        acc = term if acc is None else acc + term
    o_ref[...] = acc.astype(BF16)


def _mix(x, g, branches, wg, wbr, layer, *, tn=256):
    rows, d = x.shape
    tm = min(ROW_TILE, rows)
    bw = branches[0].shape[1]
    br_spec = pl.BlockSpec((tm, bw), lambda i, j: (i, 0))
    nt = d // tn
    wg_specs = [pl.BlockSpec((None, d, tn), functools.partial(lambda i, j, n: (layer, 0, n * nt + j), n=n))
                for n in range(N_BRANCH)]
    return pl.pallas_call(
        _mix_kernel,
        grid=(rows // tm, nt),
        in_specs=[pl.BlockSpec((tm, d), lambda i, j: (i, 0)),
                  pl.BlockSpec((1, d), lambda i, j: (0, 0)),
                  br_spec, br_spec, br_spec, br_spec, *wg_specs,
                  pl.BlockSpec((None, N_BRANCH, bw, tn), lambda i, j: (layer, 0, 0, j))],
        out_specs=pl.BlockSpec((tm, tn), lambda i, j: (i, j)),
        out_shape=jax.ShapeDtypeStruct((rows, d), BF16),
        scratch_shapes=[pltpu.VMEM((tm, d), BF16)],
        compiler_params=_cparams(2),
        name="branch_mix",
    )(x, g, *branches, wg, wg, wg, wg, wbr)


def _matmul_resnorm_kernel(a_ref, w_ref, x_ref, g_ref, o_ref):
    y = jnp.dot(a_ref[...], w_ref[...], preferred_element_type=F32)
    o_ref[...] = x_ref[...] + _rms(y, NORM_EPS) * g_ref[...]


def _matmul_resnorm(a, w_all, layer, x, g):
    rows, kdim = a.shape
    d = w_all.shape[2]
    tm = min(ROW_TILE, rows)
    return pl.pallas_call(
        _matmul_resnorm_kernel,
        grid=(rows // tm,),
        in_specs=[pl.BlockSpec((tm, kdim), lambda i: (i, 0)),
                  pl.BlockSpec((None, kdim, d), lambda i: (layer, 0, 0)),
                  pl.BlockSpec((tm, d), lambda i: (i, 0)),
                  pl.BlockSpec((1, d), lambda i: (0, 0))],
        out_specs=pl.BlockSpec((tm, d), lambda i: (i, 0)),
        out_shape=jax.ShapeDtypeStruct((rows, d), F32),
        compiler_params=_cparams(1),
        name="matmul_resnorm",
    )(a, w_all, x, g)


def _ffn_kernel(x_ref, gpre_ref, wa_ref, wb_ref, cwa_ref, cwb_ref, bufa_ref, bufb_ref, wd_ref,
                gpost_ref, o_ref, nbufa_ref, nbufb_ref, h_ref, acc_ref, act_ref, sa_ref, sb_ref, carry_ref,
                *, tm, stride, tiles_per_seq, nf):
    i = pl.program_id(0)
    f = pl.program_id(1)
    keep = (FFN_CONV - 1) * stride
    pad = max(SUBLANES, keep)
    ft = jnp.minimum(f, nf - 1)

    @pl.when(f == 0)
    def _():
        h_ref[...] = (_rms(x_ref[...], NORM_EPS) * gpre_ref[...]).astype(BF16)
        acc_ref[...] = jnp.zeros_like(acc_ref)
        sa_ref[...] = jnp.zeros_like(sa_ref)
        sb_ref[...] = jnp.zeros_like(sb_ref)

        @pl.when(i == 0)
        def _():
            carry_ref[...] = jnp.zeros_like(carry_ref)

    fp = jnp.maximum(f - 1, 0)
    cwa, cwb = cwa_ref[fp], cwb_ref[fp]
    slab = min(FFN_SLAB, tm)
    for r0 in range(0, tm, slab):
        ya = yb = None
        for j in range(FFN_CONV):
            off = pad - (FFN_CONV - 1 - j) * stride + r0
            ta = cwa[j:j + 1, :] * sa_ref[off:off + slab, :]
            tb = cwb[j:j + 1, :] * sb_ref[off:off + slab, :]
            ya = ta if ya is None else ya + ta
            yb = tb if yb is None else yb + tb
        act_ref[r0:r0 + slab, :] = (_gelu_tanh(ya) * yb).astype(BF16)

    h = h_ref[...]
    for idx, (w_ref, buf_ref, nbuf_ref, s_ref) in enumerate(
            ((wa_ref, bufa_ref, nbufa_ref, sa_ref), (wb_ref, bufb_ref, nbufb_ref, sb_ref))):
        if tiles_per_seq == 1:
            s_ref[pad - keep:pad, :] = buf_ref[0]
        else:
            s_ref[pad - keep:pad, :] = jnp.where((i % tiles_per_seq) == 0, buf_ref[0], carry_ref[ft, idx])
        s_ref[pad:pad + tm, :] = jnp.dot(h, w_ref[...], preferred_element_type=F32)
        last = s_ref[pad + tm - keep:pad + tm, :]
        nbuf_ref[0] = last
        if tiles_per_seq > 1:
            carry_ref[ft, idx] = last

    acc_ref[...] += jnp.dot(act_ref[...], wd_ref[...], preferred_element_type=F32)

    @pl.when(f == nf)
    def _():
        o_ref[...] = x_ref[...] + _rms(acc_ref[...], NORM_EPS) * gpost_ref[...]


def _ffn(x, gpre, w_up, conv_w, buf, w_down, gpost, layer, *, stride, rows_per_seq, tf=512):
    rows, d = x.shape
    tm = min(ROW_TILE, rows_per_seq)
    tiles_per_seq = rows_per_seq // tm
    nf = D_FF // tf
    keep = (FFN_CONV - 1) * stride
    pad = max(SUBLANES, keep)
    carry_rows = keep if tiles_per_seq > 1 else SUBLANES
    kern = functools.partial(_ffn_kernel, tm=tm, stride=stride, tiles_per_seq=tiles_per_seq, nf=nf)
    up = lambda f: jnp.minimum(f, nf - 1)
    down = lambda f: jnp.maximum(f - 1, 0)
    buf_a = pl.BlockSpec((1, keep, tf), lambda i, f: (i // tiles_per_seq, 0, up(f)))
    buf_b = pl.BlockSpec((1, keep, tf), lambda i, f: (i // tiles_per_seq, 0, nf + up(f)))
    cw = jnp.transpose(conv_w.reshape(FFN_CONV, 2 * nf, tf), (1, 0, 2))
    cw_spec = pl.BlockSpec((nf, FFN_CONV, tf), lambda i, f: (0, 0, 0))
    y, nbuf_a, nbuf_b = pl.pallas_call(
        kern,
        grid=(rows // tm, nf + 1),
        in_specs=[pl.BlockSpec((tm, d), lambda i, f: (i, 0)),
                  pl.BlockSpec((1, d), lambda i, f: (0, 0)),
                  pl.BlockSpec((None, d, tf), lambda i, f: (layer, 0, up(f))),
                  pl.BlockSpec((None, d, tf), lambda i, f: (layer, 0, nf + up(f))),
                  cw_spec, cw_spec,
                  buf_a, buf_b,
                  pl.BlockSpec((None, tf, d), lambda i, f: (layer, down(f), 0)),
                  pl.BlockSpec((1, d), lambda i, f: (0, 0))],
        out_specs=[pl.BlockSpec((tm, d), lambda i, f: (i, 0)),
                   pl.BlockSpec((1, keep, tf), lambda i, f: (i, 0, up(f))),
                   pl.BlockSpec((1, keep, tf), lambda i, f: (i, 0, up(f)))],
        out_shape=[jax.ShapeDtypeStruct((rows, d), F32),
                   jax.ShapeDtypeStruct((rows // tm, keep, D_FF), F32),
                   jax.ShapeDtypeStruct((rows // tm, keep, D_FF), F32)],
        scratch_shapes=[pltpu.VMEM((tm, d), BF16),
                        pltpu.VMEM((tm, d), F32),
                        pltpu.VMEM((tm, tf), BF16),
                        pltpu.VMEM((pad + tm, tf), F32),
                        pltpu.VMEM((pad + tm, tf), F32),
                        pltpu.VMEM((nf, 2, carry_rows, tf), F32)],
        compiler_params=_cparams(2),
        name="conv_ffn",
    )(x, gpre, w_up, w_up, cw[:nf], cw[nf:], buf, buf, w_down, gpost)
    nbuf = jnp.concatenate([nbuf_a, nbuf_b], axis=-1)
    return y, nbuf[tiles_per_seq - 1::tiles_per_seq]


def _s5_kernel(u_ref, h0_ref, bst_ref, kst_ref, cst_ref, apow_ref, y_ref, hfin_ref, e_ref,
               *, n_seq, n_chunks):
    rows = n_seq * n_chunks
    u = u_ref[0]
    h0 = h0_ref[0]
    half = S5_STATE
    if n_chunks > 1:
        row = lax.broadcasted_iota(jnp.int32, (rows, 2 * half), 0)
        placed = jnp.zeros((rows, 2 * half), F32)
        for b in range(n_seq):
            placed = jnp.where(row == b * n_chunks, h0[b:b + 1, :], placed)
        h0 = placed

    def cmul(k, x):
        return x * apow_ref[0, 2 * k:2 * k + 1, :] + pltpu.roll(x, half, 1) * apow_ref[0, 2 * k + 1:2 * k + 2, :]

    e = _mm3(u, bst_ref[0]) + cmul(0, h0)
    if n_chunks > 1:
        j = lax.broadcasted_iota(jnp.int32, (rows, 2 * half), 0) % n_chunks
        k, sh = 0, 1
        while sh < n_chunks:
            e = e + jnp.where(j >= sh, cmul(k, pltpu.roll(e, sh, 0)), 0.0)
            k, sh = k + 1, sh * 2
        e_ref[...] = e
        hfin_ref[0] = e_ref[pl.ds(n_chunks - 1, n_seq, stride=n_chunks), :]
        h_start = jnp.where(j >= 1, pltpu.roll(e, 1, 0), 0.0) + h0
    else:
        hfin_ref[0] = e
        h_start = h0
    y_ref[0] = _mm3(u, kst_ref[0]) + _mm3(h_start, cst_ref[0])


def _s5_scan(u_g, h0_g, bst, kst, cst, apow, *, n_seq, n_chunks):
    groups, rows, cw = u_g.shape
    kern = functools.partial(_s5_kernel, n_seq=n_seq, n_chunks=n_chunks)
    spec3 = lambda a: pl.BlockSpec((1,) + a.shape[1:], lambda g: (g, 0, 0))
    return pl.pallas_call(
        kern,
        grid=(groups,),
        in_specs=[spec3(u_g), spec3(h0_g), spec3(bst), spec3(kst), spec3(cst), spec3(apow)],
        out_specs=[pl.BlockSpec((1, rows, cw), lambda g: (g, 0, 0)),
                   pl.BlockSpec((1, n_seq, 2 * S5_STATE), lambda g: (g, 0, 0))],
        out_shape=[jax.ShapeDtypeStruct((groups, rows, cw), F32),
                   jax.ShapeDtypeStruct((groups, n_seq, 2 * S5_STATE), F32)],
        scratch_shapes=[pltpu.VMEM((rows, 2 * S5_STATE), F32)],
        compiler_params=_cparams(1),
        name="s5_scan",
    )(u_g, h0_g, bst, kst, cst, apow)


def _s5_glu_kernel(y_ref, u_ref, d_ref, w_ref, o_ref):
    g = _gelu_tanh(y_ref[...] + d_ref[...] * u_ref[...])
    o_ref[...] = (g * _sigmoid(_mm(g, w_ref[...]))).astype(BF16)


def _s5_glu(y, u, d, w_glu):
    rows, bw = y.shape
    tm = min(ROW_TILE, rows)
    row_spec = pl.BlockSpec((tm, bw), lambda i: (i, 0))
    return pl.pallas_call(
        _s5_glu_kernel,
        grid=(rows // tm,),
        in_specs=[row_spec, row_spec,
                  pl.BlockSpec((1, bw), lambda i: (0, 0)),
                  pl.BlockSpec((bw, bw), lambda i: (0, 0))],
        out_specs=row_spec,
        out_shape=jax.ShapeDtypeStruct((rows, bw), BF16),
        compiler_params=_cparams(1),
        name="s5_glu",
    )(y, u, d, w_glu)


def _s5_weights(a_re, a_im, log_dt, b_re, b_im, c_re, c_im, c):
    lam = lax.complex(a_re.astype(F32), a_im.astype(F32))
    ldt = lam * jnp.exp(log_dt.astype(F32))[:, None]
    a_bar = jnp.exp(ldt)
    b_bar = ((a_bar - 1.0) / lam)[..., None] * lax.complex(b_re.astype(F32), b_im.astype(F32))
    cm = lax.complex(c_re.astype(F32), c_im.astype(F32))
    tau = jnp.arange(c + 1, dtype=F32)
    apw = jnp.exp(ldt[None] * tau[:, None, None])
    bst = apw[:c][::-1][:, :, :, None] * b_bar[None]
    bst = jnp.transpose(bst, (1, 0, 3, 2)).reshape(S5_GROUPS, c * S5_GROUP, S5_STATE)
    bst = jnp.concatenate([jnp.real(bst), jnp.imag(bst)], axis=-1)
    kt = jnp.real(jnp.einsum('ghp,tgp,gpk->tgkh', cm, apw[:c], b_bar))
    s_idx = jnp.arange(c)[:, None]
    t_idx = jnp.arange(c)[None, :]
    kst = jnp.where((t_idx >= s_idx)[:, :, None, None, None],
                    kt[jnp.clip(t_idx - s_idx, 0, c - 1)], 0.0)
    kst = jnp.transpose(kst, (2, 0, 3, 1, 4)).reshape(S5_GROUPS, c * S5_GROUP, c * S5_GROUP)
    ca = cm[None] * apw[1:c + 1][:, :, None, :]
    ca = jnp.transpose(ca, (1, 3, 0, 2)).reshape(S5_GROUPS, S5_STATE, c * S5_GROUP)
    cst = jnp.concatenate([jnp.real(ca), -jnp.imag(ca)], axis=1)
    rows = []
    k = 0
    while True:
        p = jnp.exp(ldt * float(c * 2 ** k))
        rows.append(jnp.concatenate([jnp.real(p), jnp.real(p)], axis=-1))
        rows.append(jnp.concatenate([-jnp.imag(p), jnp.imag(p)], axis=-1))
        k += 1
        if c * 2 ** k > 4096:
            break
    apow = jnp.stack(rows, axis=1)
    return bst, kst, cst, apow


def _s5_branch(u3, h0_re, h0_im, sw, d_skip, w_glu, *, c):
    n_seq, t, _ = u3.shape
    n_chunks = t // c
    bst, kst, cst, apow = sw
    u2 = u3.reshape(n_seq * t, BRANCH_W)
    u_g = u2.reshape(n_seq * t * S5_GROUP, S5_GROUPS).T
    u_g = u_g.reshape(S5_GROUPS, n_seq * n_chunks, c * S5_GROUP)
    h0 = jnp.concatenate([h0_re, h0_im], axis=-1)
    h0 = jnp.transpose(h0, (1, 0, 2))
    y_g, hfin = _s5_scan(u_g, h0, bst, kst, cst, apow, n_seq=n_seq, n_chunks=n_chunks)
    y = y_g.reshape(S5_GROUPS, n_seq * t * S5_GROUP).T.reshape(n_seq * t, BRANCH_W)
    o = _s5_glu(y, u2, d_skip, w_glu)
    hfin = jnp.transpose(hfin, (1, 0, 2))
    return o.reshape(n_seq, t, BRANCH_W), hfin[..., :S5_STATE], hfin[..., S5_STATE:]


def _mixer_grid(n_seq, t, stacked):
    if stacked:
        assert t == SEQ_ROWS and n_seq % SEQ_BLOCK == 0
        c = SEQ_BLOCK * SEQ_ROWS
        shape = lambda w: (SEQ_BLOCK, SEQ_ROWS, w)
        imap = lambda blk: (lambda b: (b, 0, blk))
        smap = lambda nd: (lambda b: (b,) + (0,) * (nd - 1))
        return (n_seq // SEQ_BLOCK,), c, SEQ_ROWS, shape, imap, smap, SEQ_BLOCK
    assert t % PROMPT_CHUNK == 0
    c = PROMPT_CHUNK
    shape = lambda w: (1, c, w)
    imap = lambda blk: (lambda b, i: (b, i, blk))
    smap = lambda nd: (lambda b, i: (b,) + (0,) * (nd - 1))
    return (n_seq, t // c), c, c, shape, imap, smap, 1


def _layer_state_spec(s_all, layer, nblk, n_grid):
    tail = (0,) * (s_all.ndim - 2)
    imap = (lambda b: (layer, b) + tail) if n_grid == 1 else (lambda b, i: (layer, b) + tail)
    return pl.BlockSpec((None, nblk) + s_all.shape[2:], imap)


def _const_spec(a, n_grid):
    zeros = (0,) * a.ndim
    return pl.BlockSpec(a.shape, (lambda b: zeros) if n_grid == 1 else (lambda b, i: zeros))


def _gdn_kernel(pq_ref, pk_ref, pv_ref, pz_ref, pba_ref, cw_ref, cbuf_ref, alog_ref, dtb_ref,
                ng_ref, s0_ref, acc_ref, o_ref, sout_ref, cout_ref, *scratch, c, seg, tv):
    stacked = seg < c
    keep = GDN_CONV - 1
    x = jnp.concatenate([_rows(pq_ref, c), _rows(pk_ref, c), _rows(pv_ref, c)], axis=1)
    if stacked:
        (conv_ref,) = scratch
        t_in = _token_rows(c, 3 * BRANCH_W)
        x = jnp.where(t_in < keep, _rows(cbuf_ref, c), x)
        conv_ref[0:SUBLANES, :] = jnp.zeros((SUBLANES, 3 * BRANCH_W), F32)
        cout_ref[...] = x.reshape(cout_ref.shape)
    else:
        s_ref, conv_ref = scratch

        @pl.when(pl.program_id(1) == 0)
        def _():
            s_ref[...] = s0_ref[0]
            conv_ref[SUBLANES - keep:SUBLANES, :] = cbuf_ref[0]

    conv_ref[SUBLANES:SUBLANES + c, :] = x
    cw = cw_ref[...]
    y = None
    for j in range(GDN_CONV):
        off = SUBLANES - keep + j
        term = cw[j:j + 1, :] * conv_ref[off:off + c, :]
        y = term if y is None else y + term
    if not stacked:
        last = conv_ref[SUBLANES + c - keep:SUBLANES + c, :]
        conv_ref[SUBLANES - keep:SUBLANES, :] = last
        cout_ref[0] = last
    act = _silu(y)

    ba = _rows(pba_ref, c)
    beta_all = _sigmoid(ba)
    g_all = -jnp.exp(alog_ref[...]) * _softplus(ba + dtb_ref[...])
    if stacked:
        t_in = _token_rows(c, LANES)
        valid = jnp.logical_and(t_in >= PRE_ROWS, t_in < PRE_ROWS + tv)
        beta_all = jnp.where(valid, beta_all, 0.0)
        g_all = jnp.where(valid, g_all, 0.0)
    causal, strict = _seg_masks(c, seg)
    d_all = _mm3(causal.astype(F32), g_all)
    dt_all = d_all.T
    ng = ng_ref[...]
    z = _rows(pz_ref, c)
    n_seg = c // seg

    heads = range(GDN_HEADS)
    sls = [slice(h * GDN_DH, (h + 1) * GDN_DH) for h in heads]
    segs = [slice(j * seg, (j + 1) * seg) for j in range(n_seg)]
    cat0 = lambda xs: xs[0] if len(xs) == 1 else jnp.concatenate(xs, axis=0)
    state = lambda j, h: s0_ref[j, h] if stacked else s_ref[h]
    q = [act[:, sl] for sl in sls]
    k = [act[:, BRANCH_W + h * GDN_DH:BRANCH_W + (h + 1) * GDN_DH] for h in heads]
    v = [act[:, 2 * BRANCH_W + h * GDN_DH:2 * BRANCH_W + (h + 1) * GDN_DH] for h in heads]
    q = [x * lax.rsqrt(jnp.sum(x * x, axis=-1, keepdims=True) + 1e-6) * (GDN_DH ** -0.5) for x in q]
    k = [x * lax.rsqrt(jnp.sum(x * x, axis=-1, keepdims=True) + 1e-6) for x in k]
    beta = [beta_all[:, h:h + 1] for h in heads]
    d = [d_all[:, GDN_HEADS + h:GDN_HEADS + h + 1] for h in heads]
    d_row = [dt_all[GDN_HEADS + h:GDN_HEADS + h + 1, :] for h in heads]
    decay = [jnp.where(causal, jnp.exp(jnp.where(causal, d[h] - d_row[h], 0.0)), 0.0) for h in heads]
    kb = [k[h] * beta[h] for h in heads]
    prod = [_mm_nt(jnp.concatenate([kb[h], q[h]], axis=0), k[h]) for h in heads]
    m = [jnp.where(strict, prod[h][:c] * decay[h], 0.0) for h in heads]
    attn = [prod[h][c:] * decay[h] for h in heads]
    t_inv = _unit_lower_inverses(m, seg)
    ed = [jnp.exp(x) for x in d]
    sol = [_mm(t_inv[h], jnp.concatenate([v[h] * beta[h], kb[h] * ed[h]], axis=1)) for h in heads]
    qe = [q[h] * ed[h] for h in heads]
    both = [[_mm(jnp.concatenate([sol[h][rows, GDN_DH:], qe[h][rows]], axis=0), state(j, h))
             for j, rows in enumerate(segs)] for h in heads]
    u = [cat0([sol[h][rows, :GDN_DH] - both[h][j][:seg] for j, rows in enumerate(segs)]) for h in heads]
    qs = [cat0([b[seg:] for b in both[h]]) for h in heads]
    o = [qs[h] + _mm(attn[h], u[h]) for h in heads]
    for h in heads:
        for j, rows in enumerate(segs):
            dl = d[h][(j + 1) * seg - 1:(j + 1) * seg, :]
            s_new = state(j, h) * jnp.exp(dl) + _mm_tn(k[h][rows] * jnp.exp(dl - d[h][rows]), u[h][rows])
            if stacked:
                sout_ref[j, h] = s_new
            else:
                s_ref[h] = s_new
    outs = [_rms(o[h], NORM_EPS) * ng * _silu(z[:, sls[h]]) for h in heads]
    o_ref[...] = jnp.concatenate(outs, axis=1).reshape(o_ref.shape).astype(BF16)

    if not stacked:
        @pl.when(pl.program_id(1) == pl.num_programs(1) - 1)
        def _():
            sout_ref[0] = s_ref[...]


def _gdn(proj3, conv_w, cbuf, alog, dtb, ng, s_all, acc, layer, *, stacked, tv):
    n_seq, t, _ = proj3.shape
    grid, c, seg, shape, imap, smap, nblk = _mixer_grid(n_seq, t, stacked)
    ng_ = len(grid)
    kern = functools.partial(_gdn_kernel, c=c, seg=seg, tv=tv)
    col = lambda blk, w: pl.BlockSpec(shape(w), imap(blk))
    cbuf_spec = pl.BlockSpec((nblk,) + cbuf.shape[1:], smap(3))
    scratch = [pltpu.VMEM((SUBLANES + c, 3 * BRANCH_W), F32)]
    if not stacked:
        scratch = [pltpu.VMEM((GDN_HEADS, GDN_DH, GDN_DH), F32)] + scratch
    return pl.pallas_call(
        kern,
        grid=grid,
        in_specs=[col(COL_GQ, BRANCH_W), col(COL_GK, BRANCH_W), col(COL_GV, BRANCH_W),
                  col(COL_GZ, BRANCH_W), col(COL_GBA, LANES),
                  _const_spec(conv_w, ng_), cbuf_spec, _const_spec(alog, ng_), _const_spec(dtb, ng_),
                  _const_spec(ng, ng_), _layer_state_spec(s_all, layer, nblk, ng_),
                  pl.BlockSpec(memory_space=pl.ANY)],
        out_specs=[col(0, BRANCH_W), _layer_state_spec(acc, layer, nblk, ng_), cbuf_spec],
        out_shape=[jax.ShapeDtypeStruct((n_seq, t, BRANCH_W), BF16),
                   jax.ShapeDtypeStruct(acc.shape, F32),
                   jax.ShapeDtypeStruct(cbuf.shape, F32)],
        input_output_aliases={11: 1},
        scratch_shapes=scratch,
        compiler_params=_cparams(ng_),
        name="gdn",
    )(proj3, proj3, proj3, proj3, proj3, conv_w, cbuf, alog, dtb, ng, s_all, acc)


def _rwkv_kernel(pr_ref, pk_ref, pv_ref, pwl_ref, pal_ref, pgl_ref, prev_ref, mu_ref, w0_ref, w2_ref,
                 a0_ref, a2_ref, g2_ref, kk_ref, ka_ref, rk_ref, lng_ref, lnb_ref, s0_ref, acc_ref,
                 o_ref, sout_ref, pout_ref, *scratch, c, seg, tv):
    stacked = seg < c
    p = jnp.concatenate([_rows(pr_ref, c), _rows(pk_ref, c), _rows(pv_ref, c), _rows(pwl_ref, c),
                         _rows(pal_ref, c), _rows(pgl_ref, c)], axis=1)
    if stacked:
        (p_ref,) = scratch
        t_in = _token_rows(c, RWKV_PAD_COLS)
        p = jnp.where(t_in == PRE_ROWS - 1, _rows(prev_ref, c), p)
        p_ref[0:SUBLANES, :] = jnp.zeros((SUBLANES, RWKV_PAD_COLS), F32)
        pout_ref[...] = p.reshape(pout_ref.shape)
    else:
        s_ref, p_ref = scratch

        @pl.when(pl.program_id(1) == 0)
        def _():
            s_ref[...] = s0_ref[0]
            p_ref[SUBLANES - 1:SUBLANES, :] = prev_ref[0]

    p_ref[SUBLANES:SUBLANES + c, :] = p
    prev = p_ref[SUBLANES - 1:SUBLANES - 1 + c, :]
    if not stacked:
        last = p_ref[SUBLANES + c - 1:SUBLANES + c, :]
        p_ref[SUBLANES - 1:SUBLANES, :] = last
        pout_ref[0] = last
    pm = p + (prev - p) * mu_ref[...]
    r = pm[:, :BRANCH_W]
    k = pm[:, BRANCH_W:2 * BRANCH_W]
    v = pm[:, 2 * BRANCH_W:3 * BRANCH_W]
    wl = pm[:, 3 * BRANCH_W:3 * BRANCH_W + LANES]
    al = pm[:, 3 * BRANCH_W + LANES:3 * BRANCH_W + 2 * LANES]
    gl = pm[:, 3 * BRANCH_W + 2 * LANES:]
    w_log = -_softplus(-(w0_ref[...] + _mm(jnp.tanh(wl), w2_ref[...]))) - 0.5
    lw = -jnp.exp(w_log)
    a = _sigmoid(a0_ref[...] + _mm(al, a2_ref[...]))
    gate = _mm(_sigmoid(gl), g2_ref[...])
    kk_un = k * kk_ref[...]
    k2 = k * (1.0 + (a - 1.0) * ka_ref[...])
    if stacked:
        t_in = _token_rows(c, BRANCH_W)
        valid = jnp.logical_and(t_in >= PRE_ROWS, t_in < PRE_ROWS + tv)
        lw = jnp.where(valid, lw, 0.0)
        kk_un = jnp.where(valid, kk_un, 0.0)
        k2 = jnp.where(valid, k2, 0.0)
    causal, _ = _seg_masks(c, seg)
    causal2, strict2 = _seg_masks(c, seg, reps=2)
    g_cum = _mm3(causal.astype(F32), lw)
    e_pos = jnp.exp(g_cum)
    e_neg = jnp.exp(-g_cum)
    e_prev = jnp.exp(g_cum - lw)
    rk_w = rk_ref[...]
    lng = lng_ref[...]
    lnb = lnb_ref[...]
    n_seg = c // seg

    heads = range(RWKV_HEADS)
    sls = [slice(h * RWKV_DH, (h + 1) * RWKV_DH) for h in heads]
    segs = [slice(j * seg, (j + 1) * seg) for j in range(n_seg)]
    cat0 = lambda xs: xs[0] if len(xs) == 1 else jnp.concatenate(xs, axis=0)
    state = lambda j, h: s0_ref[j, h] if stacked else s_ref[h]
    kkh = [kk_un[:, sl] for sl in sls]
    kkh = [x * lax.rsqrt(jnp.sum(x * x, axis=-1, keepdims=True) + 1e-6) for x in kkh]
    alpha_hat = [kkh[h] * a[:, sls[h]] * e_neg[:, sls[h]] for h in heads]
    k_hat = [k2[:, sl] * e_neg[:, sl] for sl in sls]
    kap = [kkh[h] * e_prev[:, sls[h]] for h in heads]
    r_t = [r[:, sl] * e_pos[:, sl] for sl in sls]
    vh = [v[:, sl] for sl in sls]
    rhs = [jnp.concatenate([alpha_hat[h], k_hat[h]], axis=0) for h in heads]
    ab = [jnp.where(strict2, _mm_nt(kap[h], rhs[h]), 0.0) for h in heads]
    rr = [jnp.where(causal2, _mm_nt(r_t[h], rhs[h]), 0.0) for h in heads]
    both = [[_mm_nt(jnp.concatenate([kap[h][rows], r_t[h][rows]], axis=0), state(j, h))
             for j, rows in enumerate(segs)] for h in heads]
    bv = [_mm(jnp.concatenate([ab[h][:, c:], rr[h][:, c:]], axis=0), vh[h]) for h in heads]
    t_inv = _unit_lower_inverses([ab[h][:, :c] for h in heads], seg)
    ks = [cat0([b[:seg] for b in both[h]]) for h in heads]
    rs = [cat0([b[seg:] for b in both[h]]) for h in heads]
    u = [_mm(t_inv[h], ks[h] + bv[h][:c]) for h in heads]
    y = [rs[h] + bv[h][c:] - _mm(rr[h][:, :c], u[h]) for h in heads]
    for h in heads:
        for j, rows in enumerate(segs):
            el = e_pos[(j + 1) * seg - 1:(j + 1) * seg, sls[h]]
            s_new = state(j, h) * el + _mm_tn(
                jnp.concatenate([vh[h][rows], -u[h][rows]], axis=0),
                jnp.concatenate([k_hat[h][rows] * el, alpha_hat[h][rows] * el], axis=0))
            if stacked:
                sout_ref[j, h] = s_new
            else:
                s_ref[h] = s_new
    outs = []
    for h in heads:
        sl = sls[h]
        mu = jnp.mean(y[h], axis=-1, keepdims=True)
        yc = y[h] - mu
        var = jnp.mean(yc * yc, axis=-1, keepdims=True)
        yn = yc * lax.rsqrt(var + RWKV_LN_EPS) * lng[:, sl] + lnb[:, sl]
        bonus = jnp.sum(r[:, sl] * k2[:, sl] * rk_w[:, sl], axis=-1, keepdims=True) * vh[h]
        outs.append(yn + bonus)
    o_ref[...] = (jnp.concatenate(outs, axis=1) * gate).reshape(o_ref.shape).astype(BF16)

    if not stacked:
        @pl.when(pl.program_id(1) == pl.num_programs(1) - 1)
        def _():
            sout_ref[0] = s_ref[...]


def _rwkv(proj3, prev, mu, w0, w2, a0, a2, g2, k_k, k_a, r_k, ln_g, ln_b, s_all, acc, layer, *,
          stacked, tv):
    n_seq, t, _ = proj3.shape
    grid, c, seg, shape, imap, smap, nblk = _mixer_grid(n_seq, t, stacked)
    ng_ = len(grid)
    kern = functools.partial(_rwkv_kernel, c=c, seg=seg, tv=tv)
    col = lambda blk, w: pl.BlockSpec(shape(w), imap(blk))
    cs = lambda a: _const_spec(a, ng_)
    prev_spec = pl.BlockSpec((nblk,) + prev.shape[1:], smap(3))
    scratch = [pltpu.VMEM((SUBLANES + c, RWKV_PAD_COLS), F32)]
    if not stacked:
        scratch = [pltpu.VMEM((RWKV_HEADS, RWKV_DH, RWKV_DH), F32)] + scratch
    return pl.pallas_call(
        kern,
        grid=grid,
        in_specs=[col(COL_RR, BRANCH_W), col(COL_RK, BRANCH_W), col(COL_RV, BRANCH_W),
                  col(COL_RWL, LANES), col(COL_RAL, LANES), col(COL_RGL, 2 * LANES),
                  prev_spec, cs(mu), cs(w0), cs(w2), cs(a0), cs(a2), cs(g2),
                  cs(k_k), cs(k_a), cs(r_k), cs(ln_g), cs(ln_b),
                  _layer_state_spec(s_all, layer, nblk, ng_), pl.BlockSpec(memory_space=pl.ANY)],
        out_specs=[col(0, BRANCH_W), _layer_state_spec(acc, layer, nblk, ng_), prev_spec],
        out_shape=[jax.ShapeDtypeStruct((n_seq, t, BRANCH_W), BF16),
                   jax.ShapeDtypeStruct(acc.shape, F32),
                   jax.ShapeDtypeStruct(prev.shape, F32)],
        input_output_aliases={19: 1},
        scratch_shapes=scratch,
        compiler_params=_cparams(ng_),
        name="rwkv7",
    )(proj3, proj3, proj3, proj3, proj3, proj3, prev, mu, w0, w2, a0, a2, g2, k_k, k_a, r_k,
      ln_g, ln_b, s_all, acc)


def _hgrn_kernel(pq_ref, pf_ref, pi_ref, pg_ref, lb_ref, ng_ref, s0_ref, acc_ref, o_ref, sout_ref, *scratch,
                 c, seg, tv):
    stacked = seg < c
    if not stacked:
        (s_ref,) = scratch

        @pl.when(pl.program_id(1) == 0)
        def _():
            s_ref[...] = s0_ref[0]

    lb = lb_ref[...]
    hf = _rows(pf_ref, c)
    log_sig = jnp.minimum(hf, 0.0) - jnp.log1p(jnp.exp(-jnp.abs(hf)))
    x1 = jnp.log(jnp.maximum(lb, LB_TINY))
    x2 = jnp.log1p(-lb) + log_sig
    logf = jnp.maximum(x1, x2) + jnp.log1p(jnp.exp(-jnp.abs(x1 - x2)))
    kf = (1.0 - lb) * _sigmoid(-hf)
    q = _silu(_rows(pq_ref, c))
    v = _rows(pi_ref, c)
    if stacked:
        t_in = _token_rows(c, BRANCH_W)
        valid = jnp.logical_and(t_in >= PRE_ROWS, t_in < PRE_ROWS + tv)
        logf = jnp.where(valid, logf, 0.0)
        kf = jnp.where(valid, kf, 0.0)
    sub = seg if stacked else HGRN_SUB
    n_sub = c // sub
    blk_causal, _ = _seg_masks(c, sub)
    b_all = _mm3(blk_causal.astype(F32), logf)
    causal, _ = _seg_masks(sub, sub)
    mid = sub // 2 - 1
    b_last_rows = jnp.concatenate([b_all[(j + 1) * sub - 1:(j + 1) * sub, :] for j in range(n_sub)]
                                  + [jnp.zeros((SUBLANES - n_sub, BRANCH_W), F32)] * (n_sub < SUBLANES),
                                  axis=0)
    decay_cols = jnp.exp(b_last_rows).T
    ng = ng_ref[...]
    gate = _silu(_rows(pg_ref, c))

    heads = range(HGRN_HEADS)
    sls = [slice(h * HGRN_DH, (h + 1) * HGRN_DH) for h in heads]
    pairs = [(j, h) for j in range(n_sub) for h in heads]
    blk = lambda x, j, h: x[j * sub:(j + 1) * sub, sls[h]]
    e_mid, e_last = {}, {}
    for j, h in pairs:
        b = blk(b_all, j, h)
        e_mid[j, h] = b - b[mid:mid + 1, :]
        e_last[j, h] = b[sub - 1:sub, :] - b
    attn = {jh: jnp.where(causal, _mm_nt(blk(q, *jh) * jnp.exp(jnp.minimum(e_mid[jh], EXP_CLAMP)),
                                         blk(kf, *jh) * jnp.exp(jnp.minimum(-e_mid[jh], EXP_CLAMP))), 0.0)
            for jh in pairs}
    intra = {jh: _mm(attn[jh], blk(v, *jh)) for jh in pairs}
    q_dec = {jh: blk(q, *jh) * jnp.exp(blk(b_all, *jh)) for jh in pairs}
    kv = {jh: _mm_tn(blk(kf, *jh) * jnp.exp(e_last[jh]), blk(v, *jh)) for jh in pairs}
    outs = [[] for _ in heads]
    for j in range(n_sub):
        s = [s0_ref[j, h] if stacked else s_ref[h] for h in heads]
        o = [_mm(q_dec[j, h], s[h]) + intra[j, h] for h in heads]
        for h in heads:
            s_new = s[h] * decay_cols[sls[h], j:j + 1] + kv[j, h]
            if stacked:
                sout_ref[j, h] = s_new
            else:
                s_ref[h] = s_new
            outs[h].append(o[h])
    cols = []
    for h in range(HGRN_HEADS):
        sl = slice(h * HGRN_DH, (h + 1) * HGRN_DH)
        o = outs[h][0] if n_sub == 1 else jnp.concatenate(outs[h], axis=0)
        cols.append(_rms(o, NORM_EPS) * ng * gate[:, sl])
    o_ref[...] = jnp.concatenate(cols, axis=1).reshape(o_ref.shape).astype(BF16)

    if not stacked:
        @pl.when(pl.program_id(1) == pl.num_programs(1) - 1)
        def _():
            sout_ref[0] = s_ref[...]


def _hgrn(proj3, lb, ng, s_all, acc, layer, *, stacked, tv):
    n_seq, t, _ = proj3.shape
    grid, c, seg, shape, imap, smap, nblk = _mixer_grid(n_seq, t, stacked)
    ng_ = len(grid)
    kern = functools.partial(_hgrn_kernel, c=c, seg=seg, tv=tv)
    col = lambda blk: pl.BlockSpec(shape(BRANCH_W), imap(blk))
    scratch = [] if stacked else [pltpu.VMEM((HGRN_HEADS, HGRN_DH, HGRN_DH), F32)]
    return pl.pallas_call(
        kern,
        grid=grid,
        in_specs=[col(COL_HQ), col(COL_HF), col(COL_HI), col(COL_HG),
                  _const_spec(lb, ng_), _const_spec(ng, ng_),
                  _layer_state_spec(s_all, layer, nblk, ng_), pl.BlockSpec(memory_space=pl.ANY)],
        out_specs=[col(0), _layer_state_spec(acc, layer, nblk, ng_)],
        out_shape=[jax.ShapeDtypeStruct((n_seq, t, BRANCH_W), BF16),
                   jax.ShapeDtypeStruct(acc.shape, F32)],
        input_output_aliases={7: 1},
        scratch_shapes=scratch,
        compiler_params=_cparams(ng_),
        name="hgrn2",
    )(proj3, proj3, proj3, proj3, lb, ng, s_all, acc)


def _pad_cols(a, width):
    return jnp.pad(a, [(0, 0)] * (a.ndim - 1) + [(0, width - a.shape[-1])])


def _rwkv_cols_to_padded(a):
    o = 3 * BRANCH_W
    return jnp.concatenate([a[..., :o],
                            _pad_cols(a[..., o:o + RWKV_W_LORA], LANES),
                            _pad_cols(a[..., o + RWKV_W_LORA:o + RWKV_W_LORA + RWKV_A_LORA], LANES),
                            a[..., o + RWKV_W_LORA + RWKV_A_LORA:]], axis=-1)


def _rwkv_cols_from_padded(a):
    o = 3 * BRANCH_W
    return jnp.concatenate([a[..., :o], a[..., o:o + RWKV_W_LORA],
                            a[..., o + LANES:o + LANES + RWKV_A_LORA], a[..., o + 2 * LANES:]], axis=-1)


def _layout_w_in(w):
    s5 = w[..., :512][..., S5_PERM]
    gdn = w[..., 512:2560]
    ba = w[..., 2560:2568]
    rwkv = w[..., 2568:4552]
    hgrn = w[..., 4552:6600]
    rp = _rwkv_cols_to_padded(rwkv)
    out = jnp.concatenate([s5, gdn, rp[..., :1536], hgrn, rp[..., 1792:2048], _pad_cols(ba, LANES),
                           rp[..., 1536:1664], rp[..., 1664:1792]], axis=-1)
    return _pad_cols(out, PROJ_COLS)


def _prep_big(w):
    return dict(
        w_in=_layout_w_in(w['w_in'].astype(BF16)),
        w_gate=w['w_gate'].astype(BF16),
        w_br=jnp.concatenate([w['w_br'][:, :1][:, :, S5_PERM, :], w['w_br'][:, 1:]], axis=1).astype(BF16),
        w_o=w['w_o'].astype(BF16),
        w_up=w['w_up'].astype(BF16),
        w_down=w['w_down'].astype(BF16),
    )


def _lane_vec(a, offset):
    return jnp.zeros((1, LANES), F32).at[0, offset:offset + a.shape[0]].set(a.astype(F32))


def _prep_layer(l, w, s5_chunks):
    row = lambda a: a.astype(F32).reshape(1, -1)
    p = dict(
        g_pre_mix=row(w['g_pre_mix'][l]),
        g_post_mix=row(w['g_post_mix'][l]),
        s5_d=row(w['s5_d'][l])[:, S5_PERM],
        s5_w_glu=w['s5_w_glu'][l][S5_PERM][:, S5_PERM].astype(BF16),
        gdn_conv_w=w['gdn_conv_w'][l].astype(F32),
        gdn_alog=_lane_vec(w['gdn_a_log'][l], GDN_HEADS),
        gdn_dtb=_lane_vec(w['gdn_dt_bias'][l], GDN_HEADS),
        gdn_norm_g=row(w['gdn_norm_g'][l]),
        rwkv_mu=_rwkv_cols_to_padded(row(w['rwkv_mu'][l])),
        rwkv_w0=row(w['rwkv_w0'][l]),
        rwkv_w2=jnp.pad(w['rwkv_w2'][l], ((0, LANES - RWKV_W_LORA), (0, 0))).astype(BF16),
        rwkv_a0=row(w['rwkv_a0'][l]),
        rwkv_a2=jnp.pad(w['rwkv_a2'][l], ((0, LANES - RWKV_A_LORA), (0, 0))).astype(BF16),
        rwkv_g2=w['rwkv_g2'][l].astype(BF16),
        rwkv_k_k=row(w['rwkv_k_k'][l]),
        rwkv_k_a=row(w['rwkv_k_a'][l]),
        rwkv_r_k=row(w['rwkv_r_k'][l]),
        rwkv_ln_g=row(w['rwkv_ln_g'][l]),
        rwkv_ln_b=row(w['rwkv_ln_b'][l]),
        hgrn_norm_g=row(w['hgrn_norm_g'][l]),
        g_pre_ffn=row(w['g_pre_ffn'][l]),
        ffn_conv_w=w['ffn_conv_w'][l].astype(F32),
        g_post_ffn=row(w['g_post_ffn'][l]),
    )
    s5_args = (w['s5_a_re'][l], w['s5_a_im'][l], w['s5_log_dt'][l], w['s5_b_re'][l], w['s5_b_im'][l],
               w['s5_c_re'][l], w['s5_c_im'][l])
    p['s5'] = {c: _s5_weights(*s5_args, c) for c in s5_chunks}
    return p


def _layer(x, st, acc, p, big, lb, layer, *, n_seq, t, time_major):
    s5_re, s5_im, gdn_s, gdn_buf, rwkv_s, rwkv_prev, hgrn_s, ffn_buf = st
    acc_gdn, acc_rwkv, acc_hgrn = acc
    rwkv_prev = _rwkv_cols_to_padded(rwkv_prev)
    if time_major:
        lo, hi = PRE_ROWS, PRE_ROWS + t
        pad_rows = lambda a, before: jnp.pad(a, ((0, 0), (before, SEQ_ROWS - before - a.shape[1]), (0, 0)))
        to_seq = lambda a: pad_rows(jnp.transpose(a.reshape(t, n_seq, a.shape[-1]), (1, 0, 2)), lo)
        from_seq = lambda a: jnp.transpose(a, (1, 0, 2)).reshape(t * n_seq, a.shape[-1])
        tokens = lambda a: a[:, lo:hi]
        cbuf = pad_rows(gdn_buf, 0)
        prev = pad_rows(rwkv_prev[:, None, :], lo - 1)
    else:
        to_seq = lambda a: a.reshape(n_seq, t, a.shape[-1])
        from_seq = lambda a: a.reshape(n_seq * t, a.shape[-1])
        tokens = lambda a: a
        cbuf = gdn_buf
        prev = rwkv_prev[:, None, :]

    proj = _norm_matmul(x, p['g_pre_mix'], big['w_in'], layer, tn=1024)
    proj3 = to_seq(proj)

    s5c = min(S5_CHUNK, t)
    o_a, s5_re_n, s5_im_n = _s5_branch(tokens(proj3)[..., :BRANCH_W], s5_re, s5_im, p['s5'][s5c],
                                       p['s5_d'], p['s5_w_glu'], c=s5c)
    o_b, gdn_s_n, cout = _gdn(proj3, p['gdn_conv_w'], cbuf, p['gdn_alog'], p['gdn_dtb'],
                              p['gdn_norm_g'], gdn_s, acc_gdn, layer, stacked=time_major, tv=t)
    o_c, rwkv_s_n, pout = _rwkv(proj3, prev, p['rwkv_mu'], p['rwkv_w0'], p['rwkv_w2'], p['rwkv_a0'],
                                p['rwkv_a2'], p['rwkv_g2'], p['rwkv_k_k'], p['rwkv_k_a'],
                                p['rwkv_r_k'], p['rwkv_ln_g'], p['rwkv_ln_b'], rwkv_s, acc_rwkv, layer,
                                stacked=time_major, tv=t)
    o_d, hgrn_s_n = _hgrn(proj3, lb, p['hgrn_norm_g'], hgrn_s, acc_hgrn, layer,
                          stacked=time_major, tv=t)
    if time_major:
        gdn_buf_n = cout[:, hi - (GDN_CONV - 1):hi]
        rwkv_prev_n = _rwkv_cols_from_padded(pout[:, hi - 1])
    else:
        gdn_buf_n = cout
        rwkv_prev_n = _rwkv_cols_from_padded(pout[:, 0])

    branches = [from_seq(o_a)] + [from_seq(tokens(o)) for o in (o_b, o_c, o_d)]
    mix = _mix(x, p['g_pre_mix'], branches, big['w_gate'], big['w_br'], layer)
    x = _matmul_resnorm(mix, big['w_o'], layer, x, p['g_post_mix'])

    if time_major:
        buf = jnp.transpose(ffn_buf, (1, 0, 2)).reshape(1, (FFN_CONV - 1) * n_seq, 2 * D_FF)
        x, nbuf = _ffn(x, p['g_pre_ffn'], big['w_up'], p['ffn_conv_w'], buf, big['w_down'],
                       p['g_post_ffn'], layer, stride=n_seq, rows_per_seq=t * n_seq)
        ffn_buf_n = jnp.transpose(nbuf.reshape(FFN_CONV - 1, n_seq, 2 * D_FF), (1, 0, 2))
    else:
        x, ffn_buf_n = _ffn(x, p['g_pre_ffn'], big['w_up'], p['ffn_conv_w'], ffn_buf, big['w_down'],
                            p['g_post_ffn'], layer, stride=1, rows_per_seq=t)
    return x, (s5_re_n, s5_im_n, gdn_s_n, gdn_buf_n, rwkv_s_n, rwkv_prev_n, hgrn_s_n, ffn_buf_n)


MATRIX_STATES = (2, 4, 6)


def _run_group(x3, states, layers, big, lb_all, *, time_major):
    n_seq, t, d = x3.shape
    if time_major:
        x = jnp.transpose(x3, (1, 0, 2)).reshape(t * n_seq, d)
    else:
        x = x3.reshape(n_seq * t, d)
    new = [[] for _ in states]
    acc = [jnp.zeros(states[i].shape, F32) for i in MATRIX_STATES]
    for l in range(DEPTH):
        st = [s if i in MATRIX_STATES else s[l] for i, s in enumerate(states)]
        x, st = _layer(x, st, acc, layers[l], big, lb_all[l:l + 1], l,
                       n_seq=n_seq, t=t, time_major=time_major)
        acc = [st[i] for i in MATRIX_STATES]
        for lst, s in zip(new, st):
            lst.append(s)
    if time_major:
        y = jnp.transpose(x.reshape(t, n_seq, d), (1, 0, 2))
    else:
        y = x.reshape(n_seq, t, d)
    return y, [acc[MATRIX_STATES.index(i)] if i in MATRIX_STATES else jnp.stack(lst)
               for i, lst in enumerate(new)]


def kernel(x_prompt, x_sample, state_s5_re, state_s5_im, state_gdn, state_gdn_conv, state_rwkv, state_rwkv_shift, state_hgrn, state_ffn_conv, g_pre_mix, w_in, w_gate, w_br, w_o, g_post_mix, s5_a_re, s5_a_im, s5_log_dt, s5_b_re, s5_b_im, s5_c_re, s5_c_im, s5_d, s5_w_glu, gdn_conv_w, gdn_a_log, gdn_dt_bias, gdn_norm_g, rwkv_mu, rwkv_w0, rwkv_w2, rwkv_a0, rwkv_a2, rwkv_g2, rwkv_k_k, rwkv_k_a, rwkv_r_k, rwkv_ln_g, rwkv_ln_b, hgrn_lb_logits, hgrn_norm_g, g_pre_ffn, w_up, ffn_conv_w, w_down, g_post_ffn):
    w = dict(g_pre_mix=g_pre_mix, w_in=w_in, w_gate=w_gate, w_br=w_br, w_o=w_o, g_post_mix=g_post_mix,
             s5_a_re=s5_a_re, s5_a_im=s5_a_im, s5_log_dt=s5_log_dt, s5_b_re=s5_b_re, s5_b_im=s5_b_im,
             s5_c_re=s5_c_re, s5_c_im=s5_c_im, s5_d=s5_d, s5_w_glu=s5_w_glu, gdn_conv_w=gdn_conv_w,
             gdn_a_log=gdn_a_log, gdn_dt_bias=gdn_dt_bias, gdn_norm_g=gdn_norm_g, rwkv_mu=rwkv_mu,
             rwkv_w0=rwkv_w0, rwkv_w2=rwkv_w2, rwkv_a0=rwkv_a0, rwkv_a2=rwkv_a2, rwkv_g2=rwkv_g2,
             rwkv_k_k=rwkv_k_k, rwkv_k_a=rwkv_k_a, rwkv_r_k=rwkv_r_k, rwkv_ln_g=rwkv_ln_g,
             rwkv_ln_b=rwkv_ln_b, hgrn_norm_g=hgrn_norm_g, g_pre_ffn=g_pre_ffn, w_up=w_up,
             ffn_conv_w=ffn_conv_w, w_down=w_down, g_post_ffn=g_post_ffn)
    sm = jax.nn.softmax(hgrn_lb_logits.astype(F32), axis=0)
    lb_all = jnp.maximum(jnp.cumsum(sm, axis=0) - sm[0], 0.0)
    s5_chunks = {min(S5_CHUNK, x_prompt.shape[1]), min(S5_CHUNK, x_sample.shape[1])}
    layers = [_prep_layer(l, w, s5_chunks) for l in range(DEPTH)]
    big = _prep_big(w)
    sample_states = (state_s5_re, state_s5_im, state_gdn, state_gdn_conv, state_rwkv,
                     state_rwkv_shift, state_hgrn, state_ffn_conv)
    nb = x_prompt.shape[0]
    prompt_states = tuple(jnp.zeros((DEPTH, nb) + s.shape[2:], F32) for s in sample_states)
    y_prompt, ps = _run_group(x_prompt, prompt_states, layers, big, lb_all, time_major=False)
    y_sample, ss = _run_group(x_sample, sample_states, layers, big, lb_all, time_major=True)
    out = [y_prompt, y_sample]
    for a, b in zip(ps, ss):
        out.extend((a, b))
    return tuple(out)
```

```python
import functools
import math

import jax
import jax.numpy as jnp
from jax import lax
from jax.experimental import pallas as pl
from jax.experimental.pallas import tpu as pltpu

F32 = jnp.float32
BF16 = jnp.bfloat16

D_MODEL = 2048
DEPTH = 4
N_BRANCH = 4
BRANCH_W = 512
NORM_EPS = 1e-6

S5_GROUP = 16
S5_GROUPS = 32
S5_STATE = 64

GDN_HEADS = 4
GDN_DH = 128
GDN_CONV = 4

RWKV_DH = 64
RWKV_HEADS = 8
RWKV_W_LORA = 96
RWKV_A_LORA = 96
RWKV_G_LORA = 256
RWKV_LN_EPS = 64e-5
RWKV_COLS = 3 * BRANCH_W + RWKV_W_LORA + RWKV_A_LORA + RWKV_G_LORA

HGRN_HEADS = 4
HGRN_DH = 128
HGRN_SUB = 16
LB_TINY = 1e-30
EXP_CLAMP = 80.0

D_FF = 5632
FFN_CONV = 3

LANES = 128
SUBLANES = 8
VMEM_LIMIT_BYTES = 56 * 1024 * 1024

PROJ_COLS = 7168
COL_S5U, COL_GQ, COL_GK, COL_GV, COL_GZ = 0, 1, 2, 3, 4
COL_RR, COL_RK, COL_RV = 5, 6, 7
COL_HQ, COL_HF, COL_HI, COL_HG = 8, 9, 10, 11
COL_RGL = 6144 // 256
COL_GBA, COL_RWL, COL_RAL = 6400 // 128, 6528 // 128, 6656 // 128
RWKV_PAD_COLS = 2048

ROW_TILE = 512
PROMPT_CHUNK = 128
SEQ_ROWS = SUBLANES
PRE_ROWS = GDN_CONV - 1
SEQ_BLOCK = 8
S5_CHUNK = 16
INV_BLOCK = 64


def _s5_perm(a, axis):
    axis %= a.ndim
    shp = a.shape
    a = a.reshape(shp[:axis] + (S5_GROUPS, S5_GROUP) + shp[axis + 1:])
    return jnp.swapaxes(a, axis, axis + 1).reshape(shp)


def _cparams(n_axes):
    return pltpu.CompilerParams(dimension_semantics=("arbitrary",) * n_axes,
                                vmem_limit_bytes=VMEM_LIMIT_BYTES)


def _mm(a, b):
    return jnp.dot(a.astype(BF16), b.astype(BF16), preferred_element_type=F32)


def _mm_nt(a, b):
    return lax.dot_general(a.astype(BF16), b.astype(BF16), (((1,), (1,)), ((), ())),
                           preferred_element_type=F32)


def _mm_tn(a, b):
    return lax.dot_general(a.astype(BF16), b.astype(BF16), (((0,), (0,)), ((), ())),
                           preferred_element_type=F32)


def _split(a):
    hi = a.astype(BF16)
    lo = (a - hi.astype(F32)).astype(BF16)
    return hi, lo


def _mm3(a, b):
    ah, al = _split(a)
    bh, bl = _split(b)
    d = functools.partial(jnp.dot, preferred_element_type=F32)
    return d(ah, bh) + (d(ah, bl) + d(al, bh))


def _rms(x, eps):
    return x * lax.rsqrt(jnp.mean(x * x, axis=-1, keepdims=True) + eps)


def _sigmoid(x):
    return 1.0 / (1.0 + jnp.exp(-x))


def _silu(x):
    return x * _sigmoid(x)


def _softplus(x):
    return jnp.maximum(x, 0.0) + jnp.log1p(jnp.exp(-jnp.abs(x)))


def _gelu_tanh(x):
    return 0.5 * x * (1.0 + jnp.tanh(math.sqrt(2.0 / math.pi) * (x + 0.044715 * (x * x * x))))


def _seg_masks(c, seg, reps=1):
    r = lax.broadcasted_iota(jnp.int32, (c, reps * c), 0)
    s = lax.broadcasted_iota(jnp.int32, (c, reps * c), 1)
    if reps > 1:
        s = s % c
    causal, strict = r >= s, r > s
    if seg < c:
        same = (r // seg) == (s // seg)
        causal, strict = jnp.logical_and(causal, same), jnp.logical_and(strict, same)
    return causal, strict


def _unit_lower_inverses(mats, seg):
    c = mats[0].shape[0]
    r = lax.broadcasted_iota(jnp.int32, (c, c), 0)
    s = lax.broadcasted_iota(jnp.int32, (c, c), 1)
    eye = (r == s).astype(F32)
    if seg > INV_BLOCK:
        assert seg == 2 * INV_BLOCK
        diag = (r // INV_BLOCK) == (s // INV_BLOCK)
        tb = _unit_lower_inverses([jnp.where(diag, a, 0.0) for a in mats], INV_BLOCK)
        off = [jnp.where(diag, 0.0, a) for a in mats]
        left = [_mm(t, o) for t, o in zip(tb, off)]
        return [t - _mm(x, t) for t, x in zip(tb, left)]
    ps = [-a for a in mats]
    ts = [eye + p for p in ps]
    k = 2
    while k < seg:
        ps = [_mm(p, p) for p in ps]
        ts = [t + _mm(t, p) for t, p in zip(ts, ps)]
        k *= 2
    return ts


def _unit_lower_inverse(a, seg):
    return _unit_lower_inverses([a], seg)[0]


def _rows(ref, c):
    return ref[0] if ref.shape[0] == 1 else ref[...].reshape(c, ref.shape[-1])


def _token_rows(c, width):
    return lax.broadcasted_iota(jnp.int32, (c, width), 0) % SEQ_ROWS


def _norm_matmul_kernel(x_ref, g_ref, w_ref, o_ref, h_ref):
    @pl.when(pl.program_id(1) == 0)
    def _():
        h_ref[...] = (_rms(x_ref[...], NORM_EPS) * g_ref[...]).astype(BF16)

    o_ref[...] = jnp.dot(h_ref[...], w_ref[...], preferred_element_type=F32)


def _norm_matmul(x, g, w_all, layer, *, tn):
    rows, d = x.shape
    n = w_all.shape[2]
    tm = min(ROW_TILE, rows)
    return pl.pallas_call(
        _norm_matmul_kernel,
        grid=(rows // tm, n // tn),
        in_specs=[pl.BlockSpec((tm, d), lambda i, j: (i, 0)),
                  pl.BlockSpec((1, d), lambda i, j: (0, 0)),
                  pl.BlockSpec((None, d, tn), lambda i, j: (layer, 0, j))],
        out_specs=pl.BlockSpec((tm, tn), lambda i, j: (i, j)),
        out_shape=jax.ShapeDtypeStruct((rows, n), F32),
        scratch_shapes=[pltpu.VMEM((tm, d), BF16)],
        compiler_params=_cparams(2),
        name="norm_matmul",
    )(x, g, w_all)


def _mix_kernel(x_ref, g_ref, oa_ref, ob_ref, oc_ref, od_ref, wga_ref, wgb_ref, wgc_ref, wgd_ref,
                wbr0_ref, wbr_ref, o_ref, h_ref):
    @pl.when(pl.program_id(1) == 0)
    def _():
        h_ref[...] = (_rms(x_ref[...], NORM_EPS) * g_ref[...]).astype(BF16)

    h = h_ref[...]
    acc = None
    for n, (br_ref, wg_ref) in enumerate(((oa_ref, wga_ref), (ob_ref, wgb_ref), (oc_ref, wgc_ref),
                                          (od_ref, wgd_ref))):
        gate = _sigmoid(jnp.dot(h, wg_ref[...], preferred_element_type=F32))
        w_n = wbr0_ref[...] if n == 0 else wbr_ref[n]
        term = gate * jnp.dot(br_ref[...], w_n, preferred_element_type=F32)
        acc = term if acc is None else acc + term
    o_ref[...] = acc.astype(BF16)


def _mix(x, g, branches, wg, wbr0, wbr, layer, *, tn=256):
    rows, d = x.shape
    tm = min(ROW_TILE, rows)
    bw = branches[0].shape[1]
    br_spec = pl.BlockSpec((tm, bw), lambda i, j: (i, 0))
    nt = d // tn
    wg_specs = [pl.BlockSpec((None, d, tn), functools.partial(lambda i, j, n: (layer, 0, n * nt + j), n=n))
                for n in range(N_BRANCH)]
    return pl.pallas_call(
        _mix_kernel,
        grid=(rows // tm, nt),
        in_specs=[pl.BlockSpec((tm, d), lambda i, j: (i, 0)),
                  pl.BlockSpec((1, d), lambda i, j: (0, 0)),
                  br_spec, br_spec, br_spec, br_spec, *wg_specs,
                  pl.BlockSpec((None, bw, tn), lambda i, j: (layer, 0, j)),
                  pl.BlockSpec((None, N_BRANCH, bw, tn), lambda i, j: (layer, 0, 0, j))],
        out_specs=pl.BlockSpec((tm, tn), lambda i, j: (i, j)),
        out_shape=jax.ShapeDtypeStruct((rows, d), BF16),
        scratch_shapes=[pltpu.VMEM((tm, d), BF16)],
        compiler_params=_cparams(2),
        name="branch_mix",
    )(x, g, *branches, wg, wg, wg, wg, wbr0, wbr)


def _matmul_resnorm_kernel(a_ref, w_ref, x_ref, g_ref, o_ref):
    y = jnp.dot(a_ref[...], w_ref[...], preferred_element_type=F32)
    o_ref[...] = x_ref[...] + _rms(y, NORM_EPS) * g_ref[...]


def _matmul_resnorm(a, w_all, layer, x, g):
    rows, kdim = a.shape
    d = w_all.shape[2]
    tm = min(ROW_TILE, rows)
    return pl.pallas_call(
        _matmul_resnorm_kernel,
        grid=(rows // tm,),
        in_specs=[pl.BlockSpec((tm, kdim), lambda i: (i, 0)),
                  pl.BlockSpec((None, kdim, d), lambda i: (layer, 0, 0)),
                  pl.BlockSpec((tm, d), lambda i: (i, 0)),
                  pl.BlockSpec((1, d), lambda i: (0, 0))],
        out_specs=pl.BlockSpec((tm, d), lambda i: (i, 0)),
        out_shape=jax.ShapeDtypeStruct((rows, d), F32),
        compiler_params=_cparams(1),
        name="matmul_resnorm",
    )(a, w_all, x, g)


def _ffn_kernel(x_ref, gpre_ref, wa_ref, wb_ref, cwa_ref, cwb_ref, bufa_ref, bufb_ref, wd_ref,
                gpost_ref, o_ref, nbufa_ref, nbufb_ref, h_ref, acc_ref, sa_ref, sb_ref, carry_ref,
                *, tm, stride, tiles_per_seq):
    i = pl.program_id(0)
    f = pl.program_id(1)
    keep = (FFN_CONV - 1) * stride
    pad = max(SUBLANES, keep)

    @pl.when(f == 0)
    def _():
        h_ref[...] = (_rms(x_ref[...], NORM_EPS) * gpre_ref[...]).astype(BF16)
        acc_ref[...] = jnp.zeros_like(acc_ref)

    h = h_ref[...]
    halves = []
    for idx, (w_ref, cw_ref, buf_ref, nbuf_ref, s_ref) in enumerate(
            ((wa_ref, cwa_ref, bufa_ref, nbufa_ref, sa_ref),
             (wb_ref, cwb_ref, bufb_ref, nbufb_ref, sb_ref))):
        if tiles_per_seq == 1:
            s_ref[pad - keep:pad, :] = buf_ref[0]
        else:
            first = (i % tiles_per_seq) == 0

            @pl.when(first)
            def _():
                s_ref[pad - keep:pad, :] = buf_ref[0]

            @pl.when(jnp.logical_not(first))
            def _():
                s_ref[pad - keep:pad, :] = carry_ref[f, idx]

        s_ref[pad:pad + tm, :] = jnp.dot(h, w_ref[...], preferred_element_type=F32)
        cw = cw_ref[...]
        y = None
        for j in range(FFN_CONV):
            off = pad - (FFN_CONV - 1 - j) * stride
            term = cw[j:j + 1, :] * s_ref[off:off + tm, :]
            y = term if y is None else y + term
        last = s_ref[pad + tm - keep:pad + tm, :]
        nbuf_ref[0] = last
        if tiles_per_seq > 1:
            carry_ref[f, idx] = last
        halves.append(y)

    act = (_gelu_tanh(halves[0]) * halves[1]).astype(BF16)
    acc_ref[...] += jnp.dot(act, wd_ref[...], preferred_element_type=F32)

    @pl.when(f == pl.num_programs(1) - 1)
    def _():
        o_ref[...] = x_ref[...] + _rms(acc_ref[...], NORM_EPS) * gpost_ref[...]


def _ffn(x, gpre, w_up, conv_w, buf, w_down, gpost, layer, *, stride, rows_per_seq, tf=512):
    rows, d = x.shape
    tm = min(ROW_TILE, rows_per_seq)
    tiles_per_seq = rows_per_seq // tm
    nf = D_FF // tf
    keep = (FFN_CONV - 1) * stride
    pad = max(SUBLANES, keep)
    carry_rows = keep if tiles_per_seq > 1 else SUBLANES
    kern = functools.partial(_ffn_kernel, tm=tm, stride=stride, tiles_per_seq=tiles_per_seq)
    buf_a = pl.BlockSpec((1, keep, tf), lambda i, f: (i // tiles_per_seq, 0, f))
    buf_b = pl.BlockSpec((1, keep, tf), lambda i, f: (i // tiles_per_seq, 0, nf + f))
    y, nbuf_a, nbuf_b = pl.pallas_call(
        kern,
        grid=(rows // tm, nf),
        in_specs=[pl.BlockSpec((tm, d), lambda i, f: (i, 0)),
                  pl.BlockSpec((1, d), lambda i, f: (0, 0)),
                  pl.BlockSpec((None, d, tf), lambda i, f: (layer, 0, f)),
                  pl.BlockSpec((None, d, tf), lambda i, f: (layer, 0, nf + f)),
                  pl.BlockSpec((FFN_CONV, tf), lambda i, f: (0, f)),
                  pl.BlockSpec((FFN_CONV, tf), lambda i, f: (0, nf + f)),
                  buf_a, buf_b,
                  pl.BlockSpec((None, tf, d), lambda i, f: (layer, f, 0)),
                  pl.BlockSpec((1, d), lambda i, f: (0, 0))],
        out_specs=[pl.BlockSpec((tm, d), lambda i, f: (i, 0)),
                   pl.BlockSpec((1, keep, tf), lambda i, f: (i, 0, f)),
                   pl.BlockSpec((1, keep, tf), lambda i, f: (i, 0, f))],
        out_shape=[jax.ShapeDtypeStruct((rows, d), F32),
                   jax.ShapeDtypeStruct((rows // tm, keep, D_FF), F32),
                   jax.ShapeDtypeStruct((rows // tm, keep, D_FF), F32)],
        scratch_shapes=[pltpu.VMEM((tm, d), BF16),
                        pltpu.VMEM((tm, d), F32),
                        pltpu.VMEM((pad + tm, tf), F32),
                        pltpu.VMEM((pad + tm, tf), F32),
                        pltpu.VMEM((nf, 2, carry_rows, tf), F32)],
        compiler_params=_cparams(2),
        name="conv_ffn",
    )(x, gpre, w_up, w_up, conv_w, conv_w, buf, buf, w_down, gpost)
    nbuf = jnp.concatenate([nbuf_a, nbuf_b], axis=-1)
    return y, nbuf[tiles_per_seq - 1::tiles_per_seq]


def _s5_kernel(u_ref, h0_ref, bst_ref, kst_ref, cst_ref, apow_ref, y_ref, hfin_ref, e_ref,
               *, n_seq, n_chunks):
    rows = n_seq * n_chunks
    u = u_ref[0]
    h0 = h0_ref[0]
    half = S5_STATE
    if n_chunks > 1:
        row = lax.broadcasted_iota(jnp.int32, (rows, 2 * half), 0)
        placed = jnp.zeros((rows, 2 * half), F32)
        for b in range(n_seq):
            placed = jnp.where(row == b * n_chunks, h0[b:b + 1, :], placed)
        h0 = placed

    def cmul(k, x):
        return x * apow_ref[0, 2 * k:2 * k + 1, :] + pltpu.roll(x, half, 1) * apow_ref[0, 2 * k + 1:2 * k + 2, :]

    e = _mm3(u, bst_ref[0]) + cmul(0, h0)
    if n_chunks > 1:
        j = lax.broadcasted_iota(jnp.int32, (rows, 2 * half), 0) % n_chunks
        k, sh = 0, 1
        while sh < n_chunks:
            e = e + jnp.where(j >= sh, cmul(k, pltpu.roll(e, sh, 0)), 0.0)
            k, sh = k + 1, sh * 2
        e_ref[...] = e
        hfin_ref[0] = e_ref[pl.ds(n_chunks - 1, n_seq, stride=n_chunks), :]
        h_start = jnp.where(j >= 1, pltpu.roll(e, 1, 0), 0.0) + h0
    else:
        hfin_ref[0] = e
        h_start = h0
    y_ref[0] = _mm3(u, kst_ref[0]) + _mm3(h_start, cst_ref[0])


def _s5_scan(u_g, h0_g, bst, kst, cst, apow, *, n_seq, n_chunks):
    groups, rows, cw = u_g.shape
    kern = functools.partial(_s5_kernel, n_seq=n_seq, n_chunks=n_chunks)
    spec3 = lambda a: pl.BlockSpec((1,) + a.shape[1:], lambda g: (g, 0, 0))
    return pl.pallas_call(
        kern,
        grid=(groups,),
        in_specs=[spec3(u_g), spec3(h0_g), spec3(bst), spec3(kst), spec3(cst), spec3(apow)],
        out_specs=[pl.BlockSpec((1, rows, cw), lambda g: (g, 0, 0)),
                   pl.BlockSpec((1, n_seq, 2 * S5_STATE), lambda g: (g, 0, 0))],
        out_shape=[jax.ShapeDtypeStruct((groups, rows, cw), F32),
                   jax.ShapeDtypeStruct((groups, n_seq, 2 * S5_STATE), F32)],
        scratch_shapes=[pltpu.VMEM((rows, 2 * S5_STATE), F32)],
        compiler_params=_cparams(1),
        name="s5_scan",
    )(u_g, h0_g, bst, kst, cst, apow)


def _s5_glu_kernel(y_ref, u_ref, d_ref, w_ref, o_ref):
    g = _gelu_tanh(y_ref[...] + d_ref[...] * u_ref[...])
    o_ref[...] = (g * _sigmoid(_mm(g, w_ref[...]))).astype(BF16)


def _s5_glu(y, u, d, w_glu):
    rows, bw = y.shape
    tm = min(ROW_TILE, rows)
    row_spec = pl.BlockSpec((tm, bw), lambda i: (i, 0))
    return pl.pallas_call(
        _s5_glu_kernel,
        grid=(rows // tm,),
        in_specs=[row_spec, row_spec,
                  pl.BlockSpec((1, bw), lambda i: (0, 0)),
                  pl.BlockSpec((bw, bw), lambda i: (0, 0))],
        out_specs=row_spec,
        out_shape=jax.ShapeDtypeStruct((rows, bw), BF16),
        compiler_params=_cparams(1),
        name="s5_glu",
    )(y, u, d, w_glu)


def _s5_weights(a_re, a_im, log_dt, b_re, b_im, c_re, c_im, c):
    lam = lax.complex(a_re.astype(F32), a_im.astype(F32))
    ldt = lam * jnp.exp(log_dt.astype(F32))[:, None]
    a_bar = jnp.exp(ldt)
    b_bar = ((a_bar - 1.0) / lam)[..., None] * lax.complex(b_re.astype(F32), b_im.astype(F32))
    cm = lax.complex(c_re.astype(F32), c_im.astype(F32))
    tau = jnp.arange(c + 1, dtype=F32)
    apw = jnp.exp(ldt[None] * tau[:, None, None])
    bst = apw[:c][::-1][:, :, :, None] * b_bar[None]
    bst = jnp.transpose(bst, (1, 0, 3, 2)).reshape(S5_GROUPS, c * S5_GROUP, S5_STATE)
    bst = jnp.concatenate([jnp.real(bst), jnp.imag(bst)], axis=-1)
    kt = jnp.real(jnp.einsum('ghp,tgp,gpk->tgkh', cm, apw[:c], b_bar))
    s_idx = jnp.arange(c)[:, None]
    t_idx = jnp.arange(c)[None, :]
    kst = jnp.where((t_idx >= s_idx)[:, :, None, None, None],
                    kt[jnp.clip(t_idx - s_idx, 0, c - 1)], 0.0)
    kst = jnp.transpose(kst, (2, 0, 3, 1, 4)).reshape(S5_GROUPS, c * S5_GROUP, c * S5_GROUP)
    ca = cm[None] * apw[1:c + 1][:, :, None, :]
    ca = jnp.transpose(ca, (1, 3, 0, 2)).reshape(S5_GROUPS, S5_STATE, c * S5_GROUP)
    cst = jnp.concatenate([jnp.real(ca), -jnp.imag(ca)], axis=1)
    rows = []
    k = 0
    while True:
        p = jnp.exp(ldt * float(c * 2 ** k))
        rows.append(jnp.concatenate([jnp.real(p), jnp.real(p)], axis=-1))
        rows.append(jnp.concatenate([-jnp.imag(p), jnp.imag(p)], axis=-1))
        k += 1
        if c * 2 ** k > 4096:
            break
    apow = jnp.stack(rows, axis=1)
    return bst, kst, cst, apow


def _s5_branch(u3, h0_re, h0_im, sw, d_skip, w_glu, *, c):
    n_seq, t, _ = u3.shape
    n_chunks = t // c
    bst, kst, cst, apow = sw
    u2 = u3.reshape(n_seq * t, BRANCH_W)
    u_g = u2.reshape(n_seq * t * S5_GROUP, S5_GROUPS).T
    u_g = u_g.reshape(S5_GROUPS, n_seq * n_chunks, c * S5_GROUP)
    h0 = jnp.concatenate([h0_re, h0_im], axis=-1)
    h0 = jnp.transpose(h0, (1, 0, 2))
    y_g, hfin = _s5_scan(u_g, h0, bst, kst, cst, apow, n_seq=n_seq, n_chunks=n_chunks)
    y = y_g.reshape(S5_GROUPS, n_seq * t * S5_GROUP).T.reshape(n_seq * t, BRANCH_W)
    o = _s5_glu(y, u2, d_skip, w_glu)
    hfin = jnp.transpose(hfin, (1, 0, 2))
    return o.reshape(n_seq, t, BRANCH_W), hfin[..., :S5_STATE], hfin[..., S5_STATE:]


def _mixer_grid(n_seq, t, stacked):
    if stacked:
        assert t == SEQ_ROWS and n_seq % SEQ_BLOCK == 0
        c = SEQ_BLOCK * SEQ_ROWS
        shape = lambda w: (SEQ_BLOCK, SEQ_ROWS, w)
        imap = lambda blk: (lambda b: (b, 0, blk))
        smap = lambda nd: (lambda b: (b,) + (0,) * (nd - 1))
        return (n_seq // SEQ_BLOCK,), c, SEQ_ROWS, shape, imap, smap, SEQ_BLOCK
    assert t % PROMPT_CHUNK == 0
    c = PROMPT_CHUNK
    shape = lambda w: (1, c, w)
    imap = lambda blk: (lambda b, i: (b, i, blk))
    smap = lambda nd: (lambda b, i: (b,) + (0,) * (nd - 1))
    return (n_seq, t // c), c, c, shape, imap, smap, 1


def _layer_state_spec(s_all, layer, nblk, n_grid):
    tail = (0,) * (s_all.ndim - 2)
    imap = (lambda b: (layer, b) + tail) if n_grid == 1 else (lambda b, i: (layer, b) + tail)
    return pl.BlockSpec((None, nblk) + s_all.shape[2:], imap)


def _const_spec(a, n_grid):
    zeros = (0,) * a.ndim
    return pl.BlockSpec(a.shape, (lambda b: zeros) if n_grid == 1 else (lambda b, i: zeros))


def _gdn_kernel(pq_ref, pk_ref, pv_ref, pz_ref, pba_ref, cw_ref, cbuf_ref, alog_ref, dtb_ref,
                ng_ref, s0_ref, acc_ref, o_ref, sout_ref, cout_ref, *scratch, c, seg, tv):
    stacked = seg < c
    keep = GDN_CONV - 1
    x = jnp.concatenate([_rows(pq_ref, c), _rows(pk_ref, c), _rows(pv_ref, c)], axis=1)
    if stacked:
        (conv_ref,) = scratch
        t_in = _token_rows(c, 3 * BRANCH_W)
        x = jnp.where(t_in < keep, _rows(cbuf_ref, c), x)
        conv_ref[0:SUBLANES, :] = jnp.zeros((SUBLANES, 3 * BRANCH_W), F32)
        cout_ref[...] = x.reshape(cout_ref.shape)
    else:
        s_ref, conv_ref = scratch

        @pl.when(pl.program_id(1) == 0)
        def _():
            s_ref[...] = s0_ref[0]
            conv_ref[SUBLANES - keep:SUBLANES, :] = cbuf_ref[0]

    conv_ref[SUBLANES:SUBLANES + c, :] = x
    cw = cw_ref[...]
    y = None
    for j in range(GDN_CONV):
        off = SUBLANES - keep + j
        term = cw[j:j + 1, :] * conv_ref[off:off + c, :]
        y = term if y is None else y + term
    if not stacked:
        last = conv_ref[SUBLANES + c - keep:SUBLANES + c, :]
        conv_ref[SUBLANES - keep:SUBLANES, :] = last
        cout_ref[0] = last
    act = _silu(y)

    ba = _rows(pba_ref, c)
    beta_all = _sigmoid(ba)
    g_all = -jnp.exp(alog_ref[...]) * _softplus(ba + dtb_ref[...])
    if stacked:
        t_in = _token_rows(c, LANES)
        valid = jnp.logical_and(t_in >= PRE_ROWS, t_in < PRE_ROWS + tv)
        beta_all = jnp.where(valid, beta_all, 0.0)
        g_all = jnp.where(valid, g_all, 0.0)
    causal, strict = _seg_masks(c, seg)
    d_all = _mm3(causal.astype(F32), g_all)
    dt_all = d_all.T
    ng = ng_ref[...]
    z = _rows(pz_ref, c)
    n_seg = c // seg

    heads = range(GDN_HEADS)
    sls = [slice(h * GDN_DH, (h + 1) * GDN_DH) for h in heads]
    segs = [slice(j * seg, (j + 1) * seg) for j in range(n_seg)]
    cat0 = lambda xs: xs[0] if len(xs) == 1 else jnp.concatenate(xs, axis=0)
    state = lambda j, h: s0_ref[j, h] if stacked else s_ref[h]
    q = [act[:, sl] for sl in sls]
    k = [act[:, BRANCH_W + h * GDN_DH:BRANCH_W + (h + 1) * GDN_DH] for h in heads]
    v = [act[:, 2 * BRANCH_W + h * GDN_DH:2 * BRANCH_W + (h + 1) * GDN_DH] for h in heads]
    q = [x * lax.rsqrt(jnp.sum(x * x, axis=-1, keepdims=True) + 1e-6) * (GDN_DH ** -0.5) for x in q]
    k = [x * lax.rsqrt(jnp.sum(x * x, axis=-1, keepdims=True) + 1e-6) for x in k]
    beta = [beta_all[:, h:h + 1] for h in heads]
    d = [d_all[:, GDN_HEADS + h:GDN_HEADS + h + 1] for h in heads]
    d_row = [dt_all[GDN_HEADS + h:GDN_HEADS + h + 1, :] for h in heads]
    decay = [jnp.where(causal, jnp.exp(jnp.where(causal, d[h] - d_row[h], 0.0)), 0.0) for h in heads]
    kb = [k[h] * beta[h] for h in heads]
    prod = [_mm_nt(jnp.concatenate([kb[h], q[h]], axis=0), k[h]) for h in heads]
    m = [jnp.where(strict, prod[h][:c] * decay[h], 0.0) for h in heads]
    attn = [prod[h][c:] * decay[h] for h in heads]
    t_inv = _unit_lower_inverses(m, seg)
    ed = [jnp.exp(x) for x in d]
    sol = [_mm(t_inv[h], jnp.concatenate([v[h] * beta[h], kb[h] * ed[h]], axis=1)) for h in heads]
    qe = [q[h] * ed[h] for h in heads]
    both = [[_mm(jnp.concatenate([sol[h][rows, GDN_DH:], qe[h][rows]], axis=0), state(j, h))
             for j, rows in enumerate(segs)] for h in heads]
    u = [cat0([sol[h][rows, :GDN_DH] - both[h][j][:seg] for j, rows in enumerate(segs)]) for h in heads]
    qs = [cat0([b[seg:] for b in both[h]]) for h in heads]
    o = [qs[h] + _mm(attn[h], u[h]) for h in heads]
    for h in heads:
        for j, rows in enumerate(segs):
            dl = d[h][(j + 1) * seg - 1:(j + 1) * seg, :]
            s_new = state(j, h) * jnp.exp(dl) + _mm_tn(k[h][rows] * jnp.exp(dl - d[h][rows]), u[h][rows])
            if stacked:
                sout_ref[j, h] = s_new
            else:
                s_ref[h] = s_new
    outs = [_rms(o[h], NORM_EPS) * ng * _silu(z[:, sls[h]]) for h in heads]
    o_ref[...] = jnp.concatenate(outs, axis=1).reshape(o_ref.shape).astype(BF16)

    if not stacked:
        @pl.when(pl.program_id(1) == pl.num_programs(1) - 1)
        def _():
            sout_ref[0] = s_ref[...]


def _gdn(proj3, conv_w, cbuf, alog, dtb, ng, s_all, acc, layer, *, stacked, tv):
    n_seq, t, _ = proj3.shape
    grid, c, seg, shape, imap, smap, nblk = _mixer_grid(n_seq, t, stacked)
    ng_ = len(grid)
    kern = functools.partial(_gdn_kernel, c=c, seg=seg, tv=tv)
    col = lambda blk, w: pl.BlockSpec(shape(w), imap(blk))
    cbuf_spec = pl.BlockSpec((nblk,) + cbuf.shape[1:], smap(3))
    scratch = [pltpu.VMEM((SUBLANES + c, 3 * BRANCH_W), F32)]
    if not stacked:
        scratch = [pltpu.VMEM((GDN_HEADS, GDN_DH, GDN_DH), F32)] + scratch
    return pl.pallas_call(
        kern,
        grid=grid,
        in_specs=[col(COL_GQ, BRANCH_W), col(COL_GK, BRANCH_W), col(COL_GV, BRANCH_W),
                  col(COL_GZ, BRANCH_W), col(COL_GBA, LANES),
                  _const_spec(conv_w, ng_), cbuf_spec, _const_spec(alog, ng_), _const_spec(dtb, ng_),
                  _const_spec(ng, ng_), _layer_state_spec(s_all, layer, nblk, ng_),
                  pl.BlockSpec(memory_space=pl.ANY)],
        out_specs=[col(0, BRANCH_W), _layer_state_spec(acc, layer, nblk, ng_), cbuf_spec],
        out_shape=[jax.ShapeDtypeStruct((n_seq, t, BRANCH_W), BF16),
                   jax.ShapeDtypeStruct(acc.shape, F32),
                   jax.ShapeDtypeStruct(cbuf.shape, F32)],
        input_output_aliases={11: 1},
        scratch_shapes=scratch,
        compiler_params=_cparams(ng_),
        name="gdn",
    )(proj3, proj3, proj3, proj3, proj3, conv_w, cbuf, alog, dtb, ng, s_all, acc)


def _rwkv_kernel(pr_ref, pk_ref, pv_ref, pwl_ref, pal_ref, pgl_ref, prev_ref, mu_ref, w0_ref, w2_ref,
                 a0_ref, a2_ref, g2_ref, kk_ref, ka_ref, rk_ref, lng_ref, lnb_ref, s0_ref, acc_ref,
                 o_ref, sout_ref, pout_ref, *scratch, c, seg, tv):
    stacked = seg < c
    p = jnp.concatenate([_rows(pr_ref, c), _rows(pk_ref, c), _rows(pv_ref, c), _rows(pwl_ref, c),
                         _rows(pal_ref, c), _rows(pgl_ref, c)], axis=1)
    if stacked:
        (p_ref,) = scratch
        t_in = _token_rows(c, RWKV_PAD_COLS)
        p = jnp.where(t_in == PRE_ROWS - 1, _rows(prev_ref, c), p)
        p_ref[0:SUBLANES, :] = jnp.zeros((SUBLANES, RWKV_PAD_COLS), F32)
        pout_ref[...] = p.reshape(pout_ref.shape)
    else:
        s_ref, p_ref = scratch

        @pl.when(pl.program_id(1) == 0)
        def _():
            s_ref[...] = s0_ref[0]
            p_ref[SUBLANES - 1:SUBLANES, :] = prev_ref[0]

    p_ref[SUBLANES:SUBLANES + c, :] = p
    prev = p_ref[SUBLANES - 1:SUBLANES - 1 + c, :]
    if not stacked:
        last = p_ref[SUBLANES + c - 1:SUBLANES + c, :]
        p_ref[SUBLANES - 1:SUBLANES, :] = last
        pout_ref[0] = last
    pm = p + (prev - p) * mu_ref[...]
    r = pm[:, :BRANCH_W]
    k = pm[:, BRANCH_W:2 * BRANCH_W]
    v = pm[:, 2 * BRANCH_W:3 * BRANCH_W]
    wl = pm[:, 3 * BRANCH_W:3 * BRANCH_W + LANES]
    al = pm[:, 3 * BRANCH_W + LANES:3 * BRANCH_W + 2 * LANES]
    gl = pm[:, 3 * BRANCH_W + 2 * LANES:]
    w_log = -_softplus(-(w0_ref[...] + _mm(jnp.tanh(wl), w2_ref[...]))) - 0.5
    lw = -jnp.exp(w_log)
    a = _sigmoid(a0_ref[...] + _mm(al, a2_ref[...]))
    gate = _mm(_sigmoid(gl), g2_ref[...])
    kk_un = k * kk_ref[...]
    k2 = k * (1.0 + (a - 1.0) * ka_ref[...])
    if stacked:
        t_in = _token_rows(c, BRANCH_W)
        valid = jnp.logical_and(t_in >= PRE_ROWS, t_in < PRE_ROWS + tv)
        lw = jnp.where(valid, lw, 0.0)
        kk_un = jnp.where(valid, kk_un, 0.0)
        k2 = jnp.where(valid, k2, 0.0)
    causal, _ = _seg_masks(c, seg)
    causal2, strict2 = _seg_masks(c, seg, reps=2)
    g_cum = _mm3(causal.astype(F32), lw)
    e_pos = jnp.exp(g_cum)
    e_neg = jnp.exp(-g_cum)
    e_prev = jnp.exp(g_cum - lw)
    rk_w = rk_ref[...]
    lng = lng_ref[...]
    lnb = lnb_ref[...]
    n_seg = c // seg

    heads = range(RWKV_HEADS)
    sls = [slice(h * RWKV_DH, (h + 1) * RWKV_DH) for h in heads]
    segs = [slice(j * seg, (j + 1) * seg) for j in range(n_seg)]
    cat0 = lambda xs: xs[0] if len(xs) == 1 else jnp.concatenate(xs, axis=0)
    state = lambda j, h: s0_ref[j, h] if stacked else s_ref[h]
    kkh = [kk_un[:, sl] for sl in sls]
    kkh = [x * lax.rsqrt(jnp.sum(x * x, axis=-1, keepdims=True) + 1e-6) for x in kkh]
    alpha_hat = [kkh[h] * a[:, sls[h]] * e_neg[:, sls[h]] for h in heads]
    k_hat = [k2[:, sl] * e_neg[:, sl] for sl in sls]
    kap = [kkh[h] * e_prev[:, sls[h]] for h in heads]
    r_t = [r[:, sl] * e_pos[:, sl] for sl in sls]
    vh = [v[:, sl] for sl in sls]
    rhs = [jnp.concatenate([alpha_hat[h], k_hat[h]], axis=0) for h in heads]
    ab = [jnp.where(strict2, _mm_nt(kap[h], rhs[h]), 0.0) for h in heads]
    rr = [jnp.where(causal2, _mm_nt(r_t[h], rhs[h]), 0.0) for h in heads]
    both = [[_mm_nt(jnp.concatenate([kap[h][rows], r_t[h][rows]], axis=0), state(j, h))
             for j, rows in enumerate(segs)] for h in heads]
    bv = [_mm(jnp.concatenate([ab[h][:, c:], rr[h][:, c:]], axis=0), vh[h]) for h in heads]
    t_inv = _unit_lower_inverses([ab[h][:, :c] for h in heads], seg)
    ks = [cat0([b[:seg] for b in both[h]]) for h in heads]
    rs = [cat0([b[seg:] for b in both[h]]) for h in heads]
    u = [_mm(t_inv[h], ks[h] + bv[h][:c]) for h in heads]
    y = [rs[h] + bv[h][c:] - _mm(rr[h][:, :c], u[h]) for h in heads]
    for h in heads:
        for j, rows in enumerate(segs):
            el = e_pos[(j + 1) * seg - 1:(j + 1) * seg, sls[h]]
            s_new = state(j, h) * el + _mm_tn(
                jnp.concatenate([vh[h][rows], -u[h][rows]], axis=0),
                jnp.concatenate([k_hat[h][rows] * el, alpha_hat[h][rows] * el], axis=0))
            if stacked:
                sout_ref[j, h] = s_new
            else:
                s_ref[h] = s_new
    outs = []
    for h in heads:
        sl = sls[h]
        mu = jnp.mean(y[h], axis=-1, keepdims=True)
        yc = y[h] - mu
        var = jnp.mean(yc * yc, axis=-1, keepdims=True)
        yn = yc * lax.rsqrt(var + RWKV_LN_EPS) * lng[:, sl] + lnb[:, sl]
        bonus = jnp.sum(r[:, sl] * k2[:, sl] * rk_w[:, sl], axis=-1, keepdims=True) * vh[h]
        outs.append(yn + bonus)
    o_ref[...] = (jnp.concatenate(outs, axis=1) * gate).reshape(o_ref.shape).astype(BF16)

    if not stacked:
        @pl.when(pl.program_id(1) == pl.num_programs(1) - 1)
        def _():
            sout_ref[0] = s_ref[...]


def _rwkv(proj3, prev, mu, w0, w2, a0, a2, g2, k_k, k_a, r_k, ln_g, ln_b, s_all, acc, layer, *,
          stacked, tv):
    n_seq, t, _ = proj3.shape
    grid, c, seg, shape, imap, smap, nblk = _mixer_grid(n_seq, t, stacked)
    ng_ = len(grid)
    kern = functools.partial(_rwkv_kernel, c=c, seg=seg, tv=tv)
    col = lambda blk, w: pl.BlockSpec(shape(w), imap(blk))
    cs = lambda a: _const_spec(a, ng_)
    prev_spec = pl.BlockSpec((nblk,) + prev.shape[1:], smap(3))
    scratch = [pltpu.VMEM((SUBLANES + c, RWKV_PAD_COLS), F32)]
    if not stacked:
        scratch = [pltpu.VMEM((RWKV_HEADS, RWKV_DH, RWKV_DH), F32)] + scratch
    return pl.pallas_call(
        kern,
        grid=grid,
        in_specs=[col(COL_RR, BRANCH_W), col(COL_RK, BRANCH_W), col(COL_RV, BRANCH_W),
                  col(COL_RWL, LANES), col(COL_RAL, LANES), col(COL_RGL, 2 * LANES),
                  prev_spec, cs(mu), cs(w0), cs(w2), cs(a0), cs(a2), cs(g2),
                  cs(k_k), cs(k_a), cs(r_k), cs(ln_g), cs(ln_b),
                  _layer_state_spec(s_all, layer, nblk, ng_), pl.BlockSpec(memory_space=pl.ANY)],
        out_specs=[col(0, BRANCH_W), _layer_state_spec(acc, layer, nblk, ng_), prev_spec],
        out_shape=[jax.ShapeDtypeStruct((n_seq, t, BRANCH_W), BF16),
                   jax.ShapeDtypeStruct(acc.shape, F32),
                   jax.ShapeDtypeStruct(prev.shape, F32)],
        input_output_aliases={19: 1},
        scratch_shapes=scratch,
        compiler_params=_cparams(ng_),
        name="rwkv7",
    )(proj3, proj3, proj3, proj3, proj3, proj3, prev, mu, w0, w2, a0, a2, g2, k_k, k_a, r_k,
      ln_g, ln_b, s_all, acc)


def _hgrn_kernel(pq_ref, pf_ref, pi_ref, pg_ref, lb_ref, ng_ref, s0_ref, acc_ref, o_ref, sout_ref, *scratch,
                 c, seg, tv):
    stacked = seg < c
    if not stacked:
        (s_ref,) = scratch

        @pl.when(pl.program_id(1) == 0)
        def _():
            s_ref[...] = s0_ref[0]

    lb = lb_ref[...]
    hf = _rows(pf_ref, c)
    log_sig = jnp.minimum(hf, 0.0) - jnp.log1p(jnp.exp(-jnp.abs(hf)))
    x1 = jnp.log(jnp.maximum(lb, LB_TINY))
    x2 = jnp.log1p(-lb) + log_sig
    logf = jnp.maximum(x1, x2) + jnp.log1p(jnp.exp(-jnp.abs(x1 - x2)))
    kf = (1.0 - lb) * _sigmoid(-hf)
    q = _silu(_rows(pq_ref, c))
    v = _rows(pi_ref, c)
    if stacked:
        t_in = _token_rows(c, BRANCH_W)
        valid = jnp.logical_and(t_in >= PRE_ROWS, t_in < PRE_ROWS + tv)
        logf = jnp.where(valid, logf, 0.0)
        kf = jnp.where(valid, kf, 0.0)
    sub = seg if stacked else HGRN_SUB
    n_sub = c // sub
    blk_causal, _ = _seg_masks(c, sub)
    b_all = _mm3(blk_causal.astype(F32), logf)
    causal, _ = _seg_masks(sub, sub)
    mid = sub // 2 - 1
    b_last_rows = jnp.concatenate([b_all[(j + 1) * sub - 1:(j + 1) * sub, :] for j in range(n_sub)]
                                  + [jnp.zeros((SUBLANES - n_sub, BRANCH_W), F32)] * (n_sub < SUBLANES),
                                  axis=0)
    decay_cols = jnp.exp(b_last_rows).T
    ng = ng_ref[...]
    gate = _silu(_rows(pg_ref, c))

    heads = range(HGRN_HEADS)
    sls = [slice(h * HGRN_DH, (h + 1) * HGRN_DH) for h in heads]
    pairs = [(j, h) for j in range(n_sub) for h in heads]
    blk = lambda x, j, h: x[j * sub:(j + 1) * sub, sls[h]]
    e_mid, e_last = {}, {}
    for j, h in pairs:
        b = blk(b_all, j, h)
        e_mid[j, h] = b - b[mid:mid + 1, :]
        e_last[j, h] = b[sub - 1:sub, :] - b
    attn = {jh: jnp.where(causal, _mm_nt(blk(q, *jh) * jnp.exp(jnp.minimum(e_mid[jh], EXP_CLAMP)),
                                         blk(kf, *jh) * jnp.exp(jnp.minimum(-e_mid[jh], EXP_CLAMP))), 0.0)
            for jh in pairs}
    intra = {jh: _mm(attn[jh], blk(v, *jh)) for jh in pairs}
    q_dec = {jh: blk(q, *jh) * jnp.exp(blk(b_all, *jh)) for jh in pairs}
    kv = {jh: _mm_tn(blk(kf, *jh) * jnp.exp(e_last[jh]), blk(v, *jh)) for jh in pairs}
    outs = [[] for _ in heads]
    for j in range(n_sub):
        s = [s0_ref[j, h] if stacked else s_ref[h] for h in heads]
        o = [_mm(q_dec[j, h], s[h]) + intra[j, h] for h in heads]
        for h in heads:
            s_new = s[h] * decay_cols[sls[h], j:j + 1] + kv[j, h]
            if stacked:
                sout_ref[j, h] = s_new
            else:
                s_ref[h] = s_new
            outs[h].append(o[h])
    cols = []
    for h in range(HGRN_HEADS):
        sl = slice(h * HGRN_DH, (h + 1) * HGRN_DH)
        o = outs[h][0] if n_sub == 1 else jnp.concatenate(outs[h], axis=0)
        cols.append(_rms(o, NORM_EPS) * ng * gate[:, sl])
    o_ref[...] = jnp.concatenate(cols, axis=1).reshape(o_ref.shape).astype(BF16)

    if not stacked:
        @pl.when(pl.program_id(1) == pl.num_programs(1) - 1)
        def _():
            sout_ref[0] = s_ref[...]


def _hgrn(proj3, lb, ng, s_all, acc, layer, *, stacked, tv):
    n_seq, t, _ = proj3.shape
    grid, c, seg, shape, imap, smap, nblk = _mixer_grid(n_seq, t, stacked)
    ng_ = len(grid)
    kern = functools.partial(_hgrn_kernel, c=c, seg=seg, tv=tv)
    col = lambda blk: pl.BlockSpec(shape(BRANCH_W), imap(blk))
    scratch = [] if stacked else [pltpu.VMEM((HGRN_HEADS, HGRN_DH, HGRN_DH), F32)]
    return pl.pallas_call(
        kern,
        grid=grid,
        in_specs=[col(COL_HQ), col(COL_HF), col(COL_HI), col(COL_HG),
                  _const_spec(lb, ng_), _const_spec(ng, ng_),
                  _layer_state_spec(s_all, layer, nblk, ng_), pl.BlockSpec(memory_space=pl.ANY)],
        out_specs=[col(0), _layer_state_spec(acc, layer, nblk, ng_)],
        out_shape=[jax.ShapeDtypeStruct((n_seq, t, BRANCH_W), BF16),
                   jax.ShapeDtypeStruct(acc.shape, F32)],
        input_output_aliases={7: 1},
        scratch_shapes=scratch,
        compiler_params=_cparams(ng_),
        name="hgrn2",
    )(proj3, proj3, proj3, proj3, lb, ng, s_all, acc)


def _pad_cols(a, width):
    return jnp.pad(a, [(0, 0)] * (a.ndim - 1) + [(0, width - a.shape[-1])])


def _rwkv_cols_to_padded(a):
    o = 3 * BRANCH_W
    return jnp.concatenate([a[..., :o],
                            _pad_cols(a[..., o:o + RWKV_W_LORA], LANES),
                            _pad_cols(a[..., o + RWKV_W_LORA:o + RWKV_W_LORA + RWKV_A_LORA], LANES),
                            a[..., o + RWKV_W_LORA + RWKV_A_LORA:]], axis=-1)


def _rwkv_cols_from_padded(a):
    o = 3 * BRANCH_W
    return jnp.concatenate([a[..., :o], a[..., o:o + RWKV_W_LORA],
                            a[..., o + LANES:o + LANES + RWKV_A_LORA], a[..., o + 2 * LANES:]], axis=-1)


def _layout_w_in(w):
    s5 = _s5_perm(w[..., :512], -1)
    gdn = w[..., 512:2560]
    ba = w[..., 2560:2568]
    rwkv = w[..., 2568:4552]
    hgrn = w[..., 4552:6600]
    rp = _rwkv_cols_to_padded(rwkv)
    out = jnp.concatenate([s5, gdn, rp[..., :1536], hgrn, rp[..., 1792:2048], _pad_cols(ba, LANES),
                           rp[..., 1536:1664], rp[..., 1664:1792]], axis=-1)
    return _pad_cols(out, PROJ_COLS)


def _prep_big(w):
    return dict(
        w_in=_layout_w_in(w['w_in'].astype(BF16)),
        w_gate=w['w_gate'].astype(BF16),
        w_br=w['w_br'].astype(BF16),
        w_br0=_s5_perm(w['w_br'][:, 0], 1).astype(BF16),
        w_o=w['w_o'].astype(BF16),
        w_up=w['w_up'].astype(BF16),
        w_down=w['w_down'].astype(BF16),
    )


def _lane_vec(a, offset):
    return jnp.zeros((1, LANES), F32).at[0, offset:offset + a.shape[0]].set(a.astype(F32))


def _prep_layer(l, w, s5_chunks):
    row = lambda a: a.astype(F32).reshape(1, -1)
    p = dict(
        g_pre_mix=row(w['g_pre_mix'][l]),
        g_post_mix=row(w['g_post_mix'][l]),
        s5_d=_s5_perm(row(w['s5_d'][l]), 1),
        s5_w_glu=_s5_perm(_s5_perm(w['s5_w_glu'][l], 0), 1).astype(BF16),
        gdn_conv_w=w['gdn_conv_w'][l].astype(F32),
        gdn_alog=_lane_vec(w['gdn_a_log'][l], GDN_HEADS),
        gdn_dtb=_lane_vec(w['gdn_dt_bias'][l], GDN_HEADS),
        gdn_norm_g=row(w['gdn_norm_g'][l]),
        rwkv_mu=_rwkv_cols_to_padded(row(w['rwkv_mu'][l])),
        rwkv_w0=row(w['rwkv_w0'][l]),
        rwkv_w2=jnp.pad(w['rwkv_w2'][l], ((0, LANES - RWKV_W_LORA), (0, 0))).astype(BF16),
        rwkv_a0=row(w['rwkv_a0'][l]),
        rwkv_a2=jnp.pad(w['rwkv_a2'][l], ((0, LANES - RWKV_A_LORA), (0, 0))).astype(BF16),
        rwkv_g2=w['rwkv_g2'][l].astype(BF16),
        rwkv_k_k=row(w['rwkv_k_k'][l]),
        rwkv_k_a=row(w['rwkv_k_a'][l]),
        rwkv_r_k=row(w['rwkv_r_k'][l]),
        rwkv_ln_g=row(w['rwkv_ln_g'][l]),
        rwkv_ln_b=row(w['rwkv_ln_b'][l]),
        hgrn_norm_g=row(w['hgrn_norm_g'][l]),
        g_pre_ffn=row(w['g_pre_ffn'][l]),
        ffn_conv_w=w['ffn_conv_w'][l].astype(F32),
        g_post_ffn=row(w['g_post_ffn'][l]),
    )
    s5_args = (w['s5_a_re'][l], w['s5_a_im'][l], w['s5_log_dt'][l], w['s5_b_re'][l], w['s5_b_im'][l],
               w['s5_c_re'][l], w['s5_c_im'][l])
    p['s5'] = {c: _s5_weights(*s5_args, c) for c in s5_chunks}
    return p


def _layer(x, st, acc, p, big, lb, layer, *, n_seq, t, time_major):
    s5_re, s5_im, gdn_s, gdn_buf, rwkv_s, rwkv_prev, hgrn_s, ffn_buf = st
    acc_gdn, acc_rwkv, acc_hgrn = acc
    rwkv_prev = _rwkv_cols_to_padded(rwkv_prev)
    if time_major:
        lo, hi = PRE_ROWS, PRE_ROWS + t
        pad_rows = lambda a, before: jnp.pad(a, ((0, 0), (before, SEQ_ROWS - before - a.shape[1]), (0, 0)))
        to_seq = lambda a: pad_rows(jnp.transpose(a.reshape(t, n_seq, a.shape[-1]), (1, 0, 2)), lo)
        from_seq = lambda a: jnp.transpose(a, (1, 0, 2)).reshape(t * n_seq, a.shape[-1])
        tokens = lambda a: a[:, lo:hi]
        cbuf = pad_rows(gdn_buf, 0)
        prev = pad_rows(rwkv_prev[:, None, :], lo - 1)
    else:
        to_seq = lambda a: a.reshape(n_seq, t, a.shape[-1])
        from_seq = lambda a: a.reshape(n_seq * t, a.shape[-1])
        tokens = lambda a: a
        cbuf = gdn_buf
        prev = rwkv_prev[:, None, :]

    proj = _norm_matmul(x, p['g_pre_mix'], big['w_in'], layer, tn=1024)
    proj3 = to_seq(proj)

    s5c = min(S5_CHUNK, t)
    o_a, s5_re_n, s5_im_n = _s5_branch(tokens(proj3)[..., :BRANCH_W], s5_re, s5_im, p['s5'][s5c],
                                       p['s5_d'], p['s5_w_glu'], c=s5c)
    o_b, gdn_s_n, cout = _gdn(proj3, p['gdn_conv_w'], cbuf, p['gdn_alog'], p['gdn_dtb'],
                              p['gdn_norm_g'], gdn_s, acc_gdn, layer, stacked=time_major, tv=t)
    o_c, rwkv_s_n, pout = _rwkv(proj3, prev, p['rwkv_mu'], p['rwkv_w0'], p['rwkv_w2'], p['rwkv_a0'],
                                p['rwkv_a2'], p['rwkv_g2'], p['rwkv_k_k'], p['rwkv_k_a'],
                                p['rwkv_r_k'], p['rwkv_ln_g'], p['rwkv_ln_b'], rwkv_s, acc_rwkv, layer,
                                stacked=time_major, tv=t)
    o_d, hgrn_s_n = _hgrn(proj3, lb, p['hgrn_norm_g'], hgrn_s, acc_hgrn, layer,
                          stacked=time_major, tv=t)
    if time_major:
        gdn_buf_n = cout[:, hi - (GDN_CONV - 1):hi]
        rwkv_prev_n = _rwkv_cols_from_padded(pout[:, hi - 1])
    else:
        gdn_buf_n = cout
        rwkv_prev_n = _rwkv_cols_from_padded(pout[:, 0])

    branches = [from_seq(o_a)] + [from_seq(tokens(o)) for o in (o_b, o_c, o_d)]
    mix = _mix(x, p['g_pre_mix'], branches, big['w_gate'], big['w_br0'], big['w_br'], layer)
    x = _matmul_resnorm(mix, big['w_o'], layer, x, p['g_post_mix'])

    if time_major:
        buf = jnp.transpose(ffn_buf, (1, 0, 2)).reshape(1, (FFN_CONV - 1) * n_seq, 2 * D_FF)
        x, nbuf = _ffn(x, p['g_pre_ffn'], big['w_up'], p['ffn_conv_w'], buf, big['w_down'],
                       p['g_post_ffn'], layer, stride=n_seq, rows_per_seq=t * n_seq)
        ffn_buf_n = jnp.transpose(nbuf.reshape(FFN_CONV - 1, n_seq, 2 * D_FF), (1, 0, 2))
    else:
        x, ffn_buf_n = _ffn(x, p['g_pre_ffn'], big['w_up'], p['ffn_conv_w'], ffn_buf, big['w_down'],
                            p['g_post_ffn'], layer, stride=1, rows_per_seq=t)
    return x, (s5_re_n, s5_im_n, gdn_s_n, gdn_buf_n, rwkv_s_n, rwkv_prev_n, hgrn_s_n, ffn_buf_n)


MATRIX_STATES = (2, 4, 6)


def _run_group(x3, states, layers, big, lb_all, *, time_major):
    n_seq, t, d = x3.shape
    if time_major:
        x = jnp.transpose(x3, (1, 0, 2)).reshape(t * n_seq, d)
    else:
        x = x3.reshape(n_seq * t, d)
    new = [[] for _ in states]
    acc = [jnp.zeros(states[i].shape, F32) for i in MATRIX_STATES]
    for l in range(DEPTH):
        st = [s if i in MATRIX_STATES else s[l] for i, s in enumerate(states)]
        x, st = _layer(x, st, acc, layers[l], big, lb_all[l:l + 1], l,
                       n_seq=n_seq, t=t, time_major=time_major)
        acc = [st[i] for i in MATRIX_STATES]
        for lst, s in zip(new, st):
            lst.append(s)
    if time_major:
        y = jnp.transpose(x.reshape(t, n_seq, d), (1, 0, 2))
    else:
        y = x.reshape(n_seq, t, d)
    return y, [acc[MATRIX_STATES.index(i)] if i in MATRIX_STATES else jnp.stack(lst)
               for i, lst in enumerate(new)]


def kernel(x_prompt, x_sample, state_s5_re, state_s5_im, state_gdn, state_gdn_conv, state_rwkv, state_rwkv_shift, state_hgrn, state_ffn_conv, g_pre_mix, w_in, w_gate, w_br, w_o, g_post_mix, s5_a_re, s5_a_im, s5_log_dt, s5_b_re, s5_b_im, s5_c_re, s5_c_im, s5_d, s5_w_glu, gdn_conv_w, gdn_a_log, gdn_dt_bias, gdn_norm_g, rwkv_mu, rwkv_w0, rwkv_w2, rwkv_a0, rwkv_a2, rwkv_g2, rwkv_k_k, rwkv_k_a, rwkv_r_k, rwkv_ln_g, rwkv_ln_b, hgrn_lb_logits, hgrn_norm_g, g_pre_ffn, w_up, ffn_conv_w, w_down, g_post_ffn):
    w = dict(g_pre_mix=g_pre_mix, w_in=w_in, w_gate=w_gate, w_br=w_br, w_o=w_o, g_post_mix=g_post_mix,
             s5_a_re=s5_a_re, s5_a_im=s5_a_im, s5_log_dt=s5_log_dt, s5_b_re=s5_b_re, s5_b_im=s5_b_im,
             s5_c_re=s5_c_re, s5_c_im=s5_c_im, s5_d=s5_d, s5_w_glu=s5_w_glu, gdn_conv_w=gdn_conv_w,
             gdn_a_log=gdn_a_log, gdn_dt_bias=gdn_dt_bias, gdn_norm_g=gdn_norm_g, rwkv_mu=rwkv_mu,
             rwkv_w0=rwkv_w0, rwkv_w2=rwkv_w2, rwkv_a0=rwkv_a0, rwkv_a2=rwkv_a2, rwkv_g2=rwkv_g2,
             rwkv_k_k=rwkv_k_k, rwkv_k_a=rwkv_k_a, rwkv_r_k=rwkv_r_k, rwkv_ln_g=rwkv_ln_g,
             rwkv_ln_b=rwkv_ln_b, hgrn_norm_g=hgrn_norm_g, g_pre_ffn=g_pre_ffn, w_up=w_up,
             ffn_conv_w=ffn_conv_w, w_down=w_down, g_post_ffn=g_post_ffn)
    sm = jax.nn.softmax(hgrn_lb_logits.astype(F32), axis=0)
    lb_all = jnp.maximum(jnp.cumsum(sm, axis=0) - sm[0], 0.0)
    s5_chunks = {min(S5_CHUNK, x_prompt.shape[1]), min(S5_CHUNK, x_sample.shape[1])}
    layers = [_prep_layer(l, w, s5_chunks) for l in range(DEPTH)]
    big = _prep_big(w)
    sample_states = (state_s5_re, state_s5_im, state_gdn, state_gdn_conv, state_rwkv,
                     state_rwkv_shift, state_hgrn, state_ffn_conv)
    nb = x_prompt.shape[0]
    prompt_states = tuple(jnp.zeros((DEPTH, nb) + s.shape[2:], F32) for s in sample_states)
    y_prompt, ps = _run_group(x_prompt, prompt_states, layers, big, lb_all, time_major=False)
    y_sample, ss = _run_group(x_sample, sample_states, layers, big, lb_all, time_major=True)
    out = [y_prompt, y_sample]
    for a, b in zip(ps, ss):
        out.extend((a, b))
    return tuple(out)
```

```python
import functools
import math

import jax
import jax.numpy as jnp
from jax import lax
from jax.experimental import pallas as pl
from jax.experimental.pallas import tpu as pltpu

F32 = jnp.float32
BF16 = jnp.bfloat16

D_MODEL = 2048
DEPTH = 4
N_BRANCH = 4
BRANCH_W = 512
NORM_EPS = 1e-6

S5_GROUP = 16
S5_GROUPS = 32
S5_STATE = 64

GDN_HEADS = 4
GDN_DH = 128
GDN_CONV = 4

RWKV_DH = 64
RWKV_HEADS = 8
RWKV_W_LORA = 96
RWKV_A_LORA = 96
RWKV_G_LORA = 256
RWKV_LN_EPS = 64e-5
RWKV_COLS = 3 * BRANCH_W + RWKV_W_LORA + RWKV_A_LORA + RWKV_G_LORA

HGRN_HEADS = 4
HGRN_DH = 128
HGRN_SUB = 16
LB_TINY = 1e-30
EXP_CLAMP = 80.0

D_FF = 5632
FFN_CONV = 3

LANES = 128
SUBLANES = 8
VMEM_LIMIT_BYTES = 56 * 1024 * 1024

PROJ_COLS = 6912
PROJ_TILE = 768
COL_S5U, COL_GQ, COL_GK, COL_GV, COL_GZ = 0, 1, 2, 3, 4
COL_RR, COL_RK, COL_RV = 5, 6, 7
COL_HQ, COL_HF, COL_HI, COL_HG = 8, 9, 10, 11
COL_RGL = 6144 // 256
COL_GBA, COL_RWL, COL_RAL = 6400 // 128, 6528 // 128, 6656 // 128
RWKV_PAD_COLS = 2048

ROW_TILE = 512
PROMPT_CHUNK = 128
SEQ_ROWS = SUBLANES
PRE_ROWS = GDN_CONV - 1
SEQ_BLOCK = 8
S5_CHUNK = 16
INV_BLOCK = 64


def _s5_perm(a, axis):
    axis %= a.ndim
    shp = a.shape
    a = a.reshape(shp[:axis] + (S5_GROUPS, S5_GROUP) + shp[axis + 1:])
    return jnp.swapaxes(a, axis, axis + 1).reshape(shp)


def _cparams(n_axes):
    return pltpu.CompilerParams(dimension_semantics=("arbitrary",) * n_axes,
                                vmem_limit_bytes=VMEM_LIMIT_BYTES)


def _mm(a, b):
    return jnp.dot(a.astype(BF16), b.astype(BF16), preferred_element_type=F32)


def _mm_nt(a, b):
    return lax.dot_general(a.astype(BF16), b.astype(BF16), (((1,), (1,)), ((), ())),
                           preferred_element_type=F32)


def _mm_tn(a, b):
    return lax.dot_general(a.astype(BF16), b.astype(BF16), (((0,), (0,)), ((), ())),
                           preferred_element_type=F32)


def _split(a):
    hi = a.astype(BF16)
    lo = (a - hi.astype(F32)).astype(BF16)
    return hi, lo


def _mm3(a, b):
    ah, al = _split(a)
    bh, bl = _split(b)
    d = functools.partial(jnp.dot, preferred_element_type=F32)
    return d(ah, bh) + (d(ah, bl) + d(al, bh))


def _mm2(a, b):
    bh, bl = _split(b)
    d = functools.partial(jnp.dot, preferred_element_type=F32)
    return d(a, bh) + d(a, bl)


def _rms(x, eps):
    return x * lax.rsqrt(jnp.mean(x * x, axis=-1, keepdims=True) + eps)


def _sigmoid(x):
    return 1.0 / (1.0 + jnp.exp(-x))


def _silu(x):
    return x * _sigmoid(x)


def _softplus(x):
    return jnp.maximum(x, 0.0) + jnp.log1p(jnp.exp(-jnp.abs(x)))


def _gelu_tanh(x):
    return 0.5 * x * (1.0 + jnp.tanh(math.sqrt(2.0 / math.pi) * (x + 0.044715 * (x * x * x))))


def _seg_masks(c, seg, reps=1):
    r = lax.broadcasted_iota(jnp.int32, (c, reps * c), 0)
    s = lax.broadcasted_iota(jnp.int32, (c, reps * c), 1)
    if reps > 1:
        s = s % c
    causal, strict = r >= s, r > s
    if seg < c:
        same = (r // seg) == (s // seg)
        causal, strict = jnp.logical_and(causal, same), jnp.logical_and(strict, same)
    return causal, strict


def _unit_lower_inverses(mats, seg):
    c = mats[0].shape[0]
    r = lax.broadcasted_iota(jnp.int32, (c, c), 0)
    s = lax.broadcasted_iota(jnp.int32, (c, c), 1)
    eye = (r == s).astype(F32)
    if seg > INV_BLOCK:
        assert seg == 2 * INV_BLOCK
        diag = (r // INV_BLOCK) == (s // INV_BLOCK)
        tb = _unit_lower_inverses([jnp.where(diag, a, 0.0) for a in mats], INV_BLOCK)
        off = [jnp.where(diag, 0.0, a) for a in mats]
        left = [_mm(t, o) for t, o in zip(tb, off)]
        return [t - _mm(x, t) for t, x in zip(tb, left)]
    ps = [-a for a in mats]
    ts = [eye + p for p in ps]
    k = 2
    while k < seg:
        ps = [_mm(p, p) for p in ps]
        ts = [t + _mm(t, p) for t, p in zip(ts, ps)]
        k *= 2
    return ts


def _unit_lower_inverse(a, seg):
    return _unit_lower_inverses([a], seg)[0]


def _rows(ref, c):
    return ref[0] if ref.shape[0] == 1 else ref[...].reshape(c, ref.shape[-1])


def _token_rows(c, width):
    return lax.broadcasted_iota(jnp.int32, (c, width), 0) % SEQ_ROWS


def _norm_matmul_kernel(x_ref, g_ref, w_ref, o_ref, h_ref):
    @pl.when(pl.program_id(1) == 0)
    def _():
        h_ref[...] = (_rms(x_ref[...], NORM_EPS) * g_ref[...]).astype(BF16)

    o_ref[...] = jnp.dot(h_ref[...], w_ref[...], preferred_element_type=F32)


def _norm_matmul(x, g, w_all, layer, *, tn):
    rows, d = x.shape
    n = w_all.shape[2]
    tm = min(ROW_TILE, rows)
    return pl.pallas_call(
        _norm_matmul_kernel,
        grid=(rows // tm, n // tn),
        in_specs=[pl.BlockSpec((tm, d), lambda i, j: (i, 0)),
                  pl.BlockSpec((1, d), lambda i, j: (0, 0)),
                  pl.BlockSpec((None, d, tn), lambda i, j: (layer, 0, j))],
        out_specs=pl.BlockSpec((tm, tn), lambda i, j: (i, j)),
        out_shape=jax.ShapeDtypeStruct((rows, n), F32),
        scratch_shapes=[pltpu.VMEM((tm, d), BF16)],
        compiler_params=_cparams(2),
        name="norm_matmul",
    )(x, g, w_all)


def _mix_kernel(x_ref, g_ref, oa_ref, ob_ref, oc_ref, od_ref, wga_ref, wgb_ref, wgc_ref, wgd_ref,
                wbr0_ref, wbr_ref, o_ref, h_ref):
    @pl.when(pl.program_id(1) == 0)
    def _():
        h_ref[...] = (_rms(x_ref[...], NORM_EPS) * g_ref[...]).astype(BF16)

    h = h_ref[...]
    acc = None
    for n, (br_ref, wg_ref) in enumerate(((oa_ref, wga_ref), (ob_ref, wgb_ref), (oc_ref, wgc_ref),
                                          (od_ref, wgd_ref))):
        gate = _sigmoid(jnp.dot(h, wg_ref[...], preferred_element_type=F32))
        w_n = wbr0_ref[...] if n == 0 else wbr_ref[n]
        term = gate * jnp.dot(br_ref[...], w_n, preferred_element_type=F32)
        acc = term if acc is None else acc + term
    o_ref[...] = acc.astype(BF16)


def _mix(x, g, branches, wg, wbr0, wbr, layer, *, tn=512):
    rows, d = x.shape
    tm = min(ROW_TILE, rows)
    bw = branches[0].shape[1]
    br_spec = pl.BlockSpec((tm, bw), lambda i, j: (i, 0))
    nt = d // tn
    wg_specs = [pl.BlockSpec((None, d, tn), functools.partial(lambda i, j, n: (layer, 0, n * nt + j), n=n))
                for n in range(N_BRANCH)]
    return pl.pallas_call(
        _mix_kernel,
        grid=(rows // tm, nt),
        in_specs=[pl.BlockSpec((tm, d), lambda i, j: (i, 0)),
                  pl.BlockSpec((1, d), lambda i, j: (0, 0)),
                  br_spec, br_spec, br_spec, br_spec, *wg_specs,
                  pl.BlockSpec((None, bw, tn), lambda i, j: (layer, 0, j)),
                  pl.BlockSpec((None, N_BRANCH, bw, tn), lambda i, j: (layer, 0, 0, j))],
        out_specs=pl.BlockSpec((tm, tn), lambda i, j: (i, j)),
        out_shape=jax.ShapeDtypeStruct((rows, d), BF16),
        scratch_shapes=[pltpu.VMEM((tm, d), BF16)],
        compiler_params=_cparams(2),
        name="branch_mix",
    )(x, g, *branches, wg, wg, wg, wg, wbr0, wbr)


def _matmul_resnorm_kernel(a_ref, w_ref, x_ref, g_ref, o_ref):
    y = jnp.dot(a_ref[...], w_ref[...], preferred_element_type=F32)
    o_ref[...] = x_ref[...] + _rms(y, NORM_EPS) * g_ref[...]


def _matmul_resnorm(a, w_all, layer, x, g):
    rows, kdim = a.shape
    d = w_all.shape[2]
    tm = min(ROW_TILE, rows)
    return pl.pallas_call(
        _matmul_resnorm_kernel,
        grid=(rows // tm,),
        in_specs=[pl.BlockSpec((tm, kdim), lambda i: (i, 0)),
                  pl.BlockSpec((None, kdim, d), lambda i: (layer, 0, 0)),
                  pl.BlockSpec((tm, d), lambda i: (i, 0)),
                  pl.BlockSpec((1, d), lambda i: (0, 0))],
        out_specs=pl.BlockSpec((tm, d), lambda i: (i, 0)),
        out_shape=jax.ShapeDtypeStruct((rows, d), F32),
        compiler_params=_cparams(1),
        name="matmul_resnorm",
    )(a, w_all, x, g)


def _ffn_kernel(x_ref, gpre_ref, wa_ref, wb_ref, cwa_ref, cwb_ref, bufa_ref, bufb_ref, wd_ref,
                gpost_ref, o_ref, nbufa_ref, nbufb_ref, h_ref, acc_ref, sa_ref, sb_ref, carry_ref,
                *, tm, stride, tiles_per_seq):
    i = pl.program_id(0)
    f = pl.program_id(1)
    keep = (FFN_CONV - 1) * stride
    pad = max(SUBLANES, keep)

    @pl.when(f == 0)
    def _():
        h_ref[...] = (_rms(x_ref[...], NORM_EPS) * gpre_ref[...]).astype(BF16)
        acc_ref[...] = jnp.zeros_like(acc_ref)

    h = h_ref[...]
    halves = []
    for idx, (w_ref, cw_ref, buf_ref, nbuf_ref, s_ref) in enumerate(
            ((wa_ref, cwa_ref, bufa_ref, nbufa_ref, sa_ref),
             (wb_ref, cwb_ref, bufb_ref, nbufb_ref, sb_ref))):
        if tiles_per_seq == 1:
            s_ref[pad - keep:pad, :] = buf_ref[0]
        else:
            first = (i % tiles_per_seq) == 0

            @pl.when(first)
            def _():
                s_ref[pad - keep:pad, :] = buf_ref[0]

            @pl.when(jnp.logical_not(first))
            def _():
                s_ref[pad - keep:pad, :] = carry_ref[f, idx]

        s_ref[pad:pad + tm, :] = jnp.dot(h, w_ref[...], preferred_element_type=F32)
        cw = cw_ref[...]
        y = None
        for j in range(FFN_CONV):
            off = pad - (FFN_CONV - 1 - j) * stride
            term = cw[j:j + 1, :] * s_ref[off:off + tm, :]
            y = term if y is None else y + term
        last = s_ref[pad + tm - keep:pad + tm, :]
        nbuf_ref[0] = last
        if tiles_per_seq > 1:
            carry_ref[f, idx] = last
        halves.append(y)

    act = (_gelu_tanh(halves[0]) * halves[1]).astype(BF16)
    acc_ref[...] += jnp.dot(act, wd_ref[...], preferred_element_type=F32)

    @pl.when(f == pl.num_programs(1) - 1)
    def _():
        o_ref[...] = x_ref[...] + _rms(acc_ref[...], NORM_EPS) * gpost_ref[...]


def _ffn(x, gpre, w_up, conv_w, buf, w_down, gpost, layer, *, stride, rows_per_seq, tf=512):
    rows, d = x.shape
    tm = min(ROW_TILE, rows_per_seq)
    tiles_per_seq = rows_per_seq // tm
    nf = D_FF // tf
    keep = (FFN_CONV - 1) * stride
    pad = max(SUBLANES, keep)
    carry_rows = keep if tiles_per_seq > 1 else SUBLANES
    kern = functools.partial(_ffn_kernel, tm=tm, stride=stride, tiles_per_seq=tiles_per_seq)
    buf_a = pl.BlockSpec((1, keep, tf), lambda i, f: (i // tiles_per_seq, 0, f))
    buf_b = pl.BlockSpec((1, keep, tf), lambda i, f: (i // tiles_per_seq, 0, nf + f))
    y, nbuf_a, nbuf_b = pl.pallas_call(
        kern,
        grid=(rows // tm, nf),
        in_specs=[pl.BlockSpec((tm, d), lambda i, f: (i, 0)),
                  pl.BlockSpec((1, d), lambda i, f: (0, 0)),
                  pl.BlockSpec((None, d, tf), lambda i, f: (layer, 0, f)),
                  pl.BlockSpec((None, d, tf), lambda i, f: (layer, 0, nf + f)),
                  pl.BlockSpec((FFN_CONV, tf), lambda i, f: (0, f)),
                  pl.BlockSpec((FFN_CONV, tf), lambda i, f: (0, nf + f)),
                  buf_a, buf_b,
                  pl.BlockSpec((None, tf, d), lambda i, f: (layer, f, 0)),
                  pl.BlockSpec((1, d), lambda i, f: (0, 0))],
        out_specs=[pl.BlockSpec((tm, d), lambda i, f: (i, 0)),
                   pl.BlockSpec((1, keep, tf), lambda i, f: (i, 0, f)),
                   pl.BlockSpec((1, keep, tf), lambda i, f: (i, 0, f))],
        out_shape=[jax.ShapeDtypeStruct((rows, d), F32),
                   jax.ShapeDtypeStruct((rows // tm, keep, D_FF), F32),
                   jax.ShapeDtypeStruct((rows // tm, keep, D_FF), F32)],
        scratch_shapes=[pltpu.VMEM((tm, d), BF16),
                        pltpu.VMEM((tm, d), F32),
                        pltpu.VMEM((pad + tm, tf), F32),
                        pltpu.VMEM((pad + tm, tf), F32),
                        pltpu.VMEM((nf, 2, carry_rows, tf), F32)],
        compiler_params=_cparams(2),
        name="conv_ffn",
    )(x, gpre, w_up, w_up, conv_w, conv_w, buf, buf, w_down, gpost)
    nbuf = jnp.concatenate([nbuf_a, nbuf_b], axis=-1)
    return y, nbuf[tiles_per_seq - 1::tiles_per_seq]


def _s5_kernel(u_ref, h0_ref, bst_ref, kst_ref, cst_ref, apow_ref, y_ref, hfin_ref, e_ref,
               *, n_seq, n_chunks):
    rows = n_seq * n_chunks
    u = u_ref[0]
    h0 = h0_ref[0]
    half = S5_STATE
    if n_chunks > 1:
        row = lax.broadcasted_iota(jnp.int32, (rows, 2 * half), 0)
        placed = jnp.zeros((rows, 2 * half), F32)
        for b in range(n_seq):
            placed = jnp.where(row == b * n_chunks, h0[b:b + 1, :], placed)
        h0 = placed

    def cmul(k, x):
        return x * apow_ref[0, 2 * k:2 * k + 1, :] + pltpu.roll(x, half, 1) * apow_ref[0, 2 * k + 1:2 * k + 2, :]

    e = _mm2(u, bst_ref[0]) + cmul(0, h0)
    if n_chunks > 1:
        j = lax.broadcasted_iota(jnp.int32, (rows, 2 * half), 0) % n_chunks
        k, sh = 0, 1
        while sh < n_chunks:
            e = e + jnp.where(j >= sh, cmul(k, pltpu.roll(e, sh, 0)), 0.0)
            k, sh = k + 1, sh * 2
        e_ref[...] = e
        hfin_ref[0] = e_ref[pl.ds(n_chunks - 1, n_seq, stride=n_chunks), :]
        h_start = jnp.where(j >= 1, pltpu.roll(e, 1, 0), 0.0) + h0
    else:
        hfin_ref[0] = e
        h_start = h0
    y_ref[0] = _mm2(u, kst_ref[0]) + _mm3(h_start, cst_ref[0])


def _s5_scan(u_g, h0_g, bst, kst, cst, apow, *, n_seq, n_chunks):
    groups, rows, cw = u_g.shape
    kern = functools.partial(_s5_kernel, n_seq=n_seq, n_chunks=n_chunks)
    spec3 = lambda a: pl.BlockSpec((1,) + a.shape[1:], lambda g: (g, 0, 0))
    return pl.pallas_call(
        kern,
        grid=(groups,),
        in_specs=[spec3(u_g), spec3(h0_g), spec3(bst), spec3(kst), spec3(cst), spec3(apow)],
        out_specs=[pl.BlockSpec((1, rows, cw), lambda g: (g, 0, 0)),
                   pl.BlockSpec((1, n_seq, 2 * S5_STATE), lambda g: (g, 0, 0))],
        out_shape=[jax.ShapeDtypeStruct((groups, rows, cw), F32),
                   jax.ShapeDtypeStruct((groups, n_seq, 2 * S5_STATE), F32)],
        scratch_shapes=[pltpu.VMEM((rows, 2 * S5_STATE), F32)],
        compiler_params=_cparams(1),
        name="s5_scan",
    )(u_g, h0_g, bst, kst, cst, apow)


def _s5_glu_kernel(y_ref, u_ref, d_ref, w_ref, o_ref):
    g = _gelu_tanh(y_ref[...] + d_ref[...] * u_ref[...])
    o_ref[...] = (g * _sigmoid(_mm(g, w_ref[...]))).astype(BF16)


def _s5_glu(y, u, d, w_glu):
    rows, bw = y.shape
    tm = min(ROW_TILE, rows)
    row_spec = pl.BlockSpec((tm, bw), lambda i: (i, 0))
    return pl.pallas_call(
        _s5_glu_kernel,
        grid=(rows // tm,),
        in_specs=[row_spec, row_spec,
                  pl.BlockSpec((1, bw), lambda i: (0, 0)),
                  pl.BlockSpec((bw, bw), lambda i: (0, 0))],
        out_specs=row_spec,
        out_shape=jax.ShapeDtypeStruct((rows, bw), BF16),
        compiler_params=_cparams(1),
        name="s5_glu",
    )(y, u, d, w_glu)


def _s5_weights(a_re, a_im, log_dt, b_re, b_im, c_re, c_im, c):
    lam = lax.complex(a_re.astype(F32), a_im.astype(F32))
    ldt = lam * jnp.exp(log_dt.astype(F32))[:, None]
    a_bar = jnp.exp(ldt)
    b_bar = ((a_bar - 1.0) / lam)[..., None] * lax.complex(b_re.astype(F32), b_im.astype(F32))
    cm = lax.complex(c_re.astype(F32), c_im.astype(F32))
    tau = jnp.arange(c + 1, dtype=F32)
    apw = jnp.exp(ldt[None] * tau[:, None, None])
    bst = apw[:c][::-1][:, :, :, None] * b_bar[None]
    bst = jnp.transpose(bst, (1, 0, 3, 2)).reshape(S5_GROUPS, c * S5_GROUP, S5_STATE)
    bst = jnp.concatenate([jnp.real(bst), jnp.imag(bst)], axis=-1)
    kt = jnp.real(jnp.einsum('ghp,tgp,gpk->tgkh', cm, apw[:c], b_bar))
    s_idx = jnp.arange(c)[:, None, None]
    t_idx = jnp.arange(c)[None, :, None]
    lag = (t_idx - s_idx == jnp.arange(c)[None, None, :]).astype(F32)
    kst = jnp.einsum('stu,ugkh->stgkh', lag, kt, precision=lax.Precision.HIGHEST)
    kst = jnp.transpose(kst, (2, 0, 3, 1, 4)).reshape(S5_GROUPS, c * S5_GROUP, c * S5_GROUP)
    ca = cm[None] * apw[1:c + 1][:, :, None, :]
    ca = jnp.transpose(ca, (1, 3, 0, 2)).reshape(S5_GROUPS, S5_STATE, c * S5_GROUP)
    cst = jnp.concatenate([jnp.real(ca), -jnp.imag(ca)], axis=1)
    rows = []
    k = 0
    while True:
        p = jnp.exp(ldt * float(c * 2 ** k))
        rows.append(jnp.concatenate([jnp.real(p), jnp.real(p)], axis=-1))
        rows.append(jnp.concatenate([-jnp.imag(p), jnp.imag(p)], axis=-1))
        k += 1
        if c * 2 ** k > 4096:
            break
    apow = jnp.stack(rows, axis=1)
    return bst, kst, cst, apow


def _s5_branch(u3, h0_re, h0_im, sw, d_skip, w_glu, *, c):
    n_seq, t, _ = u3.shape
    n_chunks = t // c
    bst, kst, cst, apow = sw
    u2 = u3.reshape(n_seq * t, BRANCH_W)
    u_g = u2.astype(BF16).reshape(n_seq * t * S5_GROUP, S5_GROUPS).T
    u_g = u_g.reshape(S5_GROUPS, n_seq * n_chunks, c * S5_GROUP)
    h0 = jnp.concatenate([h0_re, h0_im], axis=-1)
    h0 = jnp.transpose(h0, (1, 0, 2))
    y_g, hfin = _s5_scan(u_g, h0, bst, kst, cst, apow, n_seq=n_seq, n_chunks=n_chunks)
    y = y_g.reshape(S5_GROUPS, n_seq * t * S5_GROUP).T.reshape(n_seq * t, BRANCH_W)
    o = _s5_glu(y, u2, d_skip, w_glu)
    hfin = jnp.transpose(hfin, (1, 0, 2))
    return o.reshape(n_seq, t, BRANCH_W), hfin[..., :S5_STATE], hfin[..., S5_STATE:]


def _mixer_grid(n_seq, t, stacked):
    if stacked:
        assert t == SEQ_ROWS and n_seq % SEQ_BLOCK == 0
        c = SEQ_BLOCK * SEQ_ROWS
        shape = lambda w: (SEQ_BLOCK, SEQ_ROWS, w)
        imap = lambda blk: (lambda b: (b, 0, blk))
        smap = lambda nd: (lambda b: (b,) + (0,) * (nd - 1))
        return (n_seq // SEQ_BLOCK,), c, SEQ_ROWS, shape, imap, smap, SEQ_BLOCK
    assert t % PROMPT_CHUNK == 0
    c = PROMPT_CHUNK
    shape = lambda w: (1, c, w)
    imap = lambda blk: (lambda b, i: (b, i, blk))
    smap = lambda nd: (lambda b, i: (b,) + (0,) * (nd - 1))
    return (n_seq, t // c), c, c, shape, imap, smap, 1


def _layer_state_spec(s_all, layer, nblk, n_grid):
    tail = (0,) * (s_all.ndim - 2)
    imap = (lambda b: (layer, b) + tail) if n_grid == 1 else (lambda b, i: (layer, b) + tail)
    return pl.BlockSpec((None, nblk) + s_all.shape[2:], imap)


def _const_spec(a, n_grid):
    zeros = (0,) * a.ndim
    return pl.BlockSpec(a.shape, (lambda b: zeros) if n_grid == 1 else (lambda b, i: zeros))


def _gdn_kernel(pq_ref, pk_ref, pv_ref, pz_ref, pba_ref, cw_ref, cbuf_ref, alog_ref, dtb_ref,
                ng_ref, s0_ref, acc_ref, o_ref, sout_ref, cout_ref, *scratch, c, seg, tv):
    stacked = seg < c
    keep = GDN_CONV - 1
    x = jnp.concatenate([_rows(pq_ref, c), _rows(pk_ref, c), _rows(pv_ref, c)], axis=1)
    if stacked:
        (conv_ref,) = scratch
        t_in = _token_rows(c, 3 * BRANCH_W)
        x = jnp.where(t_in < keep, _rows(cbuf_ref, c), x)
        conv_ref[0:SUBLANES, :] = jnp.zeros((SUBLANES, 3 * BRANCH_W), F32)
        cout_ref[...] = x.reshape(cout_ref.shape)
    else:
        s_ref, conv_ref = scratch

        @pl.when(pl.program_id(1) == 0)
        def _():
            s_ref[...] = s0_ref[0]
            conv_ref[SUBLANES - keep:SUBLANES, :] = cbuf_ref[0]

    conv_ref[SUBLANES:SUBLANES + c, :] = x
    cw = cw_ref[...]
    y = None
    for j in range(GDN_CONV):
        off = SUBLANES - keep + j
        term = cw[j:j + 1, :] * conv_ref[off:off + c, :]
        y = term if y is None else y + term
    if not stacked:
        last = conv_ref[SUBLANES + c - keep:SUBLANES + c, :]
        conv_ref[SUBLANES - keep:SUBLANES, :] = last
        cout_ref[0] = last
    act = _silu(y)

    ba = _rows(pba_ref, c)
    beta_all = _sigmoid(ba)
    g_all = -jnp.exp(alog_ref[...]) * _softplus(ba + dtb_ref[...])
    if stacked:
        t_in = _token_rows(c, LANES)
        valid = jnp.logical_and(t_in >= PRE_ROWS, t_in < PRE_ROWS + tv)
        beta_all = jnp.where(valid, beta_all, 0.0)
        g_all = jnp.where(valid, g_all, 0.0)
    causal, strict = _seg_masks(c, seg)
    d_all = _mm3(causal.astype(F32), g_all)
    dt_all = d_all.T
    ng = ng_ref[...]
    z = _rows(pz_ref, c)
    n_seg = c // seg

    heads = range(GDN_HEADS)
    sls = [slice(h * GDN_DH, (h + 1) * GDN_DH) for h in heads]
    segs = [slice(j * seg, (j + 1) * seg) for j in range(n_seg)]
    cat0 = lambda xs: xs[0] if len(xs) == 1 else jnp.concatenate(xs, axis=0)
    state = lambda j, h: s0_ref[j, h] if stacked else s_ref[h]
    q = [act[:, sl] for sl in sls]
    k = [act[:, BRANCH_W + h * GDN_DH:BRANCH_W + (h + 1) * GDN_DH] for h in heads]
    v = [act[:, 2 * BRANCH_W + h * GDN_DH:2 * BRANCH_W + (h + 1) * GDN_DH] for h in heads]
    q = [x * lax.rsqrt(jnp.sum(x * x, axis=-1, keepdims=True) + 1e-6) * (GDN_DH ** -0.5) for x in q]
    k = [x * lax.rsqrt(jnp.sum(x * x, axis=-1, keepdims=True) + 1e-6) for x in k]
    beta = [beta_all[:, h:h + 1] for h in heads]
    d = [d_all[:, GDN_HEADS + h:GDN_HEADS + h + 1] for h in heads]
    d_row = [dt_all[GDN_HEADS + h:GDN_HEADS + h + 1, :] for h in heads]
    decay = [jnp.where(causal, jnp.exp(jnp.where(causal, d[h] - d_row[h], 0.0)), 0.0) for h in heads]
    kb = [k[h] * beta[h] for h in heads]
    prod = [_mm_nt(jnp.concatenate([kb[h], q[h]], axis=0), k[h]) for h in heads]
    m = [jnp.where(strict, prod[h][:c] * decay[h], 0.0) for h in heads]
    attn = [prod[h][c:] * decay[h] for h in heads]
    t_inv = _unit_lower_inverses(m, seg)
    ed = [jnp.exp(x) for x in d]
    sol = [_mm(t_inv[h], jnp.concatenate([v[h] * beta[h], kb[h] * ed[h]], axis=1)) for h in heads]
    qe = [q[h] * ed[h] for h in heads]
    both = [[_mm(jnp.concatenate([sol[h][rows, GDN_DH:], qe[h][rows]], axis=0), state(j, h))
             for j, rows in enumerate(segs)] for h in heads]
    u = [cat0([sol[h][rows, :GDN_DH] - both[h][j][:seg] for j, rows in enumerate(segs)]) for h in heads]
    qs = [cat0([b[seg:] for b in both[h]]) for h in heads]
    o = [qs[h] + _mm(attn[h], u[h]) for h in heads]
    for h in heads:
        for j, rows in enumerate(segs):
            dl = d[h][(j + 1) * seg - 1:(j + 1) * seg, :]
            s_new = state(j, h) * jnp.exp(dl) + _mm_tn(k[h][rows] * jnp.exp(dl - d[h][rows]), u[h][rows])
            if stacked:
                sout_ref[j, h] = s_new
            else:
                s_ref[h] = s_new
    outs = [_rms(o[h], NORM_EPS) * ng * _silu(z[:, sls[h]]) for h in heads]
    o_ref[...] = jnp.concatenate(outs, axis=1).reshape(o_ref.shape).astype(BF16)

    if not stacked:
        @pl.when(pl.program_id(1) == pl.num_programs(1) - 1)
        def _():
            sout_ref[0] = s_ref[...]


def _gdn(proj3, conv_w, cbuf, alog, dtb, ng, s_all, acc, layer, *, stacked, tv):
    n_seq, t, _ = proj3.shape
    grid, c, seg, shape, imap, smap, nblk = _mixer_grid(n_seq, t, stacked)
    ng_ = len(grid)
    kern = functools.partial(_gdn_kernel, c=c, seg=seg, tv=tv)
    col = lambda blk, w: pl.BlockSpec(shape(w), imap(blk))
    cbuf_spec = pl.BlockSpec((nblk,) + cbuf.shape[1:], smap(3))
    scratch = [pltpu.VMEM((SUBLANES + c, 3 * BRANCH_W), F32)]
    if not stacked:
        scratch = [pltpu.VMEM((GDN_HEADS, GDN_DH, GDN_DH), F32)] + scratch
    return pl.pallas_call(
        kern,
        grid=grid,
        in_specs=[col(COL_GQ, BRANCH_W), col(COL_GK, BRANCH_W), col(COL_GV, BRANCH_W),
                  col(COL_GZ, BRANCH_W), col(COL_GBA, LANES),
                  _const_spec(conv_w, ng_), cbuf_spec, _const_spec(alog, ng_), _const_spec(dtb, ng_),
                  _const_spec(ng, ng_), _layer_state_spec(s_all, layer, nblk, ng_),
                  pl.BlockSpec(memory_space=pl.ANY)],
        out_specs=[col(0, BRANCH_W), _layer_state_spec(acc, layer, nblk, ng_), cbuf_spec],
        out_shape=[jax.ShapeDtypeStruct((n_seq, t, BRANCH_W), BF16),
                   jax.ShapeDtypeStruct(acc.shape, F32),
                   jax.ShapeDtypeStruct(cbuf.shape, F32)],
        input_output_aliases={11: 1},
        scratch_shapes=scratch,
        compiler_params=_cparams(ng_),
        name="gdn",
    )(proj3, proj3, proj3, proj3, proj3, conv_w, cbuf, alog, dtb, ng, s_all, acc)


def _rwkv_kernel(pr_ref, pk_ref, pv_ref, pwl_ref, pal_ref, pgl_ref, prev_ref, mu_ref, w0_ref, w2_ref,
                 a0_ref, a2_ref, g2_ref, kk_ref, ka_ref, rk_ref, lng_ref, lnb_ref, s0_ref, acc_ref,
                 o_ref, sout_ref, pout_ref, *scratch, c, seg, tv):
    stacked = seg < c
    p = jnp.concatenate([_rows(pr_ref, c), _rows(pk_ref, c), _rows(pv_ref, c), _rows(pwl_ref, c),
                         _rows(pal_ref, c), _rows(pgl_ref, c)], axis=1)
    if stacked:
        (p_ref,) = scratch
        t_in = _token_rows(c, RWKV_PAD_COLS)
        p = jnp.where(t_in == PRE_ROWS - 1, _rows(prev_ref, c), p)
        p_ref[0:SUBLANES, :] = jnp.zeros((SUBLANES, RWKV_PAD_COLS), F32)
        pout_ref[...] = p.reshape(pout_ref.shape)
    else:
        s_ref, p_ref = scratch

        @pl.when(pl.program_id(1) == 0)
        def _():
            s_ref[...] = s0_ref[0]
            p_ref[SUBLANES - 1:SUBLANES, :] = prev_ref[0]

    p_ref[SUBLANES:SUBLANES + c, :] = p
    prev = p_ref[SUBLANES - 1:SUBLANES - 1 + c, :]
    if not stacked:
        last = p_ref[SUBLANES + c - 1:SUBLANES + c, :]
        p_ref[SUBLANES - 1:SUBLANES, :] = last
        pout_ref[0] = last
    pm = p + (prev - p) * mu_ref[...]
    r = pm[:, :BRANCH_W]
    k = pm[:, BRANCH_W:2 * BRANCH_W]
    v = pm[:, 2 * BRANCH_W:3 * BRANCH_W]
    wl = pm[:, 3 * BRANCH_W:3 * BRANCH_W + LANES]
    al = pm[:, 3 * BRANCH_W + LANES:3 * BRANCH_W + 2 * LANES]
    gl = pm[:, 3 * BRANCH_W + 2 * LANES:]
    w_log = -_softplus(-(w0_ref[...] + _mm(jnp.tanh(wl), w2_ref[...]))) - 0.5
    lw = -jnp.exp(w_log)
    a = _sigmoid(a0_ref[...] + _mm(al, a2_ref[...]))
    gate = _mm(_sigmoid(gl), g2_ref[...])
    kk_un = k * kk_ref[...]
    k2 = k * (1.0 + (a - 1.0) * ka_ref[...])
    if stacked:
        t_in = _token_rows(c, BRANCH_W)
        valid = jnp.logical_and(t_in >= PRE_ROWS, t_in < PRE_ROWS + tv)
        lw = jnp.where(valid, lw, 0.0)
        kk_un = jnp.where(valid, kk_un, 0.0)
        k2 = jnp.where(valid, k2, 0.0)
    causal, _ = _seg_masks(c, seg)
    causal2, strict2 = _seg_masks(c, seg, reps=2)
    g_cum = _mm3(causal.astype(F32), lw)
    e_pos = jnp.exp(g_cum)
    e_neg = jnp.exp(-g_cum)
    e_prev = jnp.exp(g_cum - lw)
    rk_w = rk_ref[...]
    lng = lng_ref[...]
    lnb = lnb_ref[...]
    n_seg = c // seg

    heads = range(RWKV_HEADS)
    sls = [slice(h * RWKV_DH, (h + 1) * RWKV_DH) for h in heads]
    segs = [slice(j * seg, (j + 1) * seg) for j in range(n_seg)]
    cat0 = lambda xs: xs[0] if len(xs) == 1 else jnp.concatenate(xs, axis=0)
    state = lambda j, h: s0_ref[j, h] if stacked else s_ref[h]
    kkh = [kk_un[:, sl] for sl in sls]
    kkh = [x * lax.rsqrt(jnp.sum(x * x, axis=-1, keepdims=True) + 1e-6) for x in kkh]
    alpha_hat = [kkh[h] * a[:, sls[h]] * e_neg[:, sls[h]] for h in heads]
    k_hat = [k2[:, sl] * e_neg[:, sl] for sl in sls]
    kap = [kkh[h] * e_prev[:, sls[h]] for h in heads]
    r_t = [r[:, sl] * e_pos[:, sl] for sl in sls]
    vh = [v[:, sl] for sl in sls]
    rhs = [jnp.concatenate([alpha_hat[h], k_hat[h]], axis=0) for h in heads]
    ab = [jnp.where(strict2, _mm_nt(kap[h], rhs[h]), 0.0) for h in heads]
    rr = [jnp.where(causal2, _mm_nt(r_t[h], rhs[h]), 0.0) for h in heads]
    both = [[_mm_nt(jnp.concatenate([kap[h][rows], r_t[h][rows]], axis=0), state(j, h))
             for j, rows in enumerate(segs)] for h in heads]
    bv = [_mm(jnp.concatenate([ab[h][:, c:], rr[h][:, c:]], axis=0), vh[h]) for h in heads]
    t_inv = _unit_lower_inverses([ab[h][:, :c] for h in heads], seg)
    ks = [cat0([b[:seg] for b in both[h]]) for h in heads]
    rs = [cat0([b[seg:] for b in both[h]]) for h in heads]
    u = [_mm(t_inv[h], ks[h] + bv[h][:c]) for h in heads]
    y = [rs[h] + bv[h][c:] - _mm(rr[h][:, :c], u[h]) for h in heads]
    for h in heads:
        for j, rows in enumerate(segs):
            el = e_pos[(j + 1) * seg - 1:(j + 1) * seg, sls[h]]
            s_new = state(j, h) * el + _mm_tn(
                jnp.concatenate([vh[h][rows], -u[h][rows]], axis=0),
                jnp.concatenate([k_hat[h][rows] * el, alpha_hat[h][rows] * el], axis=0))
            if stacked:
                sout_ref[j, h] = s_new
            else:
                s_ref[h] = s_new
    outs = []
    for h in heads:
        sl = sls[h]
        mu = jnp.mean(y[h], axis=-1, keepdims=True)
        yc = y[h] - mu
        var = jnp.mean(yc * yc, axis=-1, keepdims=True)
        yn = yc * lax.rsqrt(var + RWKV_LN_EPS) * lng[:, sl] + lnb[:, sl]
        bonus = jnp.sum(r[:, sl] * k2[:, sl] * rk_w[:, sl], axis=-1, keepdims=True) * vh[h]
        outs.append(yn + bonus)
    o_ref[...] = (jnp.concatenate(outs, axis=1) * gate).reshape(o_ref.shape).astype(BF16)

    if not stacked:
        @pl.when(pl.program_id(1) == pl.num_programs(1) - 1)
        def _():
            sout_ref[0] = s_ref[...]


def _rwkv(proj3, prev, mu, w0, w2, a0, a2, g2, k_k, k_a, r_k, ln_g, ln_b, s_all, acc, layer, *,
          stacked, tv):
    n_seq, t, _ = proj3.shape
    grid, c, seg, shape, imap, smap, nblk = _mixer_grid(n_seq, t, stacked)
    ng_ = len(grid)
    kern = functools.partial(_rwkv_kernel, c=c, seg=seg, tv=tv)
    col = lambda blk, w: pl.BlockSpec(shape(w), imap(blk))
    cs = lambda a: _const_spec(a, ng_)
    prev_spec = pl.BlockSpec((nblk,) + prev.shape[1:], smap(3))
    scratch = [pltpu.VMEM((SUBLANES + c, RWKV_PAD_COLS), F32)]
    if not stacked:
        scratch = [pltpu.VMEM((RWKV_HEADS, RWKV_DH, RWKV_DH), F32)] + scratch
    return pl.pallas_call(
        kern,
        grid=grid,
        in_specs=[col(COL_RR, BRANCH_W), col(COL_RK, BRANCH_W), col(COL_RV, BRANCH_W),
                  col(COL_RWL, LANES), col(COL_RAL, LANES), col(COL_RGL, 2 * LANES),
                  prev_spec, cs(mu), cs(w0), cs(w2), cs(a0), cs(a2), cs(g2),
                  cs(k_k), cs(k_a), cs(r_k), cs(ln_g), cs(ln_b),
                  _layer_state_spec(s_all, layer, nblk, ng_), pl.BlockSpec(memory_space=pl.ANY)],
        out_specs=[col(0, BRANCH_W), _layer_state_spec(acc, layer, nblk, ng_), prev_spec],
        out_shape=[jax.ShapeDtypeStruct((n_seq, t, BRANCH_W), BF16),
                   jax.ShapeDtypeStruct(acc.shape, F32),
                   jax.ShapeDtypeStruct(prev.shape, F32)],
        input_output_aliases={19: 1},
        scratch_shapes=scratch,
        compiler_params=_cparams(ng_),
        name="rwkv7",
    )(proj3, proj3, proj3, proj3, proj3, proj3, prev, mu, w0, w2, a0, a2, g2, k_k, k_a, r_k,
      ln_g, ln_b, s_all, acc)


def _hgrn_kernel(pq_ref, pf_ref, pi_ref, pg_ref, lb_ref, ng_ref, s0_ref, acc_ref, o_ref, sout_ref, *scratch,
                 c, seg, tv):
    stacked = seg < c
    if not stacked:
        (s_ref,) = scratch

        @pl.when(pl.program_id(1) == 0)
        def _():
            s_ref[...] = s0_ref[0]

    lb = lb_ref[...]
    hf = _rows(pf_ref, c)
    log_sig = jnp.minimum(hf, 0.0) - jnp.log1p(jnp.exp(-jnp.abs(hf)))
    x1 = jnp.log(jnp.maximum(lb, LB_TINY))
    x2 = jnp.log1p(-lb) + log_sig
    logf = jnp.maximum(x1, x2) + jnp.log1p(jnp.exp(-jnp.abs(x1 - x2)))
    kf = (1.0 - lb) * _sigmoid(-hf)
    q = _silu(_rows(pq_ref, c))
    v = _rows(pi_ref, c)
    if stacked:
        t_in = _token_rows(c, BRANCH_W)
        valid = jnp.logical_and(t_in >= PRE_ROWS, t_in < PRE_ROWS + tv)
        logf = jnp.where(valid, logf, 0.0)
        kf = jnp.where(valid, kf, 0.0)
    sub = seg if stacked else HGRN_SUB
    n_sub = c // sub
    blk_causal, _ = _seg_masks(c, sub)
    b_all = _mm3(blk_causal.astype(F32), logf)
    causal, _ = _seg_masks(sub, sub)
    mid = sub // 2 - 1
    b_last_rows = jnp.concatenate([b_all[(j + 1) * sub - 1:(j + 1) * sub, :] for j in range(n_sub)]
                                  + [jnp.zeros((SUBLANES - n_sub, BRANCH_W), F32)] * (n_sub < SUBLANES),
                                  axis=0)
    decay_cols = jnp.exp(b_last_rows).T
    ng = ng_ref[...]
    gate = _silu(_rows(pg_ref, c))

    heads = range(HGRN_HEADS)
    sls = [slice(h * HGRN_DH, (h + 1) * HGRN_DH) for h in heads]
    pairs = [(j, h) for j in range(n_sub) for h in heads]
    blk = lambda x, j, h: x[j * sub:(j + 1) * sub, sls[h]]
    e_mid, e_last = {}, {}
    for j, h in pairs:
        b = blk(b_all, j, h)
        e_mid[j, h] = b - b[mid:mid + 1, :]
        e_last[j, h] = b[sub - 1:sub, :] - b
    attn = {jh: jnp.where(causal, _mm_nt(blk(q, *jh) * jnp.exp(jnp.minimum(e_mid[jh], EXP_CLAMP)),
                                         blk(kf, *jh) * jnp.exp(jnp.minimum(-e_mid[jh], EXP_CLAMP))), 0.0)
            for jh in pairs}
    intra = {jh: _mm(attn[jh], blk(v, *jh)) for jh in pairs}
    q_dec = {jh: blk(q, *jh) * jnp.exp(blk(b_all, *jh)) for jh in pairs}
    kv = {jh: _mm_tn(blk(kf, *jh) * jnp.exp(e_last[jh]), blk(v, *jh)) for jh in pairs}
    outs = [[] for _ in heads]
    for j in range(n_sub):
        s = [s0_ref[j, h] if stacked else s_ref[h] for h in heads]
        o = [_mm(q_dec[j, h], s[h]) + intra[j, h] for h in heads]
        for h in heads:
            s_new = s[h] * decay_cols[sls[h], j:j + 1] + kv[j, h]
            if stacked:
                sout_ref[j, h] = s_new
            else:
                s_ref[h] = s_new
            outs[h].append(o[h])
    cols = []
    for h in range(HGRN_HEADS):
        sl = slice(h * HGRN_DH, (h + 1) * HGRN_DH)
        o = outs[h][0] if n_sub == 1 else jnp.concatenate(outs[h], axis=0)
        cols.append(_rms(o, NORM_EPS) * ng * gate[:, sl])
    o_ref[...] = jnp.concatenate(cols, axis=1).reshape(o_ref.shape).astype(BF16)

    if not stacked:
        @pl.when(pl.program_id(1) == pl.num_programs(1) - 1)
        def _():
            sout_ref[0] = s_ref[...]


def _hgrn(proj3, lb, ng, s_all, acc, layer, *, stacked, tv):
    n_seq, t, _ = proj3.shape
    grid, c, seg, shape, imap, smap, nblk = _mixer_grid(n_seq, t, stacked)
    ng_ = len(grid)
    kern = functools.partial(_hgrn_kernel, c=c, seg=seg, tv=tv)
    col = lambda blk: pl.BlockSpec(shape(BRANCH_W), imap(blk))
    scratch = [] if stacked else [pltpu.VMEM((HGRN_HEADS, HGRN_DH, HGRN_DH), F32)]
    return pl.pallas_call(
        kern,
        grid=grid,
        in_specs=[col(COL_HQ), col(COL_HF), col(COL_HI), col(COL_HG),
                  _const_spec(lb, ng_), _const_spec(ng, ng_),
                  _layer_state_spec(s_all, layer, nblk, ng_), pl.BlockSpec(memory_space=pl.ANY)],
        out_specs=[col(0), _layer_state_spec(acc, layer, nblk, ng_)],
        out_shape=[jax.ShapeDtypeStruct((n_seq, t, BRANCH_W), BF16),
                   jax.ShapeDtypeStruct(acc.shape, F32)],
        input_output_aliases={7: 1},
        scratch_shapes=scratch,
        compiler_params=_cparams(ng_),
        name="hgrn2",
    )(proj3, proj3, proj3, proj3, lb, ng, s_all, acc)


def _pad_cols(a, width):
    return jnp.pad(a, [(0, 0)] * (a.ndim - 1) + [(0, width - a.shape[-1])])


def _rwkv_cols_to_padded(a):
    o = 3 * BRANCH_W
    return jnp.concatenate([a[..., :o],
                            _pad_cols(a[..., o:o + RWKV_W_LORA], LANES),
                            _pad_cols(a[..., o + RWKV_W_LORA:o + RWKV_W_LORA + RWKV_A_LORA], LANES),
                            a[..., o + RWKV_W_LORA + RWKV_A_LORA:]], axis=-1)


def _rwkv_cols_from_padded(a):
    o = 3 * BRANCH_W
    return jnp.concatenate([a[..., :o], a[..., o:o + RWKV_W_LORA],
                            a[..., o + LANES:o + LANES + RWKV_A_LORA], a[..., o + 2 * LANES:]], axis=-1)


def _layout_w_in(w):
    s5 = _s5_perm(w[..., :512], -1)
    gdn = w[..., 512:2560]
    ba = w[..., 2560:2568]
    rwkv = w[..., 2568:4552]
    hgrn = w[..., 4552:6600]
    rp = _rwkv_cols_to_padded(rwkv)
    out = jnp.concatenate([s5, gdn, rp[..., :1536], hgrn, rp[..., 1792:2048], _pad_cols(ba, LANES),
                           rp[..., 1536:1664], rp[..., 1664:1792]], axis=-1)
    return _pad_cols(out, PROJ_COLS)


def _prep_big(w):
    return dict(
        w_in=_layout_w_in(w['w_in'].astype(BF16)),
        w_gate=w['w_gate'].astype(BF16),
        w_br=w['w_br'].astype(BF16),
        w_br0=_s5_perm(w['w_br'][:, 0], 1).astype(BF16),
        w_o=w['w_o'].astype(BF16),
        w_up=w['w_up'].astype(BF16),
        w_down=w['w_down'].astype(BF16),
    )


def _lane_vec(a, offset):
    return jnp.zeros((1, LANES), F32).at[0, offset:offset + a.shape[0]].set(a.astype(F32))


def _prep_layer(l, w, s5_chunks):
    row = lambda a: a.astype(F32).reshape(1, -1)
    p = dict(
        g_pre_mix=row(w['g_pre_mix'][l]),
        g_post_mix=row(w['g_post_mix'][l]),
        s5_d=_s5_perm(row(w['s5_d'][l]), 1),
        s5_w_glu=_s5_perm(_s5_perm(w['s5_w_glu'][l], 0), 1).astype(BF16),
        gdn_conv_w=w['gdn_conv_w'][l].astype(F32),
        gdn_alog=_lane_vec(w['gdn_a_log'][l], GDN_HEADS),
        gdn_dtb=_lane_vec(w['gdn_dt_bias'][l], GDN_HEADS),
        gdn_norm_g=row(w['gdn_norm_g'][l]),
        rwkv_mu=_rwkv_cols_to_padded(row(w['rwkv_mu'][l])),
        rwkv_w0=row(w['rwkv_w0'][l]),
        rwkv_w2=jnp.pad(w['rwkv_w2'][l], ((0, LANES - RWKV_W_LORA), (0, 0))).astype(BF16),
        rwkv_a0=row(w['rwkv_a0'][l]),
        rwkv_a2=jnp.pad(w['rwkv_a2'][l], ((0, LANES - RWKV_A_LORA), (0, 0))).astype(BF16),
        rwkv_g2=w['rwkv_g2'][l].astype(BF16),
        rwkv_k_k=row(w['rwkv_k_k'][l]),
        rwkv_k_a=row(w['rwkv_k_a'][l]),
        rwkv_r_k=row(w['rwkv_r_k'][l]),
        rwkv_ln_g=row(w['rwkv_ln_g'][l]),
        rwkv_ln_b=row(w['rwkv_ln_b'][l]),
        hgrn_norm_g=row(w['hgrn_norm_g'][l]),
        g_pre_ffn=row(w['g_pre_ffn'][l]),
        ffn_conv_w=w['ffn_conv_w'][l].astype(F32),
        g_post_ffn=row(w['g_post_ffn'][l]),
    )
    s5_args = (w['s5_a_re'][l], w['s5_a_im'][l], w['s5_log_dt'][l], w['s5_b_re'][l], w['s5_b_im'][l],
               w['s5_c_re'][l], w['s5_c_im'][l])
    p['s5'] = {c: _s5_weights(*s5_args, c) for c in s5_chunks}
    return p


def _layer(x, st, acc, p, big, lb, layer, *, n_seq, t, time_major):
    s5_re, s5_im, gdn_s, gdn_buf, rwkv_s, rwkv_prev, hgrn_s, ffn_buf = st
    acc_gdn, acc_rwkv, acc_hgrn = acc
    rwkv_prev = _rwkv_cols_to_padded(rwkv_prev)
    if time_major:
        lo, hi = PRE_ROWS, PRE_ROWS + t
        pad_rows = lambda a, before: jnp.pad(a, ((0, 0), (before, SEQ_ROWS - before - a.shape[1]), (0, 0)))
        to_seq = lambda a: pad_rows(jnp.transpose(a.reshape(t, n_seq, a.shape[-1]), (1, 0, 2)), lo)
        from_seq = lambda a: jnp.transpose(a, (1, 0, 2)).reshape(t * n_seq, a.shape[-1])
        tokens = lambda a: a[:, lo:hi]
        cbuf = pad_rows(gdn_buf, 0)
        prev = pad_rows(rwkv_prev[:, None, :], lo - 1)
    else:
        to_seq = lambda a: a.reshape(n_seq, t, a.shape[-1])
        from_seq = lambda a: a.reshape(n_seq * t, a.shape[-1])
        tokens = lambda a: a
        cbuf = gdn_buf
        prev = rwkv_prev[:, None, :]

    proj = _norm_matmul(x, p['g_pre_mix'], big['w_in'], layer, tn=PROJ_TILE)
    proj3 = to_seq(proj)

    s5c = min(S5_CHUNK, t)
    o_a, s5_re_n, s5_im_n = _s5_branch(tokens(proj3)[..., :BRANCH_W], s5_re, s5_im, p['s5'][s5c],
                                       p['s5_d'], p['s5_w_glu'], c=s5c)
    o_b, gdn_s_n, cout = _gdn(proj3, p['gdn_conv_w'], cbuf, p['gdn_alog'], p['gdn_dtb'],
                              p['gdn_norm_g'], gdn_s, acc_gdn, layer, stacked=time_major, tv=t)
    o_c, rwkv_s_n, pout = _rwkv(proj3, prev, p['rwkv_mu'], p['rwkv_w0'], p['rwkv_w2'], p['rwkv_a0'],
                                p['rwkv_a2'], p['rwkv_g2'], p['rwkv_k_k'], p['rwkv_k_a'],
                                p['rwkv_r_k'], p['rwkv_ln_g'], p['rwkv_ln_b'], rwkv_s, acc_rwkv, layer,
                                stacked=time_major, tv=t)
    o_d, hgrn_s_n = _hgrn(proj3, lb, p['hgrn_norm_g'], hgrn_s, acc_hgrn, layer,
                          stacked=time_major, tv=t)
    if time_major:
        gdn_buf_n = cout[:, hi - (GDN_CONV - 1):hi]
        rwkv_prev_n = _rwkv_cols_from_padded(pout[:, hi - 1])
    else:
        gdn_buf_n = cout
        rwkv_prev_n = _rwkv_cols_from_padded(pout[:, 0])

    branches = [from_seq(o_a)] + [from_seq(tokens(o)) for o in (o_b, o_c, o_d)]
    mix = _mix(x, p['g_pre_mix'], branches, big['w_gate'], big['w_br0'], big['w_br'], layer)
    x = _matmul_resnorm(mix, big['w_o'], layer, x, p['g_post_mix'])

    if time_major:
        buf = jnp.transpose(ffn_buf, (1, 0, 2)).reshape(1, (FFN_CONV - 1) * n_seq, 2 * D_FF)
        x, nbuf = _ffn(x, p['g_pre_ffn'], big['w_up'], p['ffn_conv_w'], buf, big['w_down'],
                       p['g_post_ffn'], layer, stride=n_seq, rows_per_seq=t * n_seq)
        ffn_buf_n = jnp.transpose(nbuf.reshape(FFN_CONV - 1, n_seq, 2 * D_FF), (1, 0, 2))
    else:
        x, ffn_buf_n = _ffn(x, p['g_pre_ffn'], big['w_up'], p['ffn_conv_w'], ffn_buf, big['w_down'],
                            p['g_post_ffn'], layer, stride=1, rows_per_seq=t)
    return x, (s5_re_n, s5_im_n, gdn_s_n, gdn_buf_n, rwkv_s_n, rwkv_prev_n, hgrn_s_n, ffn_buf_n)


MATRIX_STATES = (2, 4, 6)


def _run_group(x3, states, layers, big, lb_all, *, time_major):
    n_seq, t, d = x3.shape
    if time_major:
        x = jnp.transpose(x3, (1, 0, 2)).reshape(t * n_seq, d)
    else:
        x = x3.reshape(n_seq * t, d)
    new = [[] for _ in states]
    acc = [jnp.zeros(states[i].shape, F32) for i in MATRIX_STATES]
    for l in range(DEPTH):
        st = [s if i in MATRIX_STATES else s[l] for i, s in enumerate(states)]
        x, st = _layer(x, st, acc, layers[l], big, lb_all[l:l + 1], l,
                       n_seq=n_seq, t=t, time_major=time_major)
        acc = [st[i] for i in MATRIX_STATES]
        for lst, s in zip(new, st):
            lst.append(s)
    if time_major:
        y = jnp.transpose(x.reshape(t, n_seq, d), (1, 0, 2))
    else:
        y = x.reshape(n_seq, t, d)
    return y, [acc[MATRIX_STATES.index(i)] if i in MATRIX_STATES else jnp.stack(lst)
               for i, lst in enumerate(new)]


def kernel(x_prompt, x_sample, state_s5_re, state_s5_im, state_gdn, state_gdn_conv, state_rwkv, state_rwkv_shift, state_hgrn, state_ffn_conv, g_pre_mix, w_in, w_gate, w_br, w_o, g_post_mix, s5_a_re, s5_a_im, s5_log_dt, s5_b_re, s5_b_im, s5_c_re, s5_c_im, s5_d, s5_w_glu, gdn_conv_w, gdn_a_log, gdn_dt_bias, gdn_norm_g, rwkv_mu, rwkv_w0, rwkv_w2, rwkv_a0, rwkv_a2, rwkv_g2, rwkv_k_k, rwkv_k_a, rwkv_r_k, rwkv_ln_g, rwkv_ln_b, hgrn_lb_logits, hgrn_norm_g, g_pre_ffn, w_up, ffn_conv_w, w_down, g_post_ffn):
    w = dict(g_pre_mix=g_pre_mix, w_in=w_in, w_gate=w_gate, w_br=w_br, w_o=w_o, g_post_mix=g_post_mix,
             s5_a_re=s5_a_re, s5_a_im=s5_a_im, s5_log_dt=s5_log_dt, s5_b_re=s5_b_re, s5_b_im=s5_b_im,
             s5_c_re=s5_c_re, s5_c_im=s5_c_im, s5_d=s5_d, s5_w_glu=s5_w_glu, gdn_conv_w=gdn_conv_w,
             gdn_a_log=gdn_a_log, gdn_dt_bias=gdn_dt_bias, gdn_norm_g=gdn_norm_g, rwkv_mu=rwkv_mu,
             rwkv_w0=rwkv_w0, rwkv_w2=rwkv_w2, rwkv_a0=rwkv_a0, rwkv_a2=rwkv_a2, rwkv_g2=rwkv_g2,
             rwkv_k_k=rwkv_k_k, rwkv_k_a=rwkv_k_a, rwkv_r_k=rwkv_r_k, rwkv_ln_g=rwkv_ln_g,
             rwkv_ln_b=rwkv_ln_b, hgrn_norm_g=hgrn_norm_g, g_pre_ffn=g_pre_ffn, w_up=w_up,
             ffn_conv_w=ffn_conv_w, w_down=w_down, g_post_ffn=g_post_ffn)
    sm = jax.nn.softmax(hgrn_lb_logits.astype(F32), axis=0)
    lb_all = jnp.maximum(jnp.cumsum(sm, axis=0) - sm[0], 0.0)
    s5_chunks = {min(S5_CHUNK, x_prompt.shape[1]), min(S5_CHUNK, x_sample.shape[1])}
    layers = [_prep_layer(l, w, s5_chunks) for l in range(DEPTH)]
    big = _prep_big(w)
    sample_states = (state_s5_re, state_s5_im, state_gdn, state_gdn_conv, state_rwkv,
                     state_rwkv_shift, state_hgrn, state_ffn_conv)
    nb = x_prompt.shape[0]
    prompt_states = tuple(jnp.zeros((DEPTH, nb) + s.shape[2:], F32) for s in sample_states)
    y_prompt, ps = _run_group(x_prompt, prompt_states, layers, big, lb_all, time_major=False)
    y_sample, ss = _run_group(x_sample, sample_states, layers, big, lb_all, time_major=True)
    out = [y_prompt, y_sample]
    for a, b in zip(ps, ss):
        out.extend((a, b))
    return tuple(out)
```

```python
import functools
import math

import jax
import jax.numpy as jnp
from jax import lax
from jax.experimental import pallas as pl
from jax.experimental.pallas import tpu as pltpu

F32 = jnp.float32
BF16 = jnp.bfloat16

D_MODEL = 2048
DEPTH = 4
N_BRANCH = 4
BRANCH_W = 512
NORM_EPS = 1e-6

S5_GROUP = 16
S5_GROUPS = 32
S5_STATE = 64

GDN_HEADS = 4
GDN_DH = 128
GDN_CONV = 4

RWKV_DH = 64
RWKV_HEADS = 8
RWKV_W_LORA = 96
RWKV_A_LORA = 96
RWKV_G_LORA = 256
RWKV_LN_EPS = 64e-5
RWKV_COLS = 3 * BRANCH_W + RWKV_W_LORA + RWKV_A_LORA + RWKV_G_LORA

HGRN_HEADS = 4
HGRN_DH = 128
HGRN_SUB = 16
LB_TINY = 1e-30
EXP_CLAMP = 80.0

D_FF = 5632
FFN_CONV = 3

LANES = 128
SUBLANES = 8
VMEM_LIMIT_BYTES = 56 * 1024 * 1024

PROJ_COLS = 6912
PROJ_TILE = 2304
COL_S5U, COL_GQ, COL_GK, COL_GV, COL_GZ = 0, 1, 2, 3, 4
COL_RR, COL_RK, COL_RV = 5, 6, 7
COL_HQ, COL_HF, COL_HI, COL_HG = 8, 9, 10, 11
COL_RGL = 6144 // 256
COL_GBA, COL_RWL, COL_RAL = 6400 // 128, 6528 // 128, 6656 // 128
RWKV_PAD_COLS = 2048

ROW_TILE = 512
PROMPT_CHUNK = 128
SEQ_ROWS = SUBLANES
PRE_ROWS = GDN_CONV - 1
SEQ_BLOCK = 8
S5_CHUNK = 16
INV_BLOCK = 64


def _s5_perm(a, axis):
    axis %= a.ndim
    shp = a.shape
    a = a.reshape(shp[:axis] + (S5_GROUPS, S5_GROUP) + shp[axis + 1:])
    return jnp.swapaxes(a, axis, axis + 1).reshape(shp)


def _cparams(n_axes):
    return pltpu.CompilerParams(dimension_semantics=("arbitrary",) * n_axes,
                                vmem_limit_bytes=VMEM_LIMIT_BYTES)


def _mm(a, b):
    return jnp.dot(a.astype(BF16), b.astype(BF16), preferred_element_type=F32)


def _mm_nt(a, b):
    return lax.dot_general(a.astype(BF16), b.astype(BF16), (((1,), (1,)), ((), ())),
                           preferred_element_type=F32)


def _mm_tn(a, b):
    return lax.dot_general(a.astype(BF16), b.astype(BF16), (((0,), (0,)), ((), ())),
                           preferred_element_type=F32)


def _split(a):
    hi = a.astype(BF16)
    lo = (a - hi.astype(F32)).astype(BF16)
    return hi, lo


def _mm3(a, b):
    ah, al = _split(a)
    bh, bl = _split(b)
    d = functools.partial(jnp.dot, preferred_element_type=F32)
    return d(ah, bh) + (d(ah, bl) + d(al, bh))


def _mm2(a, b):
    bh, bl = _split(b)
    d = functools.partial(jnp.dot, preferred_element_type=F32)
    return d(a, bh) + d(a, bl)


def _rms(x, eps):
    return x * lax.rsqrt(jnp.mean(x * x, axis=-1, keepdims=True) + eps)


def _sigmoid(x):
    return 1.0 / (1.0 + jnp.exp(-x))


def _silu(x):
    return x * _sigmoid(x)


def _softplus(x):
    return jnp.maximum(x, 0.0) + jnp.log1p(jnp.exp(-jnp.abs(x)))


def _gelu_tanh(x):
    return 0.5 * x * (1.0 + jnp.tanh(math.sqrt(2.0 / math.pi) * (x + 0.044715 * (x * x * x))))


def _seg_masks(c, seg, reps=1):
    r = lax.broadcasted_iota(jnp.int32, (c, reps * c), 0)
    s = lax.broadcasted_iota(jnp.int32, (c, reps * c), 1)
    if reps > 1:
        s = s % c
    causal, strict = r >= s, r > s
    if seg < c:
        same = (r // seg) == (s // seg)
        causal, strict = jnp.logical_and(causal, same), jnp.logical_and(strict, same)
    return causal, strict


def _unit_lower_inverses(mats, seg):
    c = mats[0].shape[0]
    r = lax.broadcasted_iota(jnp.int32, (c, c), 0)
    s = lax.broadcasted_iota(jnp.int32, (c, c), 1)
    eye = (r == s).astype(F32)
    if seg > INV_BLOCK:
        assert seg == 2 * INV_BLOCK
        diag = (r // INV_BLOCK) == (s // INV_BLOCK)
        tb = _unit_lower_inverses([jnp.where(diag, a, 0.0) for a in mats], INV_BLOCK)
        off = [jnp.where(diag, 0.0, a) for a in mats]
        left = [_mm(t, o) for t, o in zip(tb, off)]
        return [t - _mm(x, t) for t, x in zip(tb, left)]
    ps = [-a for a in mats]
    ts = [eye + p for p in ps]
    k = 2
    while k < seg:
        ps = [_mm(p, p) for p in ps]
        ts = [t + _mm(t, p) for t, p in zip(ts, ps)]
        k *= 2
    return ts


def _unit_lower_inverse(a, seg):
    return _unit_lower_inverses([a], seg)[0]


def _rows(ref, c):
    return ref[0] if ref.shape[0] == 1 else ref[...].reshape(c, ref.shape[-1])


def _token_rows(c, width):
    return lax.broadcasted_iota(jnp.int32, (c, width), 0) % SEQ_ROWS


def _norm_matmul_kernel(x_ref, g_ref, w_ref, o_ref, h_ref):
    @pl.when(pl.program_id(1) == 0)
    def _():
        h_ref[...] = (_rms(x_ref[...], NORM_EPS) * g_ref[...]).astype(BF16)

    o_ref[...] = jnp.dot(h_ref[...], w_ref[...], preferred_element_type=F32)


def _norm_matmul(x, g, w_all, layer, *, tn):
    rows, d = x.shape
    n = w_all.shape[2]
    tm = min(ROW_TILE, rows)
    return pl.pallas_call(
        _norm_matmul_kernel,
        grid=(rows // tm, n // tn),
        in_specs=[pl.BlockSpec((tm, d), lambda i, j: (i, 0)),
                  pl.BlockSpec((1, d), lambda i, j: (0, 0)),
                  pl.BlockSpec((None, d, tn), lambda i, j: (layer, 0, j))],
        out_specs=pl.BlockSpec((tm, tn), lambda i, j: (i, j)),
        out_shape=jax.ShapeDtypeStruct((rows, n), F32),
        scratch_shapes=[pltpu.VMEM((tm, d), BF16)],
        compiler_params=_cparams(2),
        name="norm_matmul",
    )(x, g, w_all)


def _mix_kernel(x_ref, g_ref, oa_ref, ob_ref, oc_ref, od_ref, wga_ref, wgb_ref, wgc_ref, wgd_ref,
                wbr0_ref, wbr_ref, o_ref, h_ref):
    @pl.when(pl.program_id(1) == 0)
    def _():
        h_ref[...] = (_rms(x_ref[...], NORM_EPS) * g_ref[...]).astype(BF16)

    h = h_ref[...]
    acc = None
    for n, (br_ref, wg_ref) in enumerate(((oa_ref, wga_ref), (ob_ref, wgb_ref), (oc_ref, wgc_ref),
                                          (od_ref, wgd_ref))):
        gate = _sigmoid(jnp.dot(h, wg_ref[...], preferred_element_type=F32))
        w_n = wbr0_ref[...] if n == 0 else wbr_ref[n]
        term = gate * jnp.dot(br_ref[...], w_n, preferred_element_type=F32)
        acc = term if acc is None else acc + term
    o_ref[...] = acc.astype(BF16)


def _mix(x, g, branches, wg, wbr0, wbr, layer, *, tn=512):
    rows, d = x.shape
    tm = min(ROW_TILE, rows)
    bw = branches[0].shape[1]
    br_spec = pl.BlockSpec((tm, bw), lambda i, j: (i, 0))
    nt = d // tn
    wg_specs = [pl.BlockSpec((None, d, tn), functools.partial(lambda i, j, n: (layer, 0, n * nt + j), n=n))
                for n in range(N_BRANCH)]
    return pl.pallas_call(
        _mix_kernel,
        grid=(rows // tm, nt),
        in_specs=[pl.BlockSpec((tm, d), lambda i, j: (i, 0)),
                  pl.BlockSpec((1, d), lambda i, j: (0, 0)),
                  br_spec, br_spec, br_spec, br_spec, *wg_specs,
                  pl.BlockSpec((None, bw, tn), lambda i, j: (layer, 0, j)),
                  pl.BlockSpec((None, N_BRANCH, bw, tn), lambda i, j: (layer, 0, 0, j))],
        out_specs=pl.BlockSpec((tm, tn), lambda i, j: (i, j)),
        out_shape=jax.ShapeDtypeStruct((rows, d), BF16),
        scratch_shapes=[pltpu.VMEM((tm, d), BF16)],
        compiler_params=_cparams(2),
        name="branch_mix",
    )(x, g, *branches, wg, wg, wg, wg, wbr0, wbr)


def _matmul_resnorm_kernel(a_ref, w_ref, x_ref, g_ref, o_ref):
    y = jnp.dot(a_ref[...], w_ref[...], preferred_element_type=F32)
    o_ref[...] = x_ref[...] + _rms(y, NORM_EPS) * g_ref[...]


def _matmul_resnorm(a, w_all, layer, x, g):
    rows, kdim = a.shape
    d = w_all.shape[2]
    tm = min(ROW_TILE, rows)
    return pl.pallas_call(
        _matmul_resnorm_kernel,
        grid=(rows // tm,),
        in_specs=[pl.BlockSpec((tm, kdim), lambda i: (i, 0)),
                  pl.BlockSpec((None, kdim, d), lambda i: (layer, 0, 0)),
                  pl.BlockSpec((tm, d), lambda i: (i, 0)),
                  pl.BlockSpec((1, d), lambda i: (0, 0))],
        out_specs=pl.BlockSpec((tm, d), lambda i: (i, 0)),
        out_shape=jax.ShapeDtypeStruct((rows, d), F32),
        compiler_params=_cparams(1),
        name="matmul_resnorm",
    )(a, w_all, x, g)


def _ffn_kernel(x_ref, gpre_ref, wa_ref, wb_ref, cwa_ref, cwb_ref, bufa_ref, bufb_ref, wd_ref,
                gpost_ref, o_ref, nbufa_ref, nbufb_ref, h_ref, acc_ref, carry_ref,
                *, tm, stride, tiles_per_seq):
    i = pl.program_id(0)
    f = pl.program_id(1)
    keep = (FFN_CONV - 1) * stride
    pad = max(SUBLANES, keep)

    @pl.when(f == 0)
    def _():
        h_ref[...] = (_rms(x_ref[...], NORM_EPS) * gpre_ref[...]).astype(BF16)
        acc_ref[...] = jnp.zeros_like(acc_ref)
        if tiles_per_seq > 1:
            @pl.when(i == 0)
            def _():
                carry_ref[...] = jnp.zeros_like(carry_ref)

    h = h_ref[...]
    row = lax.broadcasted_iota(jnp.int32, (tm, wa_ref.shape[-1]), 0)
    halves = []
    for idx, (w_ref, cw_ref, buf_ref, nbuf_ref) in enumerate(
            ((wa_ref, cwa_ref, bufa_ref, nbufa_ref), (wb_ref, cwb_ref, bufb_ref, nbufb_ref))):
        if tiles_per_seq == 1:
            hist = buf_ref[0]
        else:
            hist = jnp.where((i % tiles_per_seq) == 0, buf_ref[0], carry_ref[f, idx])
        up = jnp.dot(h, w_ref[...], preferred_element_type=F32)
        cw = cw_ref[...]
        y = cw[FFN_CONV - 1:FFN_CONV, :] * up
        for k in range(1, FFN_CONV):
            shifted = pltpu.roll(up, k * stride, 0)
            if stride == 1:
                for r in range(k):
                    shifted = jnp.where(row == r, hist[keep - k + r:keep - k + r + 1, :], shifted)
            else:
                head = hist[keep - k * stride:, :]
                head = jnp.concatenate([head] + [jnp.zeros((tm - k * stride, head.shape[1]), F32)], axis=0)
                shifted = jnp.where(row < k * stride, head, shifted)
            y = y + cw[FFN_CONV - 1 - k:FFN_CONV - k, :] * shifted
        last = up[tm - keep:, :]
        nbuf_ref[0] = last
        if tiles_per_seq > 1:
            carry_ref[f, idx] = last
        halves.append(y)

    act = (_gelu_tanh(halves[0]) * halves[1]).astype(BF16)
    acc_ref[...] += jnp.dot(act, wd_ref[...], preferred_element_type=F32)

    @pl.when(f == pl.num_programs(1) - 1)
    def _():
        o_ref[...] = x_ref[...] + _rms(acc_ref[...], NORM_EPS) * gpost_ref[...]


def _ffn(x, gpre, w_up, conv_w, buf, w_down, gpost, layer, *, stride, rows_per_seq, tf=512):
    rows, d = x.shape
    tm = min(ROW_TILE, rows_per_seq)
    tiles_per_seq = rows_per_seq // tm
    nf = D_FF // tf
    keep = (FFN_CONV - 1) * stride
    pad = max(SUBLANES, keep)
    carry_rows = keep if tiles_per_seq > 1 else SUBLANES
    kern = functools.partial(_ffn_kernel, tm=tm, stride=stride, tiles_per_seq=tiles_per_seq)
    buf_a = pl.BlockSpec((1, keep, tf), lambda i, f: (i // tiles_per_seq, 0, f))
    buf_b = pl.BlockSpec((1, keep, tf), lambda i, f: (i // tiles_per_seq, 0, nf + f))
    y, nbuf_a, nbuf_b = pl.pallas_call(
        kern,
        grid=(rows // tm, nf),
        in_specs=[pl.BlockSpec((tm, d), lambda i, f: (i, 0)),
                  pl.BlockSpec((1, d), lambda i, f: (0, 0)),
                  pl.BlockSpec((None, d, tf), lambda i, f: (layer, 0, f)),
                  pl.BlockSpec((None, d, tf), lambda i, f: (layer, 0, nf + f)),
                  pl.BlockSpec((FFN_CONV, tf), lambda i, f: (0, f)),
                  pl.BlockSpec((FFN_CONV, tf), lambda i, f: (0, nf + f)),
                  buf_a, buf_b,
                  pl.BlockSpec((None, tf, d), lambda i, f: (layer, f, 0)),
                  pl.BlockSpec((1, d), lambda i, f: (0, 0))],
        out_specs=[pl.BlockSpec((tm, d), lambda i, f: (i, 0)),
                   pl.BlockSpec((1, keep, tf), lambda i, f: (i, 0, f)),
                   pl.BlockSpec((1, keep, tf), lambda i, f: (i, 0, f))],
        out_shape=[jax.ShapeDtypeStruct((rows, d), F32),
                   jax.ShapeDtypeStruct((rows // tm, keep, D_FF), F32),
                   jax.ShapeDtypeStruct((rows // tm, keep, D_FF), F32)],
        scratch_shapes=[pltpu.VMEM((tm, d), BF16),
                        pltpu.VMEM((tm, d), F32),
                        pltpu.VMEM((nf, 2, carry_rows, tf), F32)],
        compiler_params=_cparams(2),
        name="conv_ffn",
    )(x, gpre, w_up, w_up, conv_w, conv_w, buf, buf, w_down, gpost)
    nbuf = jnp.concatenate([nbuf_a, nbuf_b], axis=-1)
    return y, nbuf[tiles_per_seq - 1::tiles_per_seq]


def _s5_kernel(u_ref, h0_ref, bst_ref, kst_ref, cst_ref, apow_ref, y_ref, hfin_ref, e_ref,
               *, n_seq, n_chunks):
    rows = n_seq * n_chunks
    u = u_ref[0]
    h0 = h0_ref[0]
    half = S5_STATE
    if n_chunks > 1:
        row = lax.broadcasted_iota(jnp.int32, (rows, 2 * half), 0)
        placed = jnp.zeros((rows, 2 * half), F32)
        for b in range(n_seq):
            placed = jnp.where(row == b * n_chunks, h0[b:b + 1, :], placed)
        h0 = placed

    def cmul(k, x):
        return x * apow_ref[0, 2 * k:2 * k + 1, :] + pltpu.roll(x, half, 1) * apow_ref[0, 2 * k + 1:2 * k + 2, :]

    e = _mm2(u, bst_ref[0]) + cmul(0, h0)
    if n_chunks > 1:
        j = lax.broadcasted_iota(jnp.int32, (rows, 2 * half), 0) % n_chunks
        k, sh = 0, 1
        while sh < n_chunks:
            e = e + jnp.where(j >= sh, cmul(k, pltpu.roll(e, sh, 0)), 0.0)
            k, sh = k + 1, sh * 2
        e_ref[...] = e
        hfin_ref[0] = e_ref[pl.ds(n_chunks - 1, n_seq, stride=n_chunks), :]
        h_start = jnp.where(j >= 1, pltpu.roll(e, 1, 0), 0.0) + h0
    else:
        hfin_ref[0] = e
        h_start = h0
    y_ref[0] = _mm2(u, kst_ref[0]) + _mm3(h_start, cst_ref[0])


def _s5_scan(u_g, h0_g, bst, kst, cst, apow, *, n_seq, n_chunks):
    groups, rows, cw = u_g.shape
    kern = functools.partial(_s5_kernel, n_seq=n_seq, n_chunks=n_chunks)
    spec3 = lambda a: pl.BlockSpec((1,) + a.shape[1:], lambda g: (g, 0, 0))
    return pl.pallas_call(
        kern,
        grid=(groups,),
        in_specs=[spec3(u_g), spec3(h0_g), spec3(bst), spec3(kst), spec3(cst), spec3(apow)],
        out_specs=[pl.BlockSpec((1, rows, cw), lambda g: (g, 0, 0)),
                   pl.BlockSpec((1, n_seq, 2 * S5_STATE), lambda g: (g, 0, 0))],
        out_shape=[jax.ShapeDtypeStruct((groups, rows, cw), F32),
                   jax.ShapeDtypeStruct((groups, n_seq, 2 * S5_STATE), F32)],
        scratch_shapes=[pltpu.VMEM((rows, 2 * S5_STATE), F32)],
        compiler_params=_cparams(1),
        name="s5_scan",
    )(u_g, h0_g, bst, kst, cst, apow)


def _s5_glu_kernel(y_ref, u_ref, d_ref, w_ref, o_ref):
    g = _gelu_tanh(y_ref[...] + d_ref[...] * u_ref[...])
    o_ref[...] = (g * _sigmoid(_mm(g, w_ref[...]))).astype(BF16)


def _s5_glu(y, u, d, w_glu):
    rows, bw = y.shape
    tm = min(ROW_TILE, rows)
    row_spec = pl.BlockSpec((tm, bw), lambda i: (i, 0))
    return pl.pallas_call(
        _s5_glu_kernel,
        grid=(rows // tm,),
        in_specs=[row_spec, row_spec,
                  pl.BlockSpec((1, bw), lambda i: (0, 0)),
                  pl.BlockSpec((bw, bw), lambda i: (0, 0))],
        out_specs=row_spec,
        out_shape=jax.ShapeDtypeStruct((rows, bw), BF16),
        compiler_params=_cparams(1),
        name="s5_glu",
    )(y, u, d, w_glu)


def _s5_weights(a_re, a_im, log_dt, b_re, b_im, c_re, c_im, c):
    lam = lax.complex(a_re.astype(F32), a_im.astype(F32))
    ldt = lam * jnp.exp(log_dt.astype(F32))[:, None]
    a_bar = jnp.exp(ldt)
    b_bar = ((a_bar - 1.0) / lam)[..., None] * lax.complex(b_re.astype(F32), b_im.astype(F32))
    cm = lax.complex(c_re.astype(F32), c_im.astype(F32))
    tau = jnp.arange(c + 1, dtype=F32)
    apw = jnp.exp(ldt[None] * tau[:, None, None])
    bst = apw[:c][::-1][:, :, :, None] * b_bar[None]
    bst = jnp.transpose(bst, (1, 0, 3, 2)).reshape(S5_GROUPS, c * S5_GROUP, S5_STATE)
    bst = jnp.concatenate([jnp.real(bst), jnp.imag(bst)], axis=-1)
    kt = jnp.real(jnp.einsum('ghp,tgp,gpk->tgkh', cm, apw[:c], b_bar))
    s_idx = jnp.arange(c)[:, None, None]
    t_idx = jnp.arange(c)[None, :, None]
    lag = (t_idx - s_idx == jnp.arange(c)[None, None, :]).astype(F32)
    kst = jnp.einsum('stu,ugkh->stgkh', lag, kt, precision=lax.Precision.HIGHEST)
    kst = jnp.transpose(kst, (2, 0, 3, 1, 4)).reshape(S5_GROUPS, c * S5_GROUP, c * S5_GROUP)
    ca = cm[None] * apw[1:c + 1][:, :, None, :]
    ca = jnp.transpose(ca, (1, 3, 0, 2)).reshape(S5_GROUPS, S5_STATE, c * S5_GROUP)
    cst = jnp.concatenate([jnp.real(ca), -jnp.imag(ca)], axis=1)
    rows = []
    k = 0
    while True:
        p = jnp.exp(ldt * float(c * 2 ** k))
        rows.append(jnp.concatenate([jnp.real(p), jnp.real(p)], axis=-1))
        rows.append(jnp.concatenate([-jnp.imag(p), jnp.imag(p)], axis=-1))
        k += 1
        if c * 2 ** k > 4096:
            break
    apow = jnp.stack(rows, axis=1)
    return bst, kst, cst, apow


def _s5_branch(u3, h0_re, h0_im, sw, d_skip, w_glu, *, c):
    n_seq, t, _ = u3.shape
    n_chunks = t // c
    bst, kst, cst, apow = sw
    u2 = u3.reshape(n_seq * t, BRANCH_W)
    u_g = u2.astype(BF16).reshape(n_seq * t * S5_GROUP, S5_GROUPS).T
    u_g = u_g.reshape(S5_GROUPS, n_seq * n_chunks, c * S5_GROUP)
    h0 = jnp.concatenate([h0_re, h0_im], axis=-1)
    h0 = jnp.transpose(h0, (1, 0, 2))
    y_g, hfin = _s5_scan(u_g, h0, bst, kst, cst, apow, n_seq=n_seq, n_chunks=n_chunks)
    y = y_g.reshape(S5_GROUPS, n_seq * t * S5_GROUP).T.reshape(n_seq * t, BRANCH_W)
    o = _s5_glu(y, u2, d_skip, w_glu)
    hfin = jnp.transpose(hfin, (1, 0, 2))
    return o.reshape(n_seq, t, BRANCH_W), hfin[..., :S5_STATE], hfin[..., S5_STATE:]


def _mixer_grid(n_seq, t, stacked):
    if stacked:
        assert t == SEQ_ROWS and n_seq % SEQ_BLOCK == 0
        c = SEQ_BLOCK * SEQ_ROWS
        shape = lambda w: (SEQ_BLOCK, SEQ_ROWS, w)
        imap = lambda blk: (lambda b: (b, 0, blk))
        smap = lambda nd: (lambda b: (b,) + (0,) * (nd - 1))
        return (n_seq // SEQ_BLOCK,), c, SEQ_ROWS, shape, imap, smap, SEQ_BLOCK
    assert t % PROMPT_CHUNK == 0
    c = PROMPT_CHUNK
    shape = lambda w: (1, c, w)
    imap = lambda blk: (lambda b, i: (b, i, blk))
    smap = lambda nd: (lambda b, i: (b,) + (0,) * (nd - 1))
    return (n_seq, t // c), c, c, shape, imap, smap, 1


def _layer_state_spec(s_all, layer, nblk, n_grid):
    tail = (0,) * (s_all.ndim - 2)
    imap = (lambda b: (layer, b) + tail) if n_grid == 1 else (lambda b, i: (layer, b) + tail)
    return pl.BlockSpec((None, nblk) + s_all.shape[2:], imap)


def _const_spec(a, n_grid):
    zeros = (0,) * a.ndim
    return pl.BlockSpec(a.shape, (lambda b: zeros) if n_grid == 1 else (lambda b, i: zeros))


def _gdn_kernel(pq_ref, pk_ref, pv_ref, pz_ref, pba_ref, cw_ref, cbuf_ref, alog_ref, dtb_ref,
                ng_ref, s0_ref, acc_ref, o_ref, sout_ref, cout_ref, *scratch, c, seg, tv):
    stacked = seg < c
    keep = GDN_CONV - 1
    x = jnp.concatenate([_rows(pq_ref, c), _rows(pk_ref, c), _rows(pv_ref, c)], axis=1)
    if stacked:
        (conv_ref,) = scratch
        t_in = _token_rows(c, 3 * BRANCH_W)
        x = jnp.where(t_in < keep, _rows(cbuf_ref, c), x)
        conv_ref[0:SUBLANES, :] = jnp.zeros((SUBLANES, 3 * BRANCH_W), F32)
        cout_ref[...] = x.reshape(cout_ref.shape)
    else:
        s_ref, conv_ref = scratch

        @pl.when(pl.program_id(1) == 0)
        def _():
            s_ref[...] = s0_ref[0]
            conv_ref[SUBLANES - keep:SUBLANES, :] = cbuf_ref[0]

    conv_ref[SUBLANES:SUBLANES + c, :] = x
    cw = cw_ref[...]
    y = None
    for j in range(GDN_CONV):
        off = SUBLANES - keep + j
        term = cw[j:j + 1, :] * conv_ref[off:off + c, :]
        y = term if y is None else y + term
    if not stacked:
        last = conv_ref[SUBLANES + c - keep:SUBLANES + c, :]
        conv_ref[SUBLANES - keep:SUBLANES, :] = last
        cout_ref[0] = last
    act = _silu(y)

    ba = _rows(pba_ref, c)
    beta_all = _sigmoid(ba)
    g_all = -jnp.exp(alog_ref[...]) * _softplus(ba + dtb_ref[...])
    if stacked:
        t_in = _token_rows(c, LANES)
        valid = jnp.logical_and(t_in >= PRE_ROWS, t_in < PRE_ROWS + tv)
        beta_all = jnp.where(valid, beta_all, 0.0)
        g_all = jnp.where(valid, g_all, 0.0)
    causal, strict = _seg_masks(c, seg)
    d_all = _mm3(causal.astype(F32), g_all)
    dt_all = d_all.T
    ng = ng_ref[...]
    z = _rows(pz_ref, c)
    n_seg = c // seg

    heads = range(GDN_HEADS)
    sls = [slice(h * GDN_DH, (h + 1) * GDN_DH) for h in heads]
    segs = [slice(j * seg, (j + 1) * seg) for j in range(n_seg)]
    cat0 = lambda xs: xs[0] if len(xs) == 1 else jnp.concatenate(xs, axis=0)
    state = lambda j, h: s0_ref[j, h] if stacked else s_ref[h]
    q = [act[:, sl] for sl in sls]
    k = [act[:, BRANCH_W + h * GDN_DH:BRANCH_W + (h + 1) * GDN_DH] for h in heads]
    v = [act[:, 2 * BRANCH_W + h * GDN_DH:2 * BRANCH_W + (h + 1) * GDN_DH] for h in heads]
    q = [x * lax.rsqrt(jnp.sum(x * x, axis=-1, keepdims=True) + 1e-6) * (GDN_DH ** -0.5) for x in q]
    k = [x * lax.rsqrt(jnp.sum(x * x, axis=-1, keepdims=True) + 1e-6) for x in k]
    beta = [beta_all[:, h:h + 1] for h in heads]
    d = [d_all[:, GDN_HEADS + h:GDN_HEADS + h + 1] for h in heads]
    d_row = [dt_all[GDN_HEADS + h:GDN_HEADS + h + 1, :] for h in heads]
    decay = [jnp.where(causal, jnp.exp(jnp.where(causal, d[h] - d_row[h], 0.0)), 0.0) for h in heads]
    kb = [k[h] * beta[h] for h in heads]
    prod = [_mm_nt(jnp.concatenate([kb[h], q[h]], axis=0), k[h]) for h in heads]
    m = [jnp.where(strict, prod[h][:c] * decay[h], 0.0) for h in heads]
    attn = [prod[h][c:] * decay[h] for h in heads]
    t_inv = _unit_lower_inverses(m, seg)
    ed = [jnp.exp(x) for x in d]
    sol = [_mm(t_inv[h], jnp.concatenate([v[h] * beta[h], kb[h] * ed[h]], axis=1)) for h in heads]
    qe = [q[h] * ed[h] for h in heads]
    both = [[_mm(jnp.concatenate([sol[h][rows, GDN_DH:], qe[h][rows]], axis=0), state(j, h))
             for j, rows in enumerate(segs)] for h in heads]
    u = [cat0([sol[h][rows, :GDN_DH] - both[h][j][:seg] for j, rows in enumerate(segs)]) for h in heads]
    qs = [cat0([b[seg:] for b in both[h]]) for h in heads]
    o = [qs[h] + _mm(attn[h], u[h]) for h in heads]
    for h in heads:
        for j, rows in enumerate(segs):
            dl = d[h][(j + 1) * seg - 1:(j + 1) * seg, :]
            s_new = state(j, h) * jnp.exp(dl) + _mm_tn(k[h][rows] * jnp.exp(dl - d[h][rows]), u[h][rows])
            if stacked:
                sout_ref[j, h] = s_new
            else:
                s_ref[h] = s_new
    outs = [_rms(o[h], NORM_EPS) * ng * _silu(z[:, sls[h]]) for h in heads]
    o_ref[...] = jnp.concatenate(outs, axis=1).reshape(o_ref.shape).astype(BF16)

    if not stacked:
        @pl.when(pl.program_id(1) == pl.num_programs(1) - 1)
        def _():
            sout_ref[0] = s_ref[...]


def _gdn(proj3, conv_w, cbuf, alog, dtb, ng, s_all, acc, layer, *, stacked, tv):
    n_seq, t, _ = proj3.shape
    grid, c, seg, shape, imap, smap, nblk = _mixer_grid(n_seq, t, stacked)
    ng_ = len(grid)
    kern = functools.partial(_gdn_kernel, c=c, seg=seg, tv=tv)
    col = lambda blk, w: pl.BlockSpec(shape(w), imap(blk))
    cbuf_spec = pl.BlockSpec((nblk,) + cbuf.shape[1:], smap(3))
    scratch = [pltpu.VMEM((SUBLANES + c, 3 * BRANCH_W), F32)]
    if not stacked:
        scratch = [pltpu.VMEM((GDN_HEADS, GDN_DH, GDN_DH), F32)] + scratch
    return pl.pallas_call(
        kern,
        grid=grid,
        in_specs=[col(COL_GQ, BRANCH_W), col(COL_GK, BRANCH_W), col(COL_GV, BRANCH_W),
                  col(COL_GZ, BRANCH_W), col(COL_GBA, LANES),
                  _const_spec(conv_w, ng_), cbuf_spec, _const_spec(alog, ng_), _const_spec(dtb, ng_),
                  _const_spec(ng, ng_), _layer_state_spec(s_all, layer, nblk, ng_),
                  pl.BlockSpec(memory_space=pl.ANY)],
        out_specs=[col(0, BRANCH_W), _layer_state_spec(acc, layer, nblk, ng_), cbuf_spec],
        out_shape=[jax.ShapeDtypeStruct((n_seq, t, BRANCH_W), BF16),
                   jax.ShapeDtypeStruct(acc.shape, F32),
                   jax.ShapeDtypeStruct(cbuf.shape, F32)],
        input_output_aliases={11: 1},
        scratch_shapes=scratch,
        compiler_params=_cparams(ng_),
        name="gdn",
    )(proj3, proj3, proj3, proj3, proj3, conv_w, cbuf, alog, dtb, ng, s_all, acc)


def _rwkv_kernel(pr_ref, pk_ref, pv_ref, pwl_ref, pal_ref, pgl_ref, prev_ref, mu_ref, w0_ref, w2_ref,
                 a0_ref, a2_ref, g2_ref, kk_ref, ka_ref, rk_ref, lng_ref, lnb_ref, s0_ref, acc_ref,
                 o_ref, sout_ref, pout_ref, *scratch, c, seg, tv):
    stacked = seg < c
    p = jnp.concatenate([_rows(pr_ref, c), _rows(pk_ref, c), _rows(pv_ref, c), _rows(pwl_ref, c),
                         _rows(pal_ref, c), _rows(pgl_ref, c)], axis=1)
    if stacked:
        (p_ref,) = scratch
        t_in = _token_rows(c, RWKV_PAD_COLS)
        p = jnp.where(t_in == PRE_ROWS - 1, _rows(prev_ref, c), p)
        p_ref[0:SUBLANES, :] = jnp.zeros((SUBLANES, RWKV_PAD_COLS), F32)
        pout_ref[...] = p.reshape(pout_ref.shape)
    else:
        s_ref, p_ref = scratch

        @pl.when(pl.program_id(1) == 0)
        def _():
            s_ref[...] = s0_ref[0]
            p_ref[SUBLANES - 1:SUBLANES, :] = prev_ref[0]

    p_ref[SUBLANES:SUBLANES + c, :] = p
    prev = p_ref[SUBLANES - 1:SUBLANES - 1 + c, :]
    if not stacked:
        last = p_ref[SUBLANES + c - 1:SUBLANES + c, :]
        p_ref[SUBLANES - 1:SUBLANES, :] = last
        pout_ref[0] = last
    pm = p + (prev - p) * mu_ref[...]
    r = pm[:, :BRANCH_W]
    k = pm[:, BRANCH_W:2 * BRANCH_W]
    v = pm[:, 2 * BRANCH_W:3 * BRANCH_W]
    wl = pm[:, 3 * BRANCH_W:3 * BRANCH_W + LANES]
    al = pm[:, 3 * BRANCH_W + LANES:3 * BRANCH_W + 2 * LANES]
    gl = pm[:, 3 * BRANCH_W + 2 * LANES:]
    w_log = -_softplus(-(w0_ref[...] + _mm(jnp.tanh(wl), w2_ref[...]))) - 0.5
    lw = -jnp.exp(w_log)
    a = _sigmoid(a0_ref[...] + _mm(al, a2_ref[...]))
    gate = _mm(_sigmoid(gl), g2_ref[...])
    kk_un = k * kk_ref[...]
    k2 = k * (1.0 + (a - 1.0) * ka_ref[...])
    if stacked:
        t_in = _token_rows(c, BRANCH_W)
        valid = jnp.logical_and(t_in >= PRE_ROWS, t_in < PRE_ROWS + tv)
        lw = jnp.where(valid, lw, 0.0)
        kk_un = jnp.where(valid, kk_un, 0.0)
        k2 = jnp.where(valid, k2, 0.0)
    causal, _ = _seg_masks(c, seg)
    causal2, strict2 = _seg_masks(c, seg, reps=2)
    g_cum = _mm3(causal.astype(F32), lw)
    e_pos = jnp.exp(g_cum)
    e_neg = jnp.exp(-g_cum)
    e_prev = jnp.exp(g_cum - lw)
    rk_w = rk_ref[...]
    lng = lng_ref[...]
    lnb = lnb_ref[...]
    n_seg = c // seg

    heads = range(RWKV_HEADS)
    sls = [slice(h * RWKV_DH, (h + 1) * RWKV_DH) for h in heads]
    segs = [slice(j * seg, (j + 1) * seg) for j in range(n_seg)]
    cat0 = lambda xs: xs[0] if len(xs) == 1 else jnp.concatenate(xs, axis=0)
    state = lambda j, h: s0_ref[j, h] if stacked else s_ref[h]
    kkh = [kk_un[:, sl] for sl in sls]
    kkh = [x * lax.rsqrt(jnp.sum(x * x, axis=-1, keepdims=True) + 1e-6) for x in kkh]
    alpha_hat = [kkh[h] * a[:, sls[h]] * e_neg[:, sls[h]] for h in heads]
    k_hat = [k2[:, sl] * e_neg[:, sl] for sl in sls]
    kap = [kkh[h] * e_prev[:, sls[h]] for h in heads]
    r_t = [r[:, sl] * e_pos[:, sl] for sl in sls]
    vh = [v[:, sl] for sl in sls]
    rhs = [jnp.concatenate([alpha_hat[h], k_hat[h]], axis=0) for h in heads]
    ab = [jnp.where(strict2, _mm_nt(kap[h], rhs[h]), 0.0) for h in heads]
    rr = [jnp.where(causal2, _mm_nt(r_t[h], rhs[h]), 0.0) for h in heads]
    both = [[_mm_nt(jnp.concatenate([kap[h][rows], r_t[h][rows]], axis=0), state(j, h))
             for j, rows in enumerate(segs)] for h in heads]
    bv = [_mm(jnp.concatenate([ab[h][:, c:], rr[h][:, c:]], axis=0), vh[h]) for h in heads]
    t_inv = _unit_lower_inverses([ab[h][:, :c] for h in heads], seg)
    ks = [cat0([b[:seg] for b in both[h]]) for h in heads]
    rs = [cat0([b[seg:] for b in both[h]]) for h in heads]
    u = [_mm(t_inv[h], ks[h] + bv[h][:c]) for h in heads]
    y = [rs[h] + bv[h][c:] - _mm(rr[h][:, :c], u[h]) for h in heads]
    for h in heads:
        for j, rows in enumerate(segs):
            el = e_pos[(j + 1) * seg - 1:(j + 1) * seg, sls[h]]
            s_new = state(j, h) * el + _mm_tn(
                jnp.concatenate([vh[h][rows], -u[h][rows]], axis=0),
                jnp.concatenate([k_hat[h][rows] * el, alpha_hat[h][rows] * el], axis=0))
            if stacked:
                sout_ref[j, h] = s_new
            else:
                s_ref[h] = s_new
    outs = []
    for h in heads:
        sl = sls[h]
        mu = jnp.mean(y[h], axis=-1, keepdims=True)
        yc = y[h] - mu
        var = jnp.mean(yc * yc, axis=-1, keepdims=True)
        yn = yc * lax.rsqrt(var + RWKV_LN_EPS) * lng[:, sl] + lnb[:, sl]
        bonus = jnp.sum(r[:, sl] * k2[:, sl] * rk_w[:, sl], axis=-1, keepdims=True) * vh[h]
        outs.append(yn + bonus)
    o_ref[...] = (jnp.concatenate(outs, axis=1) * gate).reshape(o_ref.shape).astype(BF16)

    if not stacked:
        @pl.when(pl.program_id(1) == pl.num_programs(1) - 1)
        def _():
            sout_ref[0] = s_ref[...]


def _rwkv(proj3, prev, mu, w0, w2, a0, a2, g2, k_k, k_a, r_k, ln_g, ln_b, s_all, acc, layer, *,
          stacked, tv):
    n_seq, t, _ = proj3.shape
    grid, c, seg, shape, imap, smap, nblk = _mixer_grid(n_seq, t, stacked)
    ng_ = len(grid)
    kern = functools.partial(_rwkv_kernel, c=c, seg=seg, tv=tv)
    col = lambda blk, w: pl.BlockSpec(shape(w), imap(blk))
    cs = lambda a: _const_spec(a, ng_)
    prev_spec = pl.BlockSpec((nblk,) + prev.shape[1:], smap(3))
    scratch = [pltpu.VMEM((SUBLANES + c, RWKV_PAD_COLS), F32)]
    if not stacked:
        scratch = [pltpu.VMEM((RWKV_HEADS, RWKV_DH, RWKV_DH), F32)] + scratch
    return pl.pallas_call(
        kern,
        grid=grid,
        in_specs=[col(COL_RR, BRANCH_W), col(COL_RK, BRANCH_W), col(COL_RV, BRANCH_W),
                  col(COL_RWL, LANES), col(COL_RAL, LANES), col(COL_RGL, 2 * LANES),
                  prev_spec, cs(mu), cs(w0), cs(w2), cs(a0), cs(a2), cs(g2),
                  cs(k_k), cs(k_a), cs(r_k), cs(ln_g), cs(ln_b),
                  _layer_state_spec(s_all, layer, nblk, ng_), pl.BlockSpec(memory_space=pl.ANY)],
        out_specs=[col(0, BRANCH_W), _layer_state_spec(acc, layer, nblk, ng_), prev_spec],
        out_shape=[jax.ShapeDtypeStruct((n_seq, t, BRANCH_W), BF16),
                   jax.ShapeDtypeStruct(acc.shape, F32),
                   jax.ShapeDtypeStruct(prev.shape, F32)],
        input_output_aliases={19: 1},
        scratch_shapes=scratch,
        compiler_params=_cparams(ng_),
        name="rwkv7",
    )(proj3, proj3, proj3, proj3, proj3, proj3, prev, mu, w0, w2, a0, a2, g2, k_k, k_a, r_k,
      ln_g, ln_b, s_all, acc)


def _hgrn_kernel(pq_ref, pf_ref, pi_ref, pg_ref, lb_ref, ng_ref, s0_ref, acc_ref, o_ref, sout_ref, *scratch,
                 c, seg, tv):
    stacked = seg < c
    if not stacked:
        (s_ref,) = scratch

        @pl.when(pl.program_id(1) == 0)
        def _():
            s_ref[...] = s0_ref[0]

    lb = lb_ref[...]
    hf = _rows(pf_ref, c)
    log_sig = jnp.minimum(hf, 0.0) - jnp.log1p(jnp.exp(-jnp.abs(hf)))
    x1 = jnp.log(jnp.maximum(lb, LB_TINY))
    x2 = jnp.log1p(-lb) + log_sig
    logf = jnp.maximum(x1, x2) + jnp.log1p(jnp.exp(-jnp.abs(x1 - x2)))
    kf = (1.0 - lb) * _sigmoid(-hf)
    q = _silu(_rows(pq_ref, c))
    v = _rows(pi_ref, c)
    if stacked:
        t_in = _token_rows(c, BRANCH_W)
        valid = jnp.logical_and(t_in >= PRE_ROWS, t_in < PRE_ROWS + tv)
        logf = jnp.where(valid, logf, 0.0)
        kf = jnp.where(valid, kf, 0.0)
    sub = seg if stacked else HGRN_SUB
    n_sub = c // sub
    blk_causal, _ = _seg_masks(c, sub)
    b_all = _mm3(blk_causal.astype(F32), logf)
    causal, _ = _seg_masks(sub, sub)
    mid = sub // 2 - 1
    b_last_rows = jnp.concatenate([b_all[(j + 1) * sub - 1:(j + 1) * sub, :] for j in range(n_sub)]
                                  + [jnp.zeros((SUBLANES - n_sub, BRANCH_W), F32)] * (n_sub < SUBLANES),
                                  axis=0)
    decay_cols = jnp.exp(b_last_rows).T
    ng = ng_ref[...]
    gate = _silu(_rows(pg_ref, c))

    heads = range(HGRN_HEADS)
    sls = [slice(h * HGRN_DH, (h + 1) * HGRN_DH) for h in heads]
    pairs = [(j, h) for j in range(n_sub) for h in heads]
    blk = lambda x, j, h: x[j * sub:(j + 1) * sub, sls[h]]
    e_mid, e_last = {}, {}
    for j, h in pairs:
        b = blk(b_all, j, h)
        e_mid[j, h] = b - b[mid:mid + 1, :]
        e_last[j, h] = b[sub - 1:sub, :] - b
    attn = {jh: jnp.where(causal, _mm_nt(blk(q, *jh) * jnp.exp(jnp.minimum(e_mid[jh], EXP_CLAMP)),
                                         blk(kf, *jh) * jnp.exp(jnp.minimum(-e_mid[jh], EXP_CLAMP))), 0.0)
            for jh in pairs}
    intra = {jh: _mm(attn[jh], blk(v, *jh)) for jh in pairs}
    q_dec = {jh: blk(q, *jh) * jnp.exp(blk(b_all, *jh)) for jh in pairs}
    kv = {jh: _mm_tn(blk(kf, *jh) * jnp.exp(e_last[jh]), blk(v, *jh)) for jh in pairs}
    outs = [[] for _ in heads]
    for j in range(n_sub):
        s = [s0_ref[j, h] if stacked else s_ref[h] for h in heads]
        o = [_mm(q_dec[j, h], s[h]) + intra[j, h] for h in heads]
        for h in heads:
            s_new = s[h] * decay_cols[sls[h], j:j + 1] + kv[j, h]
            if stacked:
                sout_ref[j, h] = s_new
            else:
                s_ref[h] = s_new
            outs[h].append(o[h])
    cols = []
    for h in range(HGRN_HEADS):
        sl = slice(h * HGRN_DH, (h + 1) * HGRN_DH)
        o = outs[h][0] if n_sub == 1 else jnp.concatenate(outs[h], axis=0)
        cols.append(_rms(o, NORM_EPS) * ng * gate[:, sl])
    o_ref[...] = jnp.concatenate(cols, axis=1).reshape(o_ref.shape).astype(BF16)

    if not stacked:
        @pl.when(pl.program_id(1) == pl.num_programs(1) - 1)
        def _():
            sout_ref[0] = s_ref[...]


def _hgrn(proj3, lb, ng, s_all, acc, layer, *, stacked, tv):
    n_seq, t, _ = proj3.shape
    grid, c, seg, shape, imap, smap, nblk = _mixer_grid(n_seq, t, stacked)
    ng_ = len(grid)
    kern = functools.partial(_hgrn_kernel, c=c, seg=seg, tv=tv)
    col = lambda blk: pl.BlockSpec(shape(BRANCH_W), imap(blk))
    scratch = [] if stacked else [pltpu.VMEM((HGRN_HEADS, HGRN_DH, HGRN_DH), F32)]
    return pl.pallas_call(
        kern,
        grid=grid,
        in_specs=[col(COL_HQ), col(COL_HF), col(COL_HI), col(COL_HG),
                  _const_spec(lb, ng_), _const_spec(ng, ng_),
                  _layer_state_spec(s_all, layer, nblk, ng_), pl.BlockSpec(memory_space=pl.ANY)],
        out_specs=[col(0), _layer_state_spec(acc, layer, nblk, ng_)],
        out_shape=[jax.ShapeDtypeStruct((n_seq, t, BRANCH_W), BF16),
                   jax.ShapeDtypeStruct(acc.shape, F32)],
        input_output_aliases={7: 1},
        scratch_shapes=scratch,
        compiler_params=_cparams(ng_),
        name="hgrn2",
    )(proj3, proj3, proj3, proj3, lb, ng, s_all, acc)


def _pad_cols(a, width):
    return jnp.pad(a, [(0, 0)] * (a.ndim - 1) + [(0, width - a.shape[-1])])


def _rwkv_cols_to_padded(a):
    o = 3 * BRANCH_W
    return jnp.concatenate([a[..., :o],
                            _pad_cols(a[..., o:o + RWKV_W_LORA], LANES),
                            _pad_cols(a[..., o + RWKV_W_LORA:o + RWKV_W_LORA + RWKV_A_LORA], LANES),
                            a[..., o + RWKV_W_LORA + RWKV_A_LORA:]], axis=-1)


def _rwkv_cols_from_padded(a):
    o = 3 * BRANCH_W
    return jnp.concatenate([a[..., :o], a[..., o:o + RWKV_W_LORA],
                            a[..., o + LANES:o + LANES + RWKV_A_LORA], a[..., o + 2 * LANES:]], axis=-1)


def _layout_w_in(w):
    s5 = _s5_perm(w[..., :512], -1)
    gdn = w[..., 512:2560]
    ba = w[..., 2560:2568]
    rwkv = w[..., 2568:4552]
    hgrn = w[..., 4552:6600]
    rp = _rwkv_cols_to_padded(rwkv)
    out = jnp.concatenate([s5, gdn, rp[..., :1536], hgrn, rp[..., 1792:2048], _pad_cols(ba, LANES),
                           rp[..., 1536:1664], rp[..., 1664:1792]], axis=-1)
    return _pad_cols(out, PROJ_COLS)


def _prep_big(w):
    return dict(
        w_in=_layout_w_in(w['w_in'].astype(BF16)),
        w_gate=w['w_gate'].astype(BF16),
        w_br=w['w_br'].astype(BF16),
        w_br0=_s5_perm(w['w_br'][:, 0], 1).astype(BF16),
        w_o=w['w_o'].astype(BF16),
        w_up=w['w_up'].astype(BF16),
        w_down=w['w_down'].astype(BF16),
    )


def _lane_vec(a, offset):
    return jnp.zeros((1, LANES), F32).at[0, offset:offset + a.shape[0]].set(a.astype(F32))


def _prep_layer(l, w, s5_chunks):
    row = lambda a: a.astype(F32).reshape(1, -1)
    p = dict(
        g_pre_mix=row(w['g_pre_mix'][l]),
        g_post_mix=row(w['g_post_mix'][l]),
        s5_d=_s5_perm(row(w['s5_d'][l]), 1),
        s5_w_glu=_s5_perm(_s5_perm(w['s5_w_glu'][l], 0), 1).astype(BF16),
        gdn_conv_w=w['gdn_conv_w'][l].astype(F32),
        gdn_alog=_lane_vec(w['gdn_a_log'][l], GDN_HEADS),
        gdn_dtb=_lane_vec(w['gdn_dt_bias'][l], GDN_HEADS),
        gdn_norm_g=row(w['gdn_norm_g'][l]),
        rwkv_mu=_rwkv_cols_to_padded(row(w['rwkv_mu'][l])),
        rwkv_w0=row(w['rwkv_w0'][l]),
        rwkv_w2=jnp.pad(w['rwkv_w2'][l], ((0, LANES - RWKV_W_LORA), (0, 0))).astype(BF16),
        rwkv_a0=row(w['rwkv_a0'][l]),
        rwkv_a2=jnp.pad(w['rwkv_a2'][l], ((0, LANES - RWKV_A_LORA), (0, 0))).astype(BF16),
        rwkv_g2=w['rwkv_g2'][l].astype(BF16),
        rwkv_k_k=row(w['rwkv_k_k'][l]),
        rwkv_k_a=row(w['rwkv_k_a'][l]),
        rwkv_r_k=row(w['rwkv_r_k'][l]),
        rwkv_ln_g=row(w['rwkv_ln_g'][l]),
        rwkv_ln_b=row(w['rwkv_ln_b'][l]),
        hgrn_norm_g=row(w['hgrn_norm_g'][l]),
        g_pre_ffn=row(w['g_pre_ffn'][l]),
        ffn_conv_w=w['ffn_conv_w'][l].astype(F32),
        g_post_ffn=row(w['g_post_ffn'][l]),
    )
    s5_args = (w['s5_a_re'][l], w['s5_a_im'][l], w['s5_log_dt'][l], w['s5_b_re'][l], w['s5_b_im'][l],
               w['s5_c_re'][l], w['s5_c_im'][l])
    p['s5'] = {c: _s5_weights(*s5_args, c) for c in s5_chunks}
    return p


def _layer(x, st, acc, p, big, lb, layer, *, n_seq, t, time_major):
    s5_re, s5_im, gdn_s, gdn_buf, rwkv_s, rwkv_prev, hgrn_s, ffn_buf = st
    acc_gdn, acc_rwkv, acc_hgrn = acc
    rwkv_prev = _rwkv_cols_to_padded(rwkv_prev)
    if time_major:
        lo, hi = PRE_ROWS, PRE_ROWS + t
        pad_rows = lambda a, before: jnp.pad(a, ((0, 0), (before, SEQ_ROWS - before - a.shape[1]), (0, 0)))
        to_seq = lambda a: pad_rows(jnp.transpose(a.reshape(t, n_seq, a.shape[-1]), (1, 0, 2)), lo)
        from_seq = lambda a: jnp.transpose(a, (1, 0, 2)).reshape(t * n_seq, a.shape[-1])
        tokens = lambda a: a[:, lo:hi]
        cbuf = pad_rows(gdn_buf, 0)
        prev = pad_rows(rwkv_prev[:, None, :], lo - 1)
    else:
        to_seq = lambda a: a.reshape(n_seq, t, a.shape[-1])
        from_seq = lambda a: a.reshape(n_seq * t, a.shape[-1])
        tokens = lambda a: a
        cbuf = gdn_buf
        prev = rwkv_prev[:, None, :]

    proj = _norm_matmul(x, p['g_pre_mix'], big['w_in'], layer, tn=PROJ_TILE)
    proj3 = to_seq(proj)

    s5c = min(S5_CHUNK, t)
    o_a, s5_re_n, s5_im_n = _s5_branch(tokens(proj3)[..., :BRANCH_W], s5_re, s5_im, p['s5'][s5c],
                                       p['s5_d'], p['s5_w_glu'], c=s5c)
    o_b, gdn_s_n, cout = _gdn(proj3, p['gdn_conv_w'], cbuf, p['gdn_alog'], p['gdn_dtb'],
                              p['gdn_norm_g'], gdn_s, acc_gdn, layer, stacked=time_major, tv=t)
    o_c, rwkv_s_n, pout = _rwkv(proj3, prev, p['rwkv_mu'], p['rwkv_w0'], p['rwkv_w2'], p['rwkv_a0'],
                                p['rwkv_a2'], p['rwkv_g2'], p['rwkv_k_k'], p['rwkv_k_a'],
                                p['rwkv_r_k'], p['rwkv_ln_g'], p['rwkv_ln_b'], rwkv_s, acc_rwkv, layer,
                                stacked=time_major, tv=t)
    o_d, hgrn_s_n = _hgrn(proj3, lb, p['hgrn_norm_g'], hgrn_s, acc_hgrn, layer,
                          stacked=time_major, tv=t)
    if time_major:
        gdn_buf_n = cout[:, hi - (GDN_CONV - 1):hi]
        rwkv_prev_n = _rwkv_cols_from_padded(pout[:, hi - 1])
    else:
        gdn_buf_n = cout
        rwkv_prev_n = _rwkv_cols_from_padded(pout[:, 0])

    branches = [from_seq(o_a)] + [from_seq(tokens(o)) for o in (o_b, o_c, o_d)]
    mix = _mix(x, p['g_pre_mix'], branches, big['w_gate'], big['w_br0'], big['w_br'], layer)
    x = _matmul_resnorm(mix, big['w_o'], layer, x, p['g_post_mix'])

    if time_major:
        buf = jnp.transpose(ffn_buf, (1, 0, 2)).reshape(1, (FFN_CONV - 1) * n_seq, 2 * D_FF)
        x, nbuf = _ffn(x, p['g_pre_ffn'], big['w_up'], p['ffn_conv_w'], buf, big['w_down'],
                       p['g_post_ffn'], layer, stride=n_seq, rows_per_seq=t * n_seq)
        ffn_buf_n = jnp.transpose(nbuf.reshape(FFN_CONV - 1, n_seq, 2 * D_FF), (1, 0, 2))
    else:
        x, ffn_buf_n = _ffn(x, p['g_pre_ffn'], big['w_up'], p['ffn_conv_w'], ffn_buf, big['w_down'],
                            p['g_post_ffn'], layer, stride=1, rows_per_seq=t)
    return x, (s5_re_n, s5_im_n, gdn_s_n, gdn_buf_n, rwkv_s_n, rwkv_prev_n, hgrn_s_n, ffn_buf_n)


MATRIX_STATES = (2, 4, 6)


def _run_group(x3, states, layers, big, lb_all, *, time_major):
    n_seq, t, d = x3.shape
    if time_major:
        x = jnp.transpose(x3, (1, 0, 2)).reshape(t * n_seq, d)
    else:
        x = x3.reshape(n_seq * t, d)
    new = [[] for _ in states]
    acc = [jnp.zeros(states[i].shape, F32) for i in MATRIX_STATES]
    for l in range(DEPTH):
        st = [s if i in MATRIX_STATES else s[l] for i, s in enumerate(states)]
        x, st = _layer(x, st, acc, layers[l], big, lb_all[l:l + 1], l,
                       n_seq=n_seq, t=t, time_major=time_major)
        acc = [st[i] for i in MATRIX_STATES]
        for lst, s in zip(new, st):
            lst.append(s)
    if time_major:
        y = jnp.transpose(x.reshape(t, n_seq, d), (1, 0, 2))
    else:
        y = x.reshape(n_seq, t, d)
    return y, [acc[MATRIX_STATES.index(i)] if i in MATRIX_STATES else jnp.stack(lst)
               for i, lst in enumerate(new)]


def kernel(x_prompt, x_sample, state_s5_re, state_s5_im, state_gdn, state_gdn_conv, state_rwkv, state_rwkv_shift, state_hgrn, state_ffn_conv, g_pre_mix, w_in, w_gate, w_br, w_o, g_post_mix, s5_a_re, s5_a_im, s5_log_dt, s5_b_re, s5_b_im, s5_c_re, s5_c_im, s5_d, s5_w_glu, gdn_conv_w, gdn_a_log, gdn_dt_bias, gdn_norm_g, rwkv_mu, rwkv_w0, rwkv_w2, rwkv_a0, rwkv_a2, rwkv_g2, rwkv_k_k, rwkv_k_a, rwkv_r_k, rwkv_ln_g, rwkv_ln_b, hgrn_lb_logits, hgrn_norm_g, g_pre_ffn, w_up, ffn_conv_w, w_down, g_post_ffn):
    w = dict(g_pre_mix=g_pre_mix, w_in=w_in, w_gate=w_gate, w_br=w_br, w_o=w_o, g_post_mix=g_post_mix,
             s5_a_re=s5_a_re, s5_a_im=s5_a_im, s5_log_dt=s5_log_dt, s5_b_re=s5_b_re, s5_b_im=s5_b_im,
             s5_c_re=s5_c_re, s5_c_im=s5_c_im, s5_d=s5_d, s5_w_glu=s5_w_glu, gdn_conv_w=gdn_conv_w,
             gdn_a_log=gdn_a_log, gdn_dt_bias=gdn_dt_bias, gdn_norm_g=gdn_norm_g, rwkv_mu=rwkv_mu,
             rwkv_w0=rwkv_w0, rwkv_w2=rwkv_w2, rwkv_a0=rwkv_a0, rwkv_a2=rwkv_a2, rwkv_g2=rwkv_g2,
             rwkv_k_k=rwkv_k_k, rwkv_k_a=rwkv_k_a, rwkv_r_k=rwkv_r_k, rwkv_ln_g=rwkv_ln_g,
             rwkv_ln_b=rwkv_ln_b, hgrn_norm_g=hgrn_norm_g, g_pre_ffn=g_pre_ffn, w_up=w_up,
             ffn_conv_w=ffn_conv_w, w_down=w_down, g_post_ffn=g_post_ffn)
    sm = jax.nn.softmax(hgrn_lb_logits.astype(F32), axis=0)
    lb_all = jnp.maximum(jnp.cumsum(sm, axis=0) - sm[0], 0.0)
    s5_chunks = {min(S5_CHUNK, x_prompt.shape[1]), min(S5_CHUNK, x_sample.shape[1])}
    layers = [_prep_layer(l, w, s5_chunks) for l in range(DEPTH)]
    big = _prep_big(w)
    sample_states = (state_s5_re, state_s5_im, state_gdn, state_gdn_conv, state_rwkv,
                     state_rwkv_shift, state_hgrn, state_ffn_conv)
    nb = x_prompt.shape[0]
    prompt_states = tuple(jnp.zeros((DEPTH, nb) + s.shape[2:], F32) for s in sample_states)
    y_prompt, ps = _run_group(x_prompt, prompt_states, layers, big, lb_all, time_major=False)
    y_sample, ss = _run_group(x_sample, sample_states, layers, big, lb_all, time_major=True)
    out = [y_prompt, y_sample]
    for a, b in zip(ps, ss):
        out.extend((a, b))
    return tuple(out)
```

```python
import functools
import math

import jax
import jax.numpy as jnp
from jax import lax
from jax.experimental import pallas as pl
from jax.experimental.pallas import tpu as pltpu

F32 = jnp.float32
BF16 = jnp.bfloat16

D_MODEL = 2048
DEPTH = 4
N_BRANCH = 4
BRANCH_W = 512
NORM_EPS = 1e-6

S5_GROUP = 16
S5_GROUPS = 32
S5_STATE = 64

GDN_HEADS = 4
GDN_DH = 128
GDN_CONV = 4

RWKV_DH = 64
RWKV_HEADS = 8
RWKV_W_LORA = 96
RWKV_A_LORA = 96
RWKV_G_LORA = 256
RWKV_LN_EPS = 64e-5
RWKV_COLS = 3 * BRANCH_W + RWKV_W_LORA + RWKV_A_LORA + RWKV_G_LORA

HGRN_HEADS = 4
HGRN_DH = 128
HGRN_SUB = 16
LB_TINY = 1e-30
EXP_CLAMP = 80.0

D_FF = 5632
FFN_CONV = 3

LANES = 128
SUBLANES = 8
VMEM_LIMIT_BYTES = 56 * 1024 * 1024

PROJ_COLS = 6912
PROJ_TILE = 2304
COL_S5U, COL_GQ, COL_GK, COL_GV, COL_GZ = 0, 1, 2, 3, 4
COL_RR, COL_RK, COL_RV = 5, 6, 7
COL_HQ, COL_HF, COL_HI, COL_HG = 8, 9, 10, 11
COL_RGL = 6144 // 256
COL_GBA, COL_RWL, COL_RAL = 6400 // 128, 6528 // 128, 6656 // 128
RWKV_PAD_COLS = 2048

ROW_TILE = 512
PROMPT_CHUNK = 128
SEQ_ROWS = SUBLANES
PRE_ROWS = GDN_CONV - 1
SEQ_BLOCK = 16
S5_CHUNK = 16
INV_BLOCK = 64


def _s5_perm(a, axis):
    axis %= a.ndim
    shp = a.shape
    a = a.reshape(shp[:axis] + (S5_GROUPS, S5_GROUP) + shp[axis + 1:])
    return jnp.swapaxes(a, axis, axis + 1).reshape(shp)


def _cparams(n_axes):
    return pltpu.CompilerParams(dimension_semantics=("arbitrary",) * n_axes,
                                vmem_limit_bytes=VMEM_LIMIT_BYTES)


def _mm(a, b):
    return jnp.dot(a.astype(BF16), b.astype(BF16), preferred_element_type=F32)


def _mm_nt(a, b):
    return lax.dot_general(a.astype(BF16), b.astype(BF16), (((1,), (1,)), ((), ())),
                           preferred_element_type=F32)


def _mm_tn(a, b):
    return lax.dot_general(a.astype(BF16), b.astype(BF16), (((0,), (0,)), ((), ())),
                           preferred_element_type=F32)


def _split(a):
    hi = a.astype(BF16)
    lo = (a - hi.astype(F32)).astype(BF16)
    return hi, lo


def _mm3(a, b):
    ah, al = _split(a)
    bh, bl = _split(b)
    d = functools.partial(jnp.dot, preferred_element_type=F32)
    return d(ah, bh) + (d(ah, bl) + d(al, bh))


def _mm2(a, b):
    bh, bl = _split(b)
    d = functools.partial(jnp.dot, preferred_element_type=F32)
    return d(a, bh) + d(a, bl)


def _rms(x, eps):
    return x * lax.rsqrt(jnp.mean(x * x, axis=-1, keepdims=True) + eps)


def _sigmoid(x):
    return 1.0 / (1.0 + jnp.exp(-x))


def _silu(x):
    return x * _sigmoid(x)


def _softplus(x):
    return jnp.maximum(x, 0.0) + jnp.log1p(jnp.exp(-jnp.abs(x)))


def _gelu_tanh(x):
    return 0.5 * x * (1.0 + jnp.tanh(math.sqrt(2.0 / math.pi) * (x + 0.044715 * (x * x * x))))


def _seg_masks(c, seg, reps=1):
    r = lax.broadcasted_iota(jnp.int32, (c, reps * c), 0)
    s = lax.broadcasted_iota(jnp.int32, (c, reps * c), 1)
    if reps > 1:
        s = s % c
    causal, strict = r >= s, r > s
    if seg < c:
        same = (r // seg) == (s // seg)
        causal, strict = jnp.logical_and(causal, same), jnp.logical_and(strict, same)
    return causal, strict


def _unit_lower_inverses(mats, seg):
    c = mats[0].shape[0]
    r = lax.broadcasted_iota(jnp.int32, (c, c), 0)
    s = lax.broadcasted_iota(jnp.int32, (c, c), 1)
    eye = (r == s).astype(F32)
    if seg > INV_BLOCK:
        assert seg == 2 * INV_BLOCK
        diag = (r // INV_BLOCK) == (s // INV_BLOCK)
        tb = _unit_lower_inverses([jnp.where(diag, a, 0.0) for a in mats], INV_BLOCK)
        off = [jnp.where(diag, 0.0, a) for a in mats]
        left = [_mm(t, o) for t, o in zip(tb, off)]
        return [t - _mm(x, t) for t, x in zip(tb, left)]
    ps = [-a for a in mats]
    ts = [eye + p for p in ps]
    k = 2
    while k < seg:
        ps = [_mm(p, p) for p in ps]
        ts = [t + _mm(t, p) for t, p in zip(ts, ps)]
        k *= 2
    return ts


def _unit_lower_inverse(a, seg):
    return _unit_lower_inverses([a], seg)[0]


def _rows(ref, c):
    return ref[0] if ref.shape[0] == 1 else ref[...].reshape(c, ref.shape[-1])


def _token_rows(c, width):
    return lax.broadcasted_iota(jnp.int32, (c, width), 0) % SEQ_ROWS


def _norm_matmul_kernel(x_ref, g_ref, w_ref, o_ref, h_ref):
    @pl.when(pl.program_id(1) == 0)
    def _():
        h_ref[...] = (_rms(x_ref[...], NORM_EPS) * g_ref[...]).astype(BF16)

    o_ref[...] = jnp.dot(h_ref[...], w_ref[...], preferred_element_type=F32)


def _norm_matmul(x, g, w_all, layer, *, tn):
    rows, d = x.shape
    n = w_all.shape[2]
    tm = min(ROW_TILE, rows)
    return pl.pallas_call(
        _norm_matmul_kernel,
        grid=(rows // tm, n // tn),
        in_specs=[pl.BlockSpec((tm, d), lambda i, j: (i, 0)),
                  pl.BlockSpec((1, d), lambda i, j: (0, 0)),
                  pl.BlockSpec((None, d, tn), lambda i, j: (layer, 0, j))],
        out_specs=pl.BlockSpec((tm, tn), lambda i, j: (i, j)),
        out_shape=jax.ShapeDtypeStruct((rows, n), F32),
        scratch_shapes=[pltpu.VMEM((tm, d), BF16)],
        compiler_params=_cparams(2),
        name="norm_matmul",
    )(x, g, w_all)


def _mix_kernel(x_ref, g_ref, oa_ref, ob_ref, oc_ref, od_ref, wga_ref, wgb_ref, wgc_ref, wgd_ref,
                wbr0_ref, wbr_ref, o_ref, h_ref):
    @pl.when(pl.program_id(1) == 0)
    def _():
        h_ref[...] = (_rms(x_ref[...], NORM_EPS) * g_ref[...]).astype(BF16)

    h = h_ref[...]
    acc = None
    for n, (br_ref, wg_ref) in enumerate(((oa_ref, wga_ref), (ob_ref, wgb_ref), (oc_ref, wgc_ref),
                                          (od_ref, wgd_ref))):
        gate = _sigmoid(jnp.dot(h, wg_ref[...], preferred_element_type=F32))
        w_n = wbr0_ref[...] if n == 0 else wbr_ref[n]
        term = gate * jnp.dot(br_ref[...], w_n, preferred_element_type=F32)
        acc = term if acc is None else acc + term
    o_ref[...] = acc.astype(BF16)


def _mix(x, g, branches, wg, wbr0, wbr, layer, *, tn=512):
    rows, d = x.shape
    tm = min(ROW_TILE, rows)
    bw = branches[0].shape[1]
    br_spec = pl.BlockSpec((tm, bw), lambda i, j: (i, 0))
    nt = d // tn
    wg_specs = [pl.BlockSpec((None, d, tn), functools.partial(lambda i, j, n: (layer, 0, n * nt + j), n=n))
                for n in range(N_BRANCH)]
    return pl.pallas_call(
        _mix_kernel,
        grid=(rows // tm, nt),
        in_specs=[pl.BlockSpec((tm, d), lambda i, j: (i, 0)),
                  pl.BlockSpec((1, d), lambda i, j: (0, 0)),
                  br_spec, br_spec, br_spec, br_spec, *wg_specs,
                  pl.BlockSpec((None, bw, tn), lambda i, j: (layer, 0, j)),
                  pl.BlockSpec((None, N_BRANCH, bw, tn), lambda i, j: (layer, 0, 0, j))],
        out_specs=pl.BlockSpec((tm, tn), lambda i, j: (i, j)),
        out_shape=jax.ShapeDtypeStruct((rows, d), BF16),
        scratch_shapes=[pltpu.VMEM((tm, d), BF16)],
        compiler_params=_cparams(2),
        name="branch_mix",
    )(x, g, *branches, wg, wg, wg, wg, wbr0, wbr)


def _matmul_resnorm_kernel(a_ref, w_ref, x_ref, g_ref, o_ref):
    y = jnp.dot(a_ref[...], w_ref[...], preferred_element_type=F32)
    o_ref[...] = x_ref[...] + _rms(y, NORM_EPS) * g_ref[...]


def _matmul_resnorm(a, w_all, layer, x, g):
    rows, kdim = a.shape
    d = w_all.shape[2]
    tm = min(ROW_TILE, rows)
    return pl.pallas_call(
        _matmul_resnorm_kernel,
        grid=(rows // tm,),
        in_specs=[pl.BlockSpec((tm, kdim), lambda i: (i, 0)),
                  pl.BlockSpec((None, kdim, d), lambda i: (layer, 0, 0)),
                  pl.BlockSpec((tm, d), lambda i: (i, 0)),
                  pl.BlockSpec((1, d), lambda i: (0, 0))],
        out_specs=pl.BlockSpec((tm, d), lambda i: (i, 0)),
        out_shape=jax.ShapeDtypeStruct((rows, d), F32),
        compiler_params=_cparams(1),
        name="matmul_resnorm",
    )(a, w_all, x, g)


def _ffn_kernel(x_ref, gpre_ref, wa_ref, wb_ref, cwa_ref, cwb_ref, bufa_ref, bufb_ref, wd_ref,
                gpost_ref, o_ref, nbufa_ref, nbufb_ref, h_ref, acc_ref, carry_ref,
                *, tm, stride, tiles_per_seq):
    i = pl.program_id(0)
    f = pl.program_id(1)
    keep = (FFN_CONV - 1) * stride
    pad = max(SUBLANES, keep)

    @pl.when(f == 0)
    def _():
        h_ref[...] = (_rms(x_ref[...], NORM_EPS) * gpre_ref[...]).astype(BF16)
        acc_ref[...] = jnp.zeros_like(acc_ref)
        if tiles_per_seq > 1:
            @pl.when(i == 0)
            def _():
                carry_ref[...] = jnp.zeros_like(carry_ref)

    h = h_ref[...]
    row = lax.broadcasted_iota(jnp.int32, (tm, wa_ref.shape[-1]), 0)
    halves = []
    for idx, (w_ref, cw_ref, buf_ref, nbuf_ref) in enumerate(
            ((wa_ref, cwa_ref, bufa_ref, nbufa_ref), (wb_ref, cwb_ref, bufb_ref, nbufb_ref))):
        if tiles_per_seq == 1:
            hist = buf_ref[0]
        else:
            hist = jnp.where((i % tiles_per_seq) == 0, buf_ref[0], carry_ref[f, idx])
        up = jnp.dot(h, w_ref[...], preferred_element_type=F32)
        cw = cw_ref[...]
        y = cw[FFN_CONV - 1:FFN_CONV, :] * up
        for k in range(1, FFN_CONV):
            shifted = pltpu.roll(up, k * stride, 0)
            if stride == 1:
                for r in range(k):
                    shifted = jnp.where(row == r, hist[keep - k + r:keep - k + r + 1, :], shifted)
            else:
                head = hist[keep - k * stride:, :]
                head = jnp.concatenate([head] + [jnp.zeros((tm - k * stride, head.shape[1]), F32)], axis=0)
                shifted = jnp.where(row < k * stride, head, shifted)
            y = y + cw[FFN_CONV - 1 - k:FFN_CONV - k, :] * shifted
        last = up[tm - keep:, :]
        nbuf_ref[0] = last
        if tiles_per_seq > 1:
            carry_ref[f, idx] = last
        halves.append(y)

    act = (_gelu_tanh(halves[0]) * halves[1]).astype(BF16)
    acc_ref[...] += jnp.dot(act, wd_ref[...], preferred_element_type=F32)

    @pl.when(f == pl.num_programs(1) - 1)
    def _():
        o_ref[...] = x_ref[...] + _rms(acc_ref[...], NORM_EPS) * gpost_ref[...]


def _ffn(x, gpre, w_up, conv_w, buf, w_down, gpost, layer, *, stride, rows_per_seq, tf=512):
    rows, d = x.shape
    tm = min(ROW_TILE, rows_per_seq)
    tiles_per_seq = rows_per_seq // tm
    nf = D_FF // tf
    keep = (FFN_CONV - 1) * stride
    pad = max(SUBLANES, keep)
    carry_rows = keep if tiles_per_seq > 1 else SUBLANES
    kern = functools.partial(_ffn_kernel, tm=tm, stride=stride, tiles_per_seq=tiles_per_seq)
    buf_a = pl.BlockSpec((1, keep, tf), lambda i, f: (i // tiles_per_seq, 0, f))
    buf_b = pl.BlockSpec((1, keep, tf), lambda i, f: (i // tiles_per_seq, 0, nf + f))
    y, nbuf_a, nbuf_b = pl.pallas_call(
        kern,
        grid=(rows // tm, nf),
        in_specs=[pl.BlockSpec((tm, d), lambda i, f: (i, 0)),
                  pl.BlockSpec((1, d), lambda i, f: (0, 0)),
                  pl.BlockSpec((None, d, tf), lambda i, f: (layer, 0, f)),
                  pl.BlockSpec((None, d, tf), lambda i, f: (layer, 0, nf + f)),
                  pl.BlockSpec((FFN_CONV, tf), lambda i, f: (0, f)),
                  pl.BlockSpec((FFN_CONV, tf), lambda i, f: (0, nf + f)),
                  buf_a, buf_b,
                  pl.BlockSpec((None, tf, d), lambda i, f: (layer, f, 0)),
                  pl.BlockSpec((1, d), lambda i, f: (0, 0))],
        out_specs=[pl.BlockSpec((tm, d), lambda i, f: (i, 0)),
                   pl.BlockSpec((1, keep, tf), lambda i, f: (i, 0, f)),
                   pl.BlockSpec((1, keep, tf), lambda i, f: (i, 0, f))],
        out_shape=[jax.ShapeDtypeStruct((rows, d), F32),
                   jax.ShapeDtypeStruct((rows // tm, keep, D_FF), F32),
                   jax.ShapeDtypeStruct((rows // tm, keep, D_FF), F32)],
        scratch_shapes=[pltpu.VMEM((tm, d), BF16),
                        pltpu.VMEM((tm, d), F32),
                        pltpu.VMEM((nf, 2, carry_rows, tf), F32)],
        compiler_params=_cparams(2),
        name="conv_ffn",
    )(x, gpre, w_up, w_up, conv_w, conv_w, buf, buf, w_down, gpost)
    nbuf = jnp.concatenate([nbuf_a, nbuf_b], axis=-1)
    return y, nbuf[tiles_per_seq - 1::tiles_per_seq]


def _s5_kernel(u_ref, h0_ref, bst_ref, kst_ref, cst_ref, apow_ref, y_ref, hfin_ref, e_ref,
               *, n_seq, n_chunks):
    rows = n_seq * n_chunks
    u = u_ref[0]
    h0 = h0_ref[0]
    half = S5_STATE
    if n_chunks > 1:
        row = lax.broadcasted_iota(jnp.int32, (rows, 2 * half), 0)
        placed = jnp.zeros((rows, 2 * half), F32)
        for b in range(n_seq):
            placed = jnp.where(row == b * n_chunks, h0[b:b + 1, :], placed)
        h0 = placed

    def cmul(k, x):
        return x * apow_ref[0, 2 * k:2 * k + 1, :] + pltpu.roll(x, half, 1) * apow_ref[0, 2 * k + 1:2 * k + 2, :]

    e = _mm2(u, bst_ref[0]) + cmul(0, h0)
    if n_chunks > 1:
        j = lax.broadcasted_iota(jnp.int32, (rows, 2 * half), 0) % n_chunks
        k, sh = 0, 1
        while sh < n_chunks:
            e = e + jnp.where(j >= sh, cmul(k, pltpu.roll(e, sh, 0)), 0.0)
            k, sh = k + 1, sh * 2
        e_ref[...] = e
        hfin_ref[0] = e_ref[pl.ds(n_chunks - 1, n_seq, stride=n_chunks), :]
        h_start = jnp.where(j >= 1, pltpu.roll(e, 1, 0), 0.0) + h0
    else:
        hfin_ref[0] = e
        h_start = h0
    y_ref[0] = _mm2(u, kst_ref[0]) + _mm3(h_start, cst_ref[0])


def _s5_scan(u_g, h0_g, bst, kst, cst, apow, *, n_seq, n_chunks):
    groups, rows, cw = u_g.shape
    kern = functools.partial(_s5_kernel, n_seq=n_seq, n_chunks=n_chunks)
    spec3 = lambda a: pl.BlockSpec((1,) + a.shape[1:], lambda g: (g, 0, 0))
    return pl.pallas_call(
        kern,
        grid=(groups,),
        in_specs=[spec3(u_g), spec3(h0_g), spec3(bst), spec3(kst), spec3(cst), spec3(apow)],
        out_specs=[pl.BlockSpec((1, rows, cw), lambda g: (g, 0, 0)),
                   pl.BlockSpec((1, n_seq, 2 * S5_STATE), lambda g: (g, 0, 0))],
        out_shape=[jax.ShapeDtypeStruct((groups, rows, cw), F32),
                   jax.ShapeDtypeStruct((groups, n_seq, 2 * S5_STATE), F32)],
        scratch_shapes=[pltpu.VMEM((rows, 2 * S5_STATE), F32)],
        compiler_params=_cparams(1),
        name="s5_scan",
    )(u_g, h0_g, bst, kst, cst, apow)


def _s5_glu_kernel(y_ref, u_ref, d_ref, w_ref, o_ref):
    g = _gelu_tanh(y_ref[...] + d_ref[...] * u_ref[...])
    o_ref[...] = (g * _sigmoid(_mm(g, w_ref[...]))).astype(BF16)


def _s5_glu(y, u, d, w_glu):
    rows, bw = y.shape
    tm = min(ROW_TILE, rows)
    row_spec = pl.BlockSpec((tm, bw), lambda i: (i, 0))
    return pl.pallas_call(
        _s5_glu_kernel,
        grid=(rows // tm,),
        in_specs=[row_spec, row_spec,
                  pl.BlockSpec((1, bw), lambda i: (0, 0)),
                  pl.BlockSpec((bw, bw), lambda i: (0, 0))],
        out_specs=row_spec,
        out_shape=jax.ShapeDtypeStruct((rows, bw), BF16),
        compiler_params=_cparams(1),
        name="s5_glu",
    )(y, u, d, w_glu)


def _s5_weights(a_re, a_im, log_dt, b_re, b_im, c_re, c_im, c):
    lam = lax.complex(a_re.astype(F32), a_im.astype(F32))
    ldt = lam * jnp.exp(log_dt.astype(F32))[:, None]
    a_bar = jnp.exp(ldt)
    b_bar = ((a_bar - 1.0) / lam)[..., None] * lax.complex(b_re.astype(F32), b_im.astype(F32))
    cm = lax.complex(c_re.astype(F32), c_im.astype(F32))
    tau = jnp.arange(c + 1, dtype=F32)
    apw = jnp.exp(ldt[None] * tau[:, None, None])
    bst = apw[:c][::-1][:, :, :, None] * b_bar[None]
    bst = jnp.transpose(bst, (1, 0, 3, 2)).reshape(S5_GROUPS, c * S5_GROUP, S5_STATE)
    bst = jnp.concatenate([jnp.real(bst), jnp.imag(bst)], axis=-1)
    kt = jnp.real(jnp.einsum('ghp,tgp,gpk->tgkh', cm, apw[:c], b_bar))
    s_idx = jnp.arange(c)[:, None, None]
    t_idx = jnp.arange(c)[None, :, None]
    lag = (t_idx - s_idx == jnp.arange(c)[None, None, :]).astype(F32)
    kst = jnp.einsum('stu,ugkh->stgkh', lag, kt, precision=lax.Precision.HIGHEST)
    kst = jnp.transpose(kst, (2, 0, 3, 1, 4)).reshape(S5_GROUPS, c * S5_GROUP, c * S5_GROUP)
    ca = cm[None] * apw[1:c + 1][:, :, None, :]
    ca = jnp.transpose(ca, (1, 3, 0, 2)).reshape(S5_GROUPS, S5_STATE, c * S5_GROUP)
    cst = jnp.concatenate([jnp.real(ca), -jnp.imag(ca)], axis=1)
    rows = []
    k = 0
    while True:
        p = jnp.exp(ldt * float(c * 2 ** k))
        rows.append(jnp.concatenate([jnp.real(p), jnp.real(p)], axis=-1))
        rows.append(jnp.concatenate([-jnp.imag(p), jnp.imag(p)], axis=-1))
        k += 1
        if c * 2 ** k > 4096:
            break
    apow = jnp.stack(rows, axis=1)
    return bst, kst, cst, apow


def _s5_branch(u3, h0_re, h0_im, sw, d_skip, w_glu, *, c):
    n_seq, t, _ = u3.shape
    n_chunks = t // c
    bst, kst, cst, apow = sw
    u2 = u3.reshape(n_seq * t, BRANCH_W)
    u_g = u2.astype(BF16).reshape(n_seq * t * S5_GROUP, S5_GROUPS).T
    u_g = u_g.reshape(S5_GROUPS, n_seq * n_chunks, c * S5_GROUP)
    h0 = jnp.concatenate([h0_re, h0_im], axis=-1)
    h0 = jnp.transpose(h0, (1, 0, 2))
    y_g, hfin = _s5_scan(u_g, h0, bst, kst, cst, apow, n_seq=n_seq, n_chunks=n_chunks)
    y = y_g.reshape(S5_GROUPS, n_seq * t * S5_GROUP).T.reshape(n_seq * t, BRANCH_W)
    o = _s5_glu(y, u2, d_skip, w_glu)
    hfin = jnp.transpose(hfin, (1, 0, 2))
    return o.reshape(n_seq, t, BRANCH_W), hfin[..., :S5_STATE], hfin[..., S5_STATE:]


def _mixer_grid(n_seq, t, stacked):
    if stacked:
        assert t == SEQ_ROWS and n_seq % SEQ_BLOCK == 0
        c = SEQ_BLOCK * SEQ_ROWS
        shape = lambda w: (SEQ_BLOCK, SEQ_ROWS, w)
        imap = lambda blk: (lambda b: (b, 0, blk))
        smap = lambda nd: (lambda b: (b,) + (0,) * (nd - 1))
        return (n_seq // SEQ_BLOCK,), c, SEQ_ROWS, shape, imap, smap, SEQ_BLOCK
    assert t % PROMPT_CHUNK == 0
    c = PROMPT_CHUNK
    shape = lambda w: (1, c, w)
    imap = lambda blk: (lambda b, i: (b, i, blk))
    smap = lambda nd: (lambda b, i: (b,) + (0,) * (nd - 1))
    return (n_seq, t // c), c, c, shape, imap, smap, 1


def _layer_state_spec(s_all, layer, nblk, n_grid):
    tail = (0,) * (s_all.ndim - 2)
    imap = (lambda b: (layer, b) + tail) if n_grid == 1 else (lambda b, i: (layer, b) + tail)
    return pl.BlockSpec((None, nblk) + s_all.shape[2:], imap)


def _const_spec(a, n_grid):
    zeros = (0,) * a.ndim
    return pl.BlockSpec(a.shape, (lambda b: zeros) if n_grid == 1 else (lambda b, i: zeros))


def _gdn_kernel(pq_ref, pk_ref, pv_ref, pz_ref, pba_ref, cw_ref, cbuf_ref, alog_ref, dtb_ref,
                ng_ref, s0_ref, acc_ref, o_ref, sout_ref, cout_ref, *scratch, c, seg, tv):
    stacked = seg < c
    keep = GDN_CONV - 1
    x = jnp.concatenate([_rows(pq_ref, c), _rows(pk_ref, c), _rows(pv_ref, c)], axis=1)
    if stacked:
        (conv_ref,) = scratch
        t_in = _token_rows(c, 3 * BRANCH_W)
        x = jnp.where(t_in < keep, _rows(cbuf_ref, c), x)
        conv_ref[0:SUBLANES, :] = jnp.zeros((SUBLANES, 3 * BRANCH_W), F32)
        cout_ref[...] = x.reshape(cout_ref.shape)
    else:
        s_ref, conv_ref = scratch

        @pl.when(pl.program_id(1) == 0)
        def _():
            s_ref[...] = s0_ref[0]
            conv_ref[SUBLANES - keep:SUBLANES, :] = cbuf_ref[0]

    conv_ref[SUBLANES:SUBLANES + c, :] = x
    cw = cw_ref[...]
    y = None
    for j in range(GDN_CONV):
        off = SUBLANES - keep + j
        term = cw[j:j + 1, :] * conv_ref[off:off + c, :]
        y = term if y is None else y + term
    if not stacked:
        last = conv_ref[SUBLANES + c - keep:SUBLANES + c, :]
        conv_ref[SUBLANES - keep:SUBLANES, :] = last
        cout_ref[0] = last
    act = _silu(y)

    ba = _rows(pba_ref, c)
    beta_all = _sigmoid(ba)
    g_all = -jnp.exp(alog_ref[...]) * _softplus(ba + dtb_ref[...])
    if stacked:
        t_in = _token_rows(c, LANES)
        valid = jnp.logical_and(t_in >= PRE_ROWS, t_in < PRE_ROWS + tv)
        beta_all = jnp.where(valid, beta_all, 0.0)
        g_all = jnp.where(valid, g_all, 0.0)
    causal, strict = _seg_masks(c, seg)
    d_all = _mm3(causal.astype(F32), g_all)
    dt_all = d_all.T
    ng = ng_ref[...]
    z = _rows(pz_ref, c)
    n_seg = c // seg

    heads = range(GDN_HEADS)
    sls = [slice(h * GDN_DH, (h + 1) * GDN_DH) for h in heads]
    segs = [slice(j * seg, (j + 1) * seg) for j in range(n_seg)]
    cat0 = lambda xs: xs[0] if len(xs) == 1 else jnp.concatenate(xs, axis=0)
    state = lambda j, h: s0_ref[j, h] if stacked else s_ref[h]
    q = [act[:, sl] for sl in sls]
    k = [act[:, BRANCH_W + h * GDN_DH:BRANCH_W + (h + 1) * GDN_DH] for h in heads]
    v = [act[:, 2 * BRANCH_W + h * GDN_DH:2 * BRANCH_W + (h + 1) * GDN_DH] for h in heads]
    q = [x * lax.rsqrt(jnp.sum(x * x, axis=-1, keepdims=True) + 1e-6) * (GDN_DH ** -0.5) for x in q]
    k = [x * lax.rsqrt(jnp.sum(x * x, axis=-1, keepdims=True) + 1e-6) for x in k]
    beta = [beta_all[:, h:h + 1] for h in heads]
    d = [d_all[:, GDN_HEADS + h:GDN_HEADS + h + 1] for h in heads]
    d_row = [dt_all[GDN_HEADS + h:GDN_HEADS + h + 1, :] for h in heads]
    decay = [jnp.where(causal, jnp.exp(jnp.where(causal, d[h] - d_row[h], 0.0)), 0.0) for h in heads]
    kb = [k[h] * beta[h] for h in heads]
    prod = [_mm_nt(jnp.concatenate([kb[h], q[h]], axis=0), k[h]) for h in heads]
    m = [jnp.where(strict, prod[h][:c] * decay[h], 0.0) for h in heads]
    attn = [prod[h][c:] * decay[h] for h in heads]
    t_inv = _unit_lower_inverses(m, seg)
    ed = [jnp.exp(x) for x in d]
    sol = [_mm(t_inv[h], jnp.concatenate([v[h] * beta[h], kb[h] * ed[h]], axis=1)) for h in heads]
    qe = [q[h] * ed[h] for h in heads]
    both = [[_mm(jnp.concatenate([sol[h][rows, GDN_DH:], qe[h][rows]], axis=0), state(j, h))
             for j, rows in enumerate(segs)] for h in heads]
    u = [cat0([sol[h][rows, :GDN_DH] - both[h][j][:seg] for j, rows in enumerate(segs)]) for h in heads]
    qs = [cat0([b[seg:] for b in both[h]]) for h in heads]
    o = [qs[h] + _mm(attn[h], u[h]) for h in heads]
    for h in heads:
        for j, rows in enumerate(segs):
            dl = d[h][(j + 1) * seg - 1:(j + 1) * seg, :]
            s_new = state(j, h) * jnp.exp(dl) + _mm_tn(k[h][rows] * jnp.exp(dl - d[h][rows]), u[h][rows])
            if stacked:
                sout_ref[j, h] = s_new
            else:
                s_ref[h] = s_new
    outs = [_rms(o[h], NORM_EPS) * ng * _silu(z[:, sls[h]]) for h in heads]
    o_ref[...] = jnp.concatenate(outs, axis=1).reshape(o_ref.shape).astype(BF16)

    if not stacked:
        @pl.when(pl.program_id(1) == pl.num_programs(1) - 1)
        def _():
            sout_ref[0] = s_ref[...]


def _gdn(proj3, conv_w, cbuf, alog, dtb, ng, s_all, acc, layer, *, stacked, tv):
    n_seq, t, _ = proj3.shape
    grid, c, seg, shape, imap, smap, nblk = _mixer_grid(n_seq, t, stacked)
    ng_ = len(grid)
    kern = functools.partial(_gdn_kernel, c=c, seg=seg, tv=tv)
    col = lambda blk, w: pl.BlockSpec(shape(w), imap(blk))
    cbuf_spec = pl.BlockSpec((nblk,) + cbuf.shape[1:], smap(3))
    scratch = [pltpu.VMEM((SUBLANES + c, 3 * BRANCH_W), F32)]
    if not stacked:
        scratch = [pltpu.VMEM((GDN_HEADS, GDN_DH, GDN_DH), F32)] + scratch
    return pl.pallas_call(
        kern,
        grid=grid,
        in_specs=[col(COL_GQ, BRANCH_W), col(COL_GK, BRANCH_W), col(COL_GV, BRANCH_W),
                  col(COL_GZ, BRANCH_W), col(COL_GBA, LANES),
                  _const_spec(conv_w, ng_), cbuf_spec, _const_spec(alog, ng_), _const_spec(dtb, ng_),
                  _const_spec(ng, ng_), _layer_state_spec(s_all, layer, nblk, ng_),
                  pl.BlockSpec(memory_space=pl.ANY)],
        out_specs=[col(0, BRANCH_W), _layer_state_spec(acc, layer, nblk, ng_), cbuf_spec],
        out_shape=[jax.ShapeDtypeStruct((n_seq, t, BRANCH_W), BF16),
                   jax.ShapeDtypeStruct(acc.shape, F32),
                   jax.ShapeDtypeStruct(cbuf.shape, F32)],
        input_output_aliases={11: 1},
        scratch_shapes=scratch,
        compiler_params=_cparams(ng_),
        name="gdn",
    )(proj3, proj3, proj3, proj3, proj3, conv_w, cbuf, alog, dtb, ng, s_all, acc)


def _rwkv_kernel(pr_ref, pk_ref, pv_ref, pwl_ref, pal_ref, pgl_ref, prev_ref, mu_ref, w0_ref, w2_ref,
                 a0_ref, a2_ref, g2_ref, kk_ref, ka_ref, rk_ref, lng_ref, lnb_ref, s0_ref, acc_ref,
                 o_ref, sout_ref, pout_ref, *scratch, c, seg, tv):
    stacked = seg < c
    p = jnp.concatenate([_rows(pr_ref, c), _rows(pk_ref, c), _rows(pv_ref, c), _rows(pwl_ref, c),
                         _rows(pal_ref, c), _rows(pgl_ref, c)], axis=1)
    if stacked:
        (p_ref,) = scratch
        t_in = _token_rows(c, RWKV_PAD_COLS)
        p = jnp.where(t_in == PRE_ROWS - 1, _rows(prev_ref, c), p)
        p_ref[0:SUBLANES, :] = jnp.zeros((SUBLANES, RWKV_PAD_COLS), F32)
        pout_ref[...] = p.reshape(pout_ref.shape)
    else:
        s_ref, p_ref = scratch

        @pl.when(pl.program_id(1) == 0)
        def _():
            s_ref[...] = s0_ref[0]
            p_ref[SUBLANES - 1:SUBLANES, :] = prev_ref[0]

    p_ref[SUBLANES:SUBLANES + c, :] = p
    prev = p_ref[SUBLANES - 1:SUBLANES - 1 + c, :]
    if not stacked:
        last = p_ref[SUBLANES + c - 1:SUBLANES + c, :]
        p_ref[SUBLANES - 1:SUBLANES, :] = last
        pout_ref[0] = last
    pm = p + (prev - p) * mu_ref[...]
    r = pm[:, :BRANCH_W]
    k = pm[:, BRANCH_W:2 * BRANCH_W]
    v = pm[:, 2 * BRANCH_W:3 * BRANCH_W]
    wl = pm[:, 3 * BRANCH_W:3 * BRANCH_W + LANES]
    al = pm[:, 3 * BRANCH_W + LANES:3 * BRANCH_W + 2 * LANES]
    gl = pm[:, 3 * BRANCH_W + 2 * LANES:]
    w_log = -_softplus(-(w0_ref[...] + _mm(jnp.tanh(wl), w2_ref[...]))) - 0.5
    lw = -jnp.exp(w_log)
    a = _sigmoid(a0_ref[...] + _mm(al, a2_ref[...]))
    gate = _mm(_sigmoid(gl), g2_ref[...])
    kk_un = k * kk_ref[...]
    k2 = k * (1.0 + (a - 1.0) * ka_ref[...])
    if stacked:
        t_in = _token_rows(c, BRANCH_W)
        valid = jnp.logical_and(t_in >= PRE_ROWS, t_in < PRE_ROWS + tv)
        lw = jnp.where(valid, lw, 0.0)
        kk_un = jnp.where(valid, kk_un, 0.0)
        k2 = jnp.where(valid, k2, 0.0)
    causal, _ = _seg_masks(c, seg)
    causal2, strict2 = _seg_masks(c, seg, reps=2)
    g_cum = _mm3(causal.astype(F32), lw)
    e_pos = jnp.exp(g_cum)
    e_neg = jnp.exp(-g_cum)
    e_prev = jnp.exp(g_cum - lw)
    rk_w = rk_ref[...]
    lng = lng_ref[...]
    lnb = lnb_ref[...]
    n_seg = c // seg

    heads = range(RWKV_HEADS)
    sls = [slice(h * RWKV_DH, (h + 1) * RWKV_DH) for h in heads]
    segs = [slice(j * seg, (j + 1) * seg) for j in range(n_seg)]
    cat0 = lambda xs: xs[0] if len(xs) == 1 else jnp.concatenate(xs, axis=0)
    state = lambda j, h: s0_ref[j, h] if stacked else s_ref[h]
    kkh = [kk_un[:, sl] for sl in sls]
    kkh = [x * lax.rsqrt(jnp.sum(x * x, axis=-1, keepdims=True) + 1e-6) for x in kkh]
    alpha_hat = [kkh[h] * a[:, sls[h]] * e_neg[:, sls[h]] for h in heads]
    k_hat = [k2[:, sl] * e_neg[:, sl] for sl in sls]
    kap = [kkh[h] * e_prev[:, sls[h]] for h in heads]
    r_t = [r[:, sl] * e_pos[:, sl] for sl in sls]
    vh = [v[:, sl] for sl in sls]
    rhs = [jnp.concatenate([alpha_hat[h], k_hat[h]], axis=0) for h in heads]
    ab = [jnp.where(strict2, _mm_nt(kap[h], rhs[h]), 0.0) for h in heads]
    rr = [jnp.where(causal2, _mm_nt(r_t[h], rhs[h]), 0.0) for h in heads]
    both = [[_mm_nt(jnp.concatenate([kap[h][rows], r_t[h][rows]], axis=0), state(j, h))
             for j, rows in enumerate(segs)] for h in heads]
    bv = [_mm(jnp.concatenate([ab[h][:, c:], rr[h][:, c:]], axis=0), vh[h]) for h in heads]
    t_inv = _unit_lower_inverses([ab[h][:, :c] for h in heads], seg)
    ks = [cat0([b[:seg] for b in both[h]]) for h in heads]
    rs = [cat0([b[seg:] for b in both[h]]) for h in heads]
    u = [_mm(t_inv[h], ks[h] + bv[h][:c]) for h in heads]
    y = [rs[h] + bv[h][c:] - _mm(rr[h][:, :c], u[h]) for h in heads]
    for h in heads:
        for j, rows in enumerate(segs):
            el = e_pos[(j + 1) * seg - 1:(j + 1) * seg, sls[h]]
            s_new = state(j, h) * el + _mm_tn(
                jnp.concatenate([vh[h][rows], -u[h][rows]], axis=0),
                jnp.concatenate([k_hat[h][rows] * el, alpha_hat[h][rows] * el], axis=0))
            if stacked:
                sout_ref[j, h] = s_new
            else:
                s_ref[h] = s_new
    outs = []
    for h in heads:
        sl = sls[h]
        mu = jnp.mean(y[h], axis=-1, keepdims=True)
        yc = y[h] - mu
        var = jnp.mean(yc * yc, axis=-1, keepdims=True)
        yn = yc * lax.rsqrt(var + RWKV_LN_EPS) * lng[:, sl] + lnb[:, sl]
        bonus = jnp.sum(r[:, sl] * k2[:, sl] * rk_w[:, sl], axis=-1, keepdims=True) * vh[h]
        outs.append(yn + bonus)
    o_ref[...] = (jnp.concatenate(outs, axis=1) * gate).reshape(o_ref.shape).astype(BF16)

    if not stacked:
        @pl.when(pl.program_id(1) == pl.num_programs(1) - 1)
        def _():
            sout_ref[0] = s_ref[...]


def _rwkv(proj3, prev, mu, w0, w2, a0, a2, g2, k_k, k_a, r_k, ln_g, ln_b, s_all, acc, layer, *,
          stacked, tv):
    n_seq, t, _ = proj3.shape
    grid, c, seg, shape, imap, smap, nblk = _mixer_grid(n_seq, t, stacked)
    ng_ = len(grid)
    kern = functools.partial(_rwkv_kernel, c=c, seg=seg, tv=tv)
    col = lambda blk, w: pl.BlockSpec(shape(w), imap(blk))
    cs = lambda a: _const_spec(a, ng_)
    prev_spec = pl.BlockSpec((nblk,) + prev.shape[1:], smap(3))
    scratch = [pltpu.VMEM((SUBLANES + c, RWKV_PAD_COLS), F32)]
    if not stacked:
        scratch = [pltpu.VMEM((RWKV_HEADS, RWKV_DH, RWKV_DH), F32)] + scratch
    return pl.pallas_call(
        kern,
        grid=grid,
        in_specs=[col(COL_RR, BRANCH_W), col(COL_RK, BRANCH_W), col(COL_RV, BRANCH_W),
                  col(COL_RWL, LANES), col(COL_RAL, LANES), col(COL_RGL, 2 * LANES),
                  prev_spec, cs(mu), cs(w0), cs(w2), cs(a0), cs(a2), cs(g2),
                  cs(k_k), cs(k_a), cs(r_k), cs(ln_g), cs(ln_b),
                  _layer_state_spec(s_all, layer, nblk, ng_), pl.BlockSpec(memory_space=pl.ANY)],
        out_specs=[col(0, BRANCH_W), _layer_state_spec(acc, layer, nblk, ng_), prev_spec],
        out_shape=[jax.ShapeDtypeStruct((n_seq, t, BRANCH_W), BF16),
                   jax.ShapeDtypeStruct(acc.shape, F32),
                   jax.ShapeDtypeStruct(prev.shape, F32)],
        input_output_aliases={19: 1},
        scratch_shapes=scratch,
        compiler_params=_cparams(ng_),
        name="rwkv7",
    )(proj3, proj3, proj3, proj3, proj3, proj3, prev, mu, w0, w2, a0, a2, g2, k_k, k_a, r_k,
      ln_g, ln_b, s_all, acc)


def _hgrn_kernel(pq_ref, pf_ref, pi_ref, pg_ref, lb_ref, ng_ref, s0_ref, acc_ref, o_ref, sout_ref, *scratch,
                 c, seg, tv):
    stacked = seg < c
    if not stacked:
        (s_ref,) = scratch

        @pl.when(pl.program_id(1) == 0)
        def _():
            s_ref[...] = s0_ref[0]

    lb = lb_ref[...]
    hf = _rows(pf_ref, c)
    log_sig = jnp.minimum(hf, 0.0) - jnp.log1p(jnp.exp(-jnp.abs(hf)))
    x1 = jnp.log(jnp.maximum(lb, LB_TINY))
    x2 = jnp.log1p(-lb) + log_sig
    logf = jnp.maximum(x1, x2) + jnp.log1p(jnp.exp(-jnp.abs(x1 - x2)))
    kf = (1.0 - lb) * _sigmoid(-hf)
    q = _silu(_rows(pq_ref, c))
    v = _rows(pi_ref, c)
    if stacked:
        t_in = _token_rows(c, BRANCH_W)
        valid = jnp.logical_and(t_in >= PRE_ROWS, t_in < PRE_ROWS + tv)
        logf = jnp.where(valid, logf, 0.0)
        kf = jnp.where(valid, kf, 0.0)
    sub = seg if stacked else HGRN_SUB
    n_sub = c // sub
    blk_causal, _ = _seg_masks(c, sub)
    b_all = _mm3(blk_causal.astype(F32), logf)
    causal, _ = _seg_masks(sub, sub)
    mid = sub // 2 - 1
    b_last_rows = jnp.concatenate([b_all[(j + 1) * sub - 1:(j + 1) * sub, :] for j in range(n_sub)]
                                  + [jnp.zeros((-n_sub % SUBLANES, BRANCH_W), F32)] * (n_sub % SUBLANES > 0),
                                  axis=0)
    decay_cols = jnp.exp(b_last_rows).T
    ng = ng_ref[...]
    gate = _silu(_rows(pg_ref, c))

    heads = range(HGRN_HEADS)
    sls = [slice(h * HGRN_DH, (h + 1) * HGRN_DH) for h in heads]
    pairs = [(j, h) for j in range(n_sub) for h in heads]
    blk = lambda x, j, h: x[j * sub:(j + 1) * sub, sls[h]]
    e_mid, e_last = {}, {}
    for j, h in pairs:
        b = blk(b_all, j, h)
        e_mid[j, h] = b - b[mid:mid + 1, :]
        e_last[j, h] = b[sub - 1:sub, :] - b
    attn = {jh: jnp.where(causal, _mm_nt(blk(q, *jh) * jnp.exp(jnp.minimum(e_mid[jh], EXP_CLAMP)),
                                         blk(kf, *jh) * jnp.exp(jnp.minimum(-e_mid[jh], EXP_CLAMP))), 0.0)
            for jh in pairs}
    intra = {jh: _mm(attn[jh], blk(v, *jh)) for jh in pairs}
    q_dec = {jh: blk(q, *jh) * jnp.exp(blk(b_all, *jh)) for jh in pairs}
    kv = {jh: _mm_tn(blk(kf, *jh) * jnp.exp(e_last[jh]), blk(v, *jh)) for jh in pairs}
    outs = [[] for _ in heads]
    for j in range(n_sub):
        s = [s0_ref[j, h] if stacked else s_ref[h] for h in heads]
        o = [_mm(q_dec[j, h], s[h]) + intra[j, h] for h in heads]
        for h in heads:
            s_new = s[h] * decay_cols[sls[h], j:j + 1] + kv[j, h]
            if stacked:
                sout_ref[j, h] = s_new
            else:
                s_ref[h] = s_new
            outs[h].append(o[h])
    cols = []
    for h in range(HGRN_HEADS):
        sl = slice(h * HGRN_DH, (h + 1) * HGRN_DH)
        o = outs[h][0] if n_sub == 1 else jnp.concatenate(outs[h], axis=0)
        cols.append(_rms(o, NORM_EPS) * ng * gate[:, sl])
    o_ref[...] = jnp.concatenate(cols, axis=1).reshape(o_ref.shape).astype(BF16)

    if not stacked:
        @pl.when(pl.program_id(1) == pl.num_programs(1) - 1)
        def _():
            sout_ref[0] = s_ref[...]


def _hgrn(proj3, lb, ng, s_all, acc, layer, *, stacked, tv):
    n_seq, t, _ = proj3.shape
    grid, c, seg, shape, imap, smap, nblk = _mixer_grid(n_seq, t, stacked)
    ng_ = len(grid)
    kern = functools.partial(_hgrn_kernel, c=c, seg=seg, tv=tv)
    col = lambda blk: pl.BlockSpec(shape(BRANCH_W), imap(blk))
    scratch = [] if stacked else [pltpu.VMEM((HGRN_HEADS, HGRN_DH, HGRN_DH), F32)]
    return pl.pallas_call(
        kern,
        grid=grid,
        in_specs=[col(COL_HQ), col(COL_HF), col(COL_HI), col(COL_HG),
                  _const_spec(lb, ng_), _const_spec(ng, ng_),
                  _layer_state_spec(s_all, layer, nblk, ng_), pl.BlockSpec(memory_space=pl.ANY)],
        out_specs=[col(0), _layer_state_spec(acc, layer, nblk, ng_)],
        out_shape=[jax.ShapeDtypeStruct((n_seq, t, BRANCH_W), BF16),
                   jax.ShapeDtypeStruct(acc.shape, F32)],
        input_output_aliases={7: 1},
        scratch_shapes=scratch,
        compiler_params=_cparams(ng_),
        name="hgrn2",
    )(proj3, proj3, proj3, proj3, lb, ng, s_all, acc)


def _pad_cols(a, width):
    return jnp.pad(a, [(0, 0)] * (a.ndim - 1) + [(0, width - a.shape[-1])])


def _rwkv_cols_to_padded(a):
    o = 3 * BRANCH_W
    return jnp.concatenate([a[..., :o],
                            _pad_cols(a[..., o:o + RWKV_W_LORA], LANES),
                            _pad_cols(a[..., o + RWKV_W_LORA:o + RWKV_W_LORA + RWKV_A_LORA], LANES),
                            a[..., o + RWKV_W_LORA + RWKV_A_LORA:]], axis=-1)


def _rwkv_cols_from_padded(a):
    o = 3 * BRANCH_W
    return jnp.concatenate([a[..., :o], a[..., o:o + RWKV_W_LORA],
                            a[..., o + LANES:o + LANES + RWKV_A_LORA], a[..., o + 2 * LANES:]], axis=-1)


def _layout_w_in(w):
    s5 = _s5_perm(w[..., :512], -1)
    gdn = w[..., 512:2560]
    ba = w[..., 2560:2568]
    rwkv = w[..., 2568:4552]
    hgrn = w[..., 4552:6600]
    rp = _rwkv_cols_to_padded(rwkv)
    out = jnp.concatenate([s5, gdn, rp[..., :1536], hgrn, rp[..., 1792:2048], _pad_cols(ba, LANES),
                           rp[..., 1536:1664], rp[..., 1664:1792]], axis=-1)
    return _pad_cols(out, PROJ_COLS)


def _prep_big(w):
    return dict(
        w_in=_layout_w_in(w['w_in'].astype(BF16)),
        w_gate=w['w_gate'].astype(BF16),
        w_br=w['w_br'].astype(BF16),
        w_br0=_s5_perm(w['w_br'][:, 0], 1).astype(BF16),
        w_o=w['w_o'].astype(BF16),
        w_up=w['w_up'].astype(BF16),
        w_down=w['w_down'].astype(BF16),
    )


def _lane_vec(a, offset):
    return jnp.zeros((1, LANES), F32).at[0, offset:offset + a.shape[0]].set(a.astype(F32))


def _prep_layer(l, w, s5_chunks):
    row = lambda a: a.astype(F32).reshape(1, -1)
    p = dict(
        g_pre_mix=row(w['g_pre_mix'][l]),
        g_post_mix=row(w['g_post_mix'][l]),
        s5_d=_s5_perm(row(w['s5_d'][l]), 1),
        s5_w_glu=_s5_perm(_s5_perm(w['s5_w_glu'][l], 0), 1).astype(BF16),
        gdn_conv_w=w['gdn_conv_w'][l].astype(F32),
        gdn_alog=_lane_vec(w['gdn_a_log'][l], GDN_HEADS),
        gdn_dtb=_lane_vec(w['gdn_dt_bias'][l], GDN_HEADS),
        gdn_norm_g=row(w['gdn_norm_g'][l]),
        rwkv_mu=_rwkv_cols_to_padded(row(w['rwkv_mu'][l])),
        rwkv_w0=row(w['rwkv_w0'][l]),
        rwkv_w2=jnp.pad(w['rwkv_w2'][l], ((0, LANES - RWKV_W_LORA), (0, 0))).astype(BF16),
        rwkv_a0=row(w['rwkv_a0'][l]),
        rwkv_a2=jnp.pad(w['rwkv_a2'][l], ((0, LANES - RWKV_A_LORA), (0, 0))).astype(BF16),
        rwkv_g2=w['rwkv_g2'][l].astype(BF16),
        rwkv_k_k=row(w['rwkv_k_k'][l]),
        rwkv_k_a=row(w['rwkv_k_a'][l]),
        rwkv_r_k=row(w['rwkv_r_k'][l]),
        rwkv_ln_g=row(w['rwkv_ln_g'][l]),
        rwkv_ln_b=row(w['rwkv_ln_b'][l]),
        hgrn_norm_g=row(w['hgrn_norm_g'][l]),
        g_pre_ffn=row(w['g_pre_ffn'][l]),
        ffn_conv_w=w['ffn_conv_w'][l].astype(F32),
        g_post_ffn=row(w['g_post_ffn'][l]),
    )
    s5_args = (w['s5_a_re'][l], w['s5_a_im'][l], w['s5_log_dt'][l], w['s5_b_re'][l], w['s5_b_im'][l],
               w['s5_c_re'][l], w['s5_c_im'][l])
    p['s5'] = {c: _s5_weights(*s5_args, c) for c in s5_chunks}
    return p


def _layer(x, st, acc, p, big, lb, layer, *, n_seq, t, time_major):
    s5_re, s5_im, gdn_s, gdn_buf, rwkv_s, rwkv_prev, hgrn_s, ffn_buf = st
    acc_gdn, acc_rwkv, acc_hgrn = acc
    rwkv_prev = _rwkv_cols_to_padded(rwkv_prev)
    if time_major:
        lo, hi = PRE_ROWS, PRE_ROWS + t
        pad_rows = lambda a, before: jnp.pad(a, ((0, 0), (before, SEQ_ROWS - before - a.shape[1]), (0, 0)))
        to_seq = lambda a: pad_rows(jnp.transpose(a.reshape(t, n_seq, a.shape[-1]), (1, 0, 2)), lo)
        from_seq = lambda a: jnp.transpose(a, (1, 0, 2)).reshape(t * n_seq, a.shape[-1])
        tokens = lambda a: a[:, lo:hi]
        cbuf = pad_rows(gdn_buf, 0)
        prev = pad_rows(rwkv_prev[:, None, :], lo - 1)
    else:
        to_seq = lambda a: a.reshape(n_seq, t, a.shape[-1])
        from_seq = lambda a: a.reshape(n_seq * t, a.shape[-1])
        tokens = lambda a: a
        cbuf = gdn_buf
        prev = rwkv_prev[:, None, :]

    proj = _norm_matmul(x, p['g_pre_mix'], big['w_in'], layer, tn=PROJ_TILE)
    proj3 = to_seq(proj)

    s5c = min(S5_CHUNK, t)
    o_a, s5_re_n, s5_im_n = _s5_branch(tokens(proj3)[..., :BRANCH_W], s5_re, s5_im, p['s5'][s5c],
                                       p['s5_d'], p['s5_w_glu'], c=s5c)
    o_b, gdn_s_n, cout = _gdn(proj3, p['gdn_conv_w'], cbuf, p['gdn_alog'], p['gdn_dtb'],
                              p['gdn_norm_g'], gdn_s, acc_gdn, layer, stacked=time_major, tv=t)
    o_c, rwkv_s_n, pout = _rwkv(proj3, prev, p['rwkv_mu'], p['rwkv_w0'], p['rwkv_w2'], p['rwkv_a0'],
                                p['rwkv_a2'], p['rwkv_g2'], p['rwkv_k_k'], p['rwkv_k_a'],
                                p['rwkv_r_k'], p['rwkv_ln_g'], p['rwkv_ln_b'], rwkv_s, acc_rwkv, layer,
                                stacked=time_major, tv=t)
    o_d, hgrn_s_n = _hgrn(proj3, lb, p['hgrn_norm_g'], hgrn_s, acc_hgrn, layer,
                          stacked=time_major, tv=t)
    if time_major:
        gdn_buf_n = cout[:, hi - (GDN_CONV - 1):hi]
        rwkv_prev_n = _rwkv_cols_from_padded(pout[:, hi - 1])
    else:
        gdn_buf_n = cout
        rwkv_prev_n = _rwkv_cols_from_padded(pout[:, 0])

    branches = [from_seq(o_a)] + [from_seq(tokens(o)) for o in (o_b, o_c, o_d)]
    mix = _mix(x, p['g_pre_mix'], branches, big['w_gate'], big['w_br0'], big['w_br'], layer)
    x = _matmul_resnorm(mix, big['w_o'], layer, x, p['g_post_mix'])

    if time_major:
        buf = jnp.transpose(ffn_buf, (1, 0, 2)).reshape(1, (FFN_CONV - 1) * n_seq, 2 * D_FF)
        x, nbuf = _ffn(x, p['g_pre_ffn'], big['w_up'], p['ffn_conv_w'], buf, big['w_down'],
                       p['g_post_ffn'], layer, stride=n_seq, rows_per_seq=t * n_seq)
        ffn_buf_n = jnp.transpose(nbuf.reshape(FFN_CONV - 1, n_seq, 2 * D_FF), (1, 0, 2))
    else:
        x, ffn_buf_n = _ffn(x, p['g_pre_ffn'], big['w_up'], p['ffn_conv_w'], ffn_buf, big['w_down'],
                            p['g_post_ffn'], layer, stride=1, rows_per_seq=t)
    return x, (s5_re_n, s5_im_n, gdn_s_n, gdn_buf_n, rwkv_s_n, rwkv_prev_n, hgrn_s_n, ffn_buf_n)


MATRIX_STATES = (2, 4, 6)


def _run_group(x3, states, layers, big, lb_all, *, time_major):
    n_seq, t, d = x3.shape
    if time_major:
        x = jnp.transpose(x3, (1, 0, 2)).reshape(t * n_seq, d)
    else:
        x = x3.reshape(n_seq * t, d)
    new = [[] for _ in states]
    acc = [jnp.zeros(states[i].shape, F32) for i in MATRIX_STATES]
    for l in range(DEPTH):
        st = [s if i in MATRIX_STATES else s[l] for i, s in enumerate(states)]
        x, st = _layer(x, st, acc, layers[l], big, lb_all[l:l + 1], l,
                       n_seq=n_seq, t=t, time_major=time_major)
        acc = [st[i] for i in MATRIX_STATES]
        for lst, s in zip(new, st):
            lst.append(s)
    if time_major:
        y = jnp.transpose(x.reshape(t, n_seq, d), (1, 0, 2))
    else:
        y = x.reshape(n_seq, t, d)
    return y, [acc[MATRIX_STATES.index(i)] if i in MATRIX_STATES else jnp.stack(lst)
               for i, lst in enumerate(new)]


def kernel(x_prompt, x_sample, state_s5_re, state_s5_im, state_gdn, state_gdn_conv, state_rwkv, state_rwkv_shift, state_hgrn, state_ffn_conv, g_pre_mix, w_in, w_gate, w_br, w_o, g_post_mix, s5_a_re, s5_a_im, s5_log_dt, s5_b_re, s5_b_im, s5_c_re, s5_c_im, s5_d, s5_w_glu, gdn_conv_w, gdn_a_log, gdn_dt_bias, gdn_norm_g, rwkv_mu, rwkv_w0, rwkv_w2, rwkv_a0, rwkv_a2, rwkv_g2, rwkv_k_k, rwkv_k_a, rwkv_r_k, rwkv_ln_g, rwkv_ln_b, hgrn_lb_logits, hgrn_norm_g, g_pre_ffn, w_up, ffn_conv_w, w_down, g_post_ffn):
    w = dict(g_pre_mix=g_pre_mix, w_in=w_in, w_gate=w_gate, w_br=w_br, w_o=w_o, g_post_mix=g_post_mix,
             s5_a_re=s5_a_re, s5_a_im=s5_a_im, s5_log_dt=s5_log_dt, s5_b_re=s5_b_re, s5_b_im=s5_b_im,
             s5_c_re=s5_c_re, s5_c_im=s5_c_im, s5_d=s5_d, s5_w_glu=s5_w_glu, gdn_conv_w=gdn_conv_w,
             gdn_a_log=gdn_a_log, gdn_dt_bias=gdn_dt_bias, gdn_norm_g=gdn_norm_g, rwkv_mu=rwkv_mu,
             rwkv_w0=rwkv_w0, rwkv_w2=rwkv_w2, rwkv_a0=rwkv_a0, rwkv_a2=rwkv_a2, rwkv_g2=rwkv_g2,
             rwkv_k_k=rwkv_k_k, rwkv_k_a=rwkv_k_a, rwkv_r_k=rwkv_r_k, rwkv_ln_g=rwkv_ln_g,
             rwkv_ln_b=rwkv_ln_b, hgrn_norm_g=hgrn_norm_g, g_pre_ffn=g_pre_ffn, w_up=w_up,
             ffn_conv_w=ffn_conv_w, w_down=w_down, g_post_ffn=g_post_ffn)
    sm = jax.nn.softmax(hgrn_lb_logits.astype(F32), axis=0)
    lb_all = jnp.maximum(jnp.cumsum(sm, axis=0) - sm[0], 0.0)
    s5_chunks = {min(S5_CHUNK, x_prompt.shape[1]), min(S5_CHUNK, x_sample.shape[1])}
    layers = [_prep_layer(l, w, s5_chunks) for l in range(DEPTH)]
    big = _prep_big(w)
    sample_states = (state_s5_re, state_s5_im, state_gdn, state_gdn_conv, state_rwkv,
                     state_rwkv_shift, state_hgrn, state_ffn_conv)
    nb = x_prompt.shape[0]
    prompt_states = tuple(jnp.zeros((DEPTH, nb) + s.shape[2:], F32) for s in sample_states)
    y_prompt, ps = _run_group(x_prompt, prompt_states, layers, big, lb_all, time_major=False)
    y_sample, ss = _run_group(x_sample, sample_states, layers, big, lb_all, time_major=True)
    out = [y_prompt, y_sample]
    for a, b in zip(ps, ss):
        out.extend((a, b))
    return tuple(out)
```
